```python
import math
import jax
import jax.numpy as jnp
from jax import lax
import numpy as np

D_MODEL = 1024
BATCH = 2
SEQ = 8192
DEPTH = 1

D_MIX = D_MODEL
GDN_HEADS = 4
GDN_HEAD_DIM = 128
GDN_WIDTH = GDN_HEADS * GDN_HEAD_DIM
CONV_WIDTH = 5
CHUNK = 64
SWA_HEADS = 8
SWA_HEAD_DIM = 64
SWA_WIDTH = SWA_HEADS * SWA_HEAD_DIM
DILATION_PATTERNS = ((128, 1), (512, 4), (2048, 16))
BAND_BLOCK = 64
REL_BUCKETS = 32
REL_MAX_DISTANCE = 1024
D_FF = 2816
EPS = 1e-6
NEG_BIG = -1e30
SPLITS = (3 * GDN_WIDTH, GDN_WIDTH, 2 * GDN_HEADS, 2 * GDN_HEADS, 3 * SWA_WIDTH)
N_IN = sum(SPLITS)

kernel_name = "hybrid_gdn_dilated_swa_macaron"


def rms_norm(x, w):
    xf = x.astype(jnp.float32)
    y = xf * lax.rsqrt(jnp.mean(xf * xf, axis=-1, keepdims=True) + EPS)
    return (y * w.astype(jnp.float32)).astype(x.dtype)


def l2_normalize(x):
    return x * lax.rsqrt(jnp.sum(x * x, axis=-1, keepdims=True) + EPS)


def swiglu(x, w_gate, w_up, w_down):
    return (jax.nn.silu(x @ w_gate) * (x @ w_up)) @ w_down


def t5_bucket(rel):
    nb = REL_BUCKETS // 2
    bucket = (rel > 0).astype(np.int32) * nb
    n = np.abs(rel)
    max_exact = nb // 2
    large = max_exact + (np.log(np.maximum(n, 1) / max_exact)
                         / math.log(REL_MAX_DISTANCE / max_exact) * (nb - max_exact)).astype(np.int32)
    large = np.minimum(large, nb - 1)
    return (bucket + np.where(n < max_exact, n, large)).astype(np.int32)


def short_conv(x, w):
    c, kw = w.shape
    rhs = jnp.transpose(w).astype(x.dtype)[:, None, :]
    return lax.conv_general_dilated(x, rhs, window_strides=(1,),
                                    padding=((kw // 2, kw // 2),),
                                    dimension_numbers=('NWC', 'WIO', 'NWC'),
                                    feature_group_count=c)


def gated_delta_chunked(q, k, v, g, beta):
    b, h, t, dk = q.shape
    dv = v.shape[-1]
    nc = t // CHUNK
    q = q * (dk ** -0.5)

    def chunks(a):
        return a.reshape(b, h, nc, CHUNK, *a.shape[3:])

    q, k, v, g, beta = chunks(q), chunks(k), chunks(v), chunks(g), chunks(beta)
    g = jnp.cumsum(g, axis=-1)
    incl = jnp.tril(jnp.ones((CHUNK, CHUNK), dtype=bool))
    strict = jnp.tril(jnp.ones((CHUNK, CHUNK), dtype=bool), -1)
    diff = g[..., :, None] - g[..., None, :]
    decay = jnp.where(incl, jnp.exp(jnp.where(incl, diff, 0.0)), 0.0)
    kb = k * beta[..., None]
    lmat = jnp.where(strict, jnp.einsum('bhncd,bhnjd->bhncj', kb, k) * decay, 0.0)
    eye = jnp.eye(CHUNK, dtype=jnp.float32)
    tmat = lax.linalg.triangular_solve(eye + lmat, jnp.broadcast_to(eye, lmat.shape),
                                       left_side=True, lower=True, unit_diagonal=True)
    u = tmat @ (v * beta[..., None])
    w = tmat @ (kb * jnp.exp(g)[..., None])
    intra = jnp.einsum('bhncd,bhnjd->bhncj', q, k) * decay
    qg = q * jnp.exp(g)[..., None]
    g_last = g[..., -1]
    kdec = k * jnp.exp(g_last[..., None] - g)[..., None]

    def step(state, inp):
        u_c, w_c, qg_c, intra_c, kdec_c, gl_c = inp
        v_new = u_c - w_c @ state
        o_c = qg_c @ state + intra_c @ v_new
        state = state * jnp.exp(gl_c)[..., None, None] + jnp.swapaxes(kdec_c, -1, -2) @ v_new
        return state, o_c

    xs = tuple(jnp.moveaxis(a, 2, 0) for a in (u, w, qg, intra, kdec, g_last))
    state0 = jnp.zeros((b, h, dk, dv), jnp.float32)
    _, o = lax.scan(step, state0, xs)
    return jnp.moveaxis(o, 0, 2).reshape(b, h, t, dv)


def _reverse(a):
    return jnp.flip(a, axis=2)


def gdn_mixer(qkv, z, a, beta_logit, conv_w, a_log, dt_bias, norm_w):
    b, s, _ = qkv.shape
    f32 = jnp.float32
    qkv = jax.nn.silu(short_conv(qkv, conv_w)).astype(f32)
    qkv = qkv.reshape(b, s, 3, GDN_HEADS, GDN_HEAD_DIM).transpose(2, 0, 3, 1, 4)
    q, k, v = l2_normalize(qkv[0]), l2_normalize(qkv[1]), qkv[2]
    a = a.astype(f32).reshape(b, s, 2, GDN_HEADS)
    g = -jnp.exp(a_log.astype(f32)) * jax.nn.softplus(a + dt_bias.astype(f32))
    beta = jax.nn.sigmoid(beta_logit.astype(f32).reshape(b, s, 2, GDN_HEADS))
    g = g.transpose(2, 0, 3, 1)
    beta = beta.transpose(2, 0, 3, 1)
    o_fwd = gated_delta_chunked(q, k, v, g[0], beta[0])
    o_bwd = _reverse(gated_delta_chunked(_reverse(q), _reverse(k), _reverse(v),
                                         _reverse(g[1]), _reverse(beta[1])))
    o = (o_fwd + o_bwd).transpose(0, 2, 1, 3)
    zg = jax.nn.silu(z.astype(f32).reshape(b, s, GDN_HEADS, GDN_HEAD_DIM))
    o = rms_norm(o, norm_w) * zg
    return o.reshape(b, s, GDN_WIDTH).astype(z.dtype)


def dilated_band_attention(q, k, v, rel_bias, window, dilation):
    b, s, h, dh = q.shape
    radius = window // (2 * dilation)
    blk = BAND_BLOCK
    length = s // dilation
    nb = -(-length // blk)
    lp = nb * blk

    def to_blocks(a):
        a = a.reshape(b, length, dilation, h, dh).transpose(0, 2, 3, 1, 4)
        a = jnp.pad(a, ((0, 0), (0, 0), (0, 0), (0, lp - length), (0, 0)))
        return a.reshape(b, dilation, h, nb, blk, dh)

    def band(a):
        ap = jnp.pad(a, ((0, 0), (0, 0), (0, 0), (1, 1), (0, 0), (0, 0)))
        return jnp.concatenate([ap[:, :, :, :-2], ap[:, :, :, 1:-1], ap[:, :, :, 2:]], axis=-2)

    qb = to_blocks(q)
    kw = band(to_blocks(k))
    vw = band(to_blocks(v))
    qi = np.arange(blk)[:, None]
    ki = np.arange(3 * blk)[None, :]
    rel = ki - blk - qi
    key_t = (np.arange(nb)[:, None, None] - 1) * blk + ki[None]
    valid = (np.abs(rel)[None] <= radius) & (key_t >= 0) & (key_t < length)
    bias = jnp.transpose(rel_bias[t5_bucket(rel * dilation)], (2, 0, 1))[:, None]
    logits = jnp.einsum('bdhnqe,bdhnke->bdhnqk', qb, kw, preferred_element_type=jnp.float32)
    logits = jnp.where(valid, logits + bias.astype(jnp.float32), NEG_BIG)
    m = jnp.max(logits, axis=-1, keepdims=True)
    p = jnp.exp(logits - m)
    den = jnp.sum(p, axis=-1, keepdims=True)
    o = jnp.einsum('bdhnqk,bdhnke->bdhnqe', p, vw.astype(jnp.float32)) / den
    lse = (m + jnp.log(den))[..., 0]
    o = o.reshape(b, dilation, h, lp, dh)[:, :, :, :length].transpose(0, 3, 1, 2, 4).reshape(b, s, h, dh)
    lse = lse.reshape(b, dilation, h, lp)[..., :length].transpose(0, 3, 1, 2).reshape(b, s, h)
    return o, lse


def dilated_mixer(qkv, q_norm_w, k_norm_w, rel_bias):
    b, s, _ = qkv.shape
    qkv = qkv.reshape(b, s, 3, SWA_HEADS, SWA_HEAD_DIM)
    q = rms_norm(qkv[:, :, 0], q_norm_w) * (SWA_HEAD_DIM ** -0.5)
    k = rms_norm(qkv[:, :, 1], k_norm_w)
    v = qkv[:, :, 2]
    outs, lses = [], []
    for window, dilation in DILATION_PATTERNS:
        o_p, lse_p = dilated_band_attention(q, k, v, rel_bias, window, dilation)
        outs.append(o_p)
        lses.append(lse_p)
    wts = jax.nn.softmax(jnp.stack(lses, axis=0), axis=0)
    o = jnp.sum(wts[..., None] * jnp.stack(outs, axis=0), axis=0)
    return o.reshape(b, s, SWA_WIDTH).astype(qkv.dtype)


def setup_inputs(seed: int = 0) -> dict:
    key = jax.random.key(seed)
    ks = jax.random.split(key, 24)
    f32 = jnp.float32

    def dense(k, shape, fan_in):
        return jax.random.normal(k, shape, f32) * (fan_in ** -0.5)

    def gain(k, shape):
        return 1.0 + 0.02 * jax.random.normal(k, shape, f32)

    x = jax.random.normal(ks[0], (BATCH, SEQ, D_MODEL), f32)
    ffn1_norm = gain(ks[1], (DEPTH, D_MODEL))
    ffn1_w_gate = dense(ks[2], (DEPTH, D_MODEL, D_FF), D_MODEL)
    ffn1_w_up = dense(ks[3], (DEPTH, D_MODEL, D_FF), D_MODEL)
    ffn1_w_down = dense(ks[4], (DEPTH, D_FF, D_MODEL), D_FF)
    mix_norm = gain(ks[5], (DEPTH, D_MODEL))
    w_in = dense(ks[6], (DEPTH, D_MODEL, N_IN), D_MODEL)
    conv_w = dense(ks[7], (DEPTH, 3 * GDN_WIDTH, CONV_WIDTH), CONV_WIDTH)
    a_log = jnp.log(jax.random.uniform(ks[8], (DEPTH, 2, GDN_HEADS), f32, 1.0, 16.0))
    dt = jnp.exp(jax.random.uniform(ks[9], (DEPTH, 2, GDN_HEADS), f32, math.log(1e-3), math.log(1e-1)))
    dt_bias = dt + jnp.log(-jnp.expm1(-dt))
    gdn_norm_w = gain(ks[10], (DEPTH, GDN_HEAD_DIM))
    q_norm_w = gain(ks[11], (DEPTH, SWA_HEAD_DIM))
    k_norm_w = gain(ks[12], (DEPTH, SWA_HEAD_DIM))
    rel_bias = 0.2 * jax.random.normal(ks[13], (REL_BUCKETS, SWA_HEADS), f32)
    w_out = dense(ks[14], (DEPTH, D_MIX, D_MODEL), D_MIX)
    ffn2_norm = gain(ks[15], (DEPTH, D_MODEL))
    ffn2_w_gate = dense(ks[16], (DEPTH, D_MODEL, D_FF), D_MODEL)
    ffn2_w_up = dense(ks[17], (DEPTH, D_MODEL, D_FF), D_MODEL)
    ffn2_w_down = dense(ks[18], (DEPTH, D_FF, D_MODEL), D_FF)
    final_norm = gain(ks[19], (DEPTH, D_MODEL))
    return {"x": x, "ffn1_norm": ffn1_norm, "ffn1_w_gate": ffn1_w_gate, "ffn1_w_up": ffn1_w_up,
            "ffn1_w_down": ffn1_w_down, "mix_norm": mix_norm, "w_in": w_in, "conv_w": conv_w,
            "a_log": a_log, "dt_bias": dt_bias, "gdn_norm_w": gdn_norm_w, "q_norm_w": q_norm_w,
            "k_norm_w": k_norm_w, "rel_bias": rel_bias, "w_out": w_out, "ffn2_norm": ffn2_norm,
            "ffn2_w_gate": ffn2_w_gate, "ffn2_w_up": ffn2_w_up, "ffn2_w_down": ffn2_w_down,
            "final_norm": final_norm}


def reference(x, ffn1_norm, ffn1_w_gate, ffn1_w_up, ffn1_w_down, mix_norm, w_in, conv_w,
              a_log, dt_bias, gdn_norm_w, q_norm_w, k_norm_w, rel_bias, w_out, ffn2_norm,
              ffn2_w_gate, ffn2_w_up, ffn2_w_down, final_norm):
    split_at = np.cumsum(SPLITS)[:-1].tolist()
    for l in range(DEPTH):
        x = x + 0.5 * swiglu(rms_norm(x, ffn1_norm[l]), ffn1_w_gate[l], ffn1_w_up[l], ffn1_w_down[l])
        h = rms_norm(x, mix_norm[l])
        proj = h @ w_in[l]
        qkv_a, z_a, a_a, b_a, qkv_b = jnp.split(proj, split_at, axis=-1)
        o_a = gdn_mixer(qkv_a, z_a, a_a, b_a, conv_w[l], a_log[l], dt_bias[l], gdn_norm_w[l])
        o_b = dilated_mixer(qkv_b, q_norm_w[l], k_norm_w[l], rel_bias)
        x = x + jnp.concatenate([o_a, o_b], axis=-1) @ w_out[l]
        x = x + 0.5 * swiglu(rms_norm(x, ffn2_norm[l]), ffn2_w_gate[l], ffn2_w_up[l], ffn2_w_down[l])
        x = rms_norm(x, final_norm[l])
    return x
```

```python
import functools
import math

import numpy as np
import jax
import jax.numpy as jnp
from jax import lax
from jax.experimental import pallas as pl
from jax.experimental.pallas import tpu as pltpu

F32 = jnp.float32
BF16 = jnp.bfloat16
EPS = 1e-6
NEG_BIG = -1e30

LANES = 128
GDN_HEADS = 4
GDN_HEAD_DIM = 128
GDN_WIDTH = GDN_HEADS * GDN_HEAD_DIM
CONV_WIDTH = 5
CHUNK = 64
SWA_HEADS = 8
SWA_HEAD_DIM = 64
SWA_WIDTH = SWA_HEADS * SWA_HEAD_DIM
DILATIONS = (1, 4, 16)
BAND_RADIUS = 64
REL_BUCKETS = 32
REL_MAX_DISTANCE = 1024
VMEM_LIMIT = 56 * 1024 * 1024


def _cparams(sem):
    return pltpu.CompilerParams(dimension_semantics=sem, vmem_limit_bytes=VMEM_LIMIT)


def _resident(shape):
    zeros = (0,) * len(shape)
    return pl.BlockSpec(shape, lambda *_: zeros, pipeline_mode=pl.Buffered(1))


def _rms(x, w):
    return x * lax.rsqrt(jnp.mean(x * x, axis=-1, keepdims=True) + EPS) * w


def _silu(x):
    return x * (1.0 / (1.0 + jnp.exp(-x)))


def _dot(a, b):
    return jnp.dot(a, b, preferred_element_type=F32)


def _swiglu(h, wg_ref, wu_ref, wd_ref, fchunk):
    acc = None
    for c0 in range(0, wg_ref.shape[1], fchunk):
        g = _dot(h, wg_ref[:, c0:c0 + fchunk])
        u = _dot(h, wu_ref[:, c0:c0 + fchunk])
        a = (_silu(g) * u).astype(BF16)
        d = _dot(a, wd_ref[c0:c0 + fchunk, :])
        acc = d if acc is None else acc + d
    return acc


def _ffn1_proj_kernel(x_ref, n1_ref, wg_ref, wu_ref, wd_ref, nm_ref, win_ref,
                      x1_ref, qkva_ref, z_ref, ab_ref, qkvb_ref, *, fchunk):
    x = x_ref[...]
    h = _rms(x, n1_ref[...]).astype(BF16)
    x1 = x + 0.5 * _swiglu(h, wg_ref, wu_ref, wd_ref, fchunk)
    x1_ref[...] = x1
    h2 = _rms(x1, nm_ref[...]).astype(BF16)
    c = 0
    for ref in (qkva_ref, z_ref, ab_ref, qkvb_ref):
        n = ref.shape[1]
        ref[...] = _dot(h2, win_ref[:, c:c + n])
        c += n


def _ffn1_proj(x2d, n1, wg, wu, wd, nm, win, *, tm, fchunk):
    n, d = x2d.shape
    f = wg.shape[1]
    widths = (3 * GDN_WIDTH, GDN_WIDTH, LANES, 3 * SWA_WIDTH)
    assert win.shape[1] == sum(widths) and n % tm == 0
    row = lambda w: pl.BlockSpec((tm, w), lambda i: (i, 0))
    return pl.pallas_call(
        functools.partial(_ffn1_proj_kernel, fchunk=fchunk),
        grid=(n // tm,),
        in_specs=[row(d), _resident((1, d)), _resident((d, f)), _resident((d, f)), _resident((f, d)),
                  _resident((1, d)), _resident(win.shape)],
        out_specs=[row(d)] + [row(w) for w in widths],
        out_shape=[jax.ShapeDtypeStruct((n, d), F32)] + [jax.ShapeDtypeStruct((n, w), F32) for w in widths],
        compiler_params=_cparams(("arbitrary",)),
        name="ffn1_proj",
    )(x2d, n1, wg, wu, wd, nm, win)


def _out_ffn2_kernel(x1_ref, of_ref, ob_ref, z_ref, attn_ref, gnw_ref, wout_ref, n2_ref,
                     wg_ref, wu_ref, wd_ref, nf_ref, out_ref, *, fchunk):
    o = of_ref[0] + ob_ref[0]
    z = z_ref[...]
    gnw = gnw_ref[...]
    heads = []
    for h in range(GDN_HEADS):
        sl = slice(h * GDN_HEAD_DIM, (h + 1) * GDN_HEAD_DIM)
        heads.append(_rms(o[:, sl], gnw) * _silu(z[:, sl]))
    oa = jnp.concatenate(heads, axis=1).astype(BF16)
    mix = _dot(oa, wout_ref[:GDN_WIDTH, :]) + _dot(attn_ref[...].astype(BF16), wout_ref[GDN_WIDTH:, :])
    x2 = x1_ref[...] + mix
    h2 = _rms(x2, n2_ref[...]).astype(BF16)
    x3 = x2 + 0.5 * _swiglu(h2, wg_ref, wu_ref, wd_ref, fchunk)
    out_ref[...] = _rms(x3, nf_ref[...])


def _out_ffn2(x1, o_dirs, z, attn, gnw, wout, n2, wg, wu, wd, nf, *, tm, fchunk):
    n, d = x1.shape
    f = wg.shape[1]
    row = lambda w: pl.BlockSpec((tm, w), lambda i: (i, 0))
    direction = lambda k: pl.BlockSpec((1, tm, GDN_WIDTH), lambda i: (k, i, 0))
    return pl.pallas_call(
        functools.partial(_out_ffn2_kernel, fchunk=fchunk),
        grid=(n // tm,),
        in_specs=[row(d), direction(0), direction(1), row(GDN_WIDTH), row(SWA_WIDTH),
                  _resident((1, GDN_HEAD_DIM)), _resident(wout.shape), _resident((1, d)),
                  _resident((d, f)), _resident((d, f)), _resident((f, d)), _resident((1, d))],
        out_specs=row(d),
        out_shape=jax.ShapeDtypeStruct((n, d), F32),
        compiler_params=_cparams(("arbitrary",)),
        name="out_ffn2",
    )(x1, o_dirs, o_dirs, z, attn, gnw, wout, n2, wg, wu, wd, nf)


def _gdn_prep_kernel(cur_ref, prev_ref, next_ref, ab_ref, cw_ref, alog_ref, dtb_ref,
                     q_ref, k_ref, v_ref, gcol_ref, grow_ref, pad_ref, *, ts):
    j = pl.program_id(1)
    nj = pl.num_programs(1)
    pad_ref[0:8, :] = jnp.where(j > 0, prev_ref[0], 0.0)
    pad_ref[8:8 + ts, :] = cur_ref[0]
    pad_ref[8 + ts:16 + ts, :] = jnp.where(j < nj - 1, next_ref[0], 0.0)
    for cb in range(3 * GDN_HEADS):
        cs = slice(cb * LANES, (cb + 1) * LANES)
        acc = None
        for t in range(CONV_WIDTH):
            term = pad_ref[6 + t:6 + t + ts, cs] * cw_ref[t:t + 1, cs]
            acc = term if acc is None else acc + term
        y = _silu(acc)
        if cb < 2 * GDN_HEADS:
            y = y * lax.rsqrt(jnp.sum(y * y, axis=-1, keepdims=True) + EPS)
        if cb < GDN_HEADS:
            y = y * (GDN_HEAD_DIM ** -0.5)
        dst = (q_ref, k_ref, v_ref)[cb // GDN_HEADS]
        hs = slice((cb % GDN_HEADS) * LANES, (cb % GDN_HEADS + 1) * LANES)
        dst[0, :, hs] = y

    ab = ab_ref[0]
    xs = ab + dtb_ref[...]
    softplus = jnp.maximum(xs, 0.0) + jnp.log(1.0 + jnp.exp(-jnp.abs(xs)))
    g = -jnp.exp(alog_ref[...]) * softplus
    beta = 1.0 / (1.0 + jnp.exp(-ab))
    g1 = g.astype(BF16)
    r1 = g - g1.astype(F32)
    g2 = r1.astype(BF16)
    g3 = (r1 - g2.astype(F32)).astype(BF16)
    ri = lax.broadcasted_iota(jnp.int32, (LANES, LANES), 0)
    ci = lax.broadcasted_iota(jnp.int32, (LANES, LANES), 1)
    same = (ri // CHUNK) == (ci // CHUNK)
    lower = jnp.where(same & (ri >= ci), 1.0, 0.0).astype(BF16)
    upper = jnp.where(same & (ri <= ci), 1.0, 0.0).astype(BF16)
    lane = lax.broadcasted_iota(jnp.int32, (1, LANES), 1)
    for s in range(ts // LANES):
        rs = slice(s * LANES, (s + 1) * LANES)
        pre = _dot(lower, g1[rs]) + _dot(lower, g2[rs]) + _dot(lower, g3[rs])
        suf = _dot(upper, g1[rs]) + _dot(upper, g2[rs]) + _dot(upper, g3[rs])
        gs = g[rs]
        col = jnp.where(lane < 4, pre,
              jnp.where(lane < 8, suf,
              jnp.where(lane < 16, beta[rs],
              jnp.where(lane < 20, suf - gs,
              jnp.where(lane < 24, pre - gs, pre + suf - gs)))))
        gcol_ref[0, rs, :] = col
        colt = col.T
        grow_ref[0, 2 * s] = colt[0:8, 0:CHUNK]
        grow_ref[0, 2 * s + 1] = colt[0:8, CHUNK:2 * CHUNK]


def _gdn_prep(qkva, ab, cw, alog, dtb, *, ts):
    b, s, w = qkva.shape
    nblk = s // ts
    r8 = ts // 8
    last8 = s // 8 - 1
    f = lambda shape: jax.ShapeDtypeStruct(shape, F32)
    return pl.pallas_call(
        functools.partial(_gdn_prep_kernel, ts=ts),
        grid=(b, nblk),
        in_specs=[pl.BlockSpec((1, ts, w), lambda i, j: (i, j, 0)),
                  pl.BlockSpec((1, 8, w), lambda i, j: (i, jnp.maximum(j * r8 - 1, 0), 0)),
                  pl.BlockSpec((1, 8, w), lambda i, j: (i, jnp.minimum((j + 1) * r8, last8), 0)),
                  pl.BlockSpec((1, ts, LANES), lambda i, j: (i, j, 0)),
                  _resident(cw.shape), _resident((1, LANES)), _resident((1, LANES))],
        out_specs=[pl.BlockSpec((1, ts, GDN_WIDTH), lambda i, j: (i, j, 0))] * 3
                  + [pl.BlockSpec((1, ts, LANES), lambda i, j: (i, j, 0)),
                     pl.BlockSpec((1, ts // CHUNK, 8, CHUNK), lambda i, j: (i, j, 0, 0))],
        out_shape=[f((b, s, GDN_WIDTH))] * 3 + [f((b, s, LANES)), f((b, s // CHUNK, 8, CHUNK))],
        scratch_shapes=[pltpu.VMEM((ts + 16, w), F32)],
        compiler_params=_cparams(("arbitrary", "arbitrary")),
        name="gdn_prep",
    )(qkva, qkva, qkva, ab, cw, alog, dtb)


def _gdn_scan_kernel(q_ref, k_ref, v_ref, gcol_ref, grow_ref, o_ref,
                     s_ref, wqg_ref, u_ref, ik_ref, eg_ref, *, nc):
    dirn = pl.program_id(1)
    fwd = dirn == 0

    @pl.when(pl.program_id(2) == 0)
    def _():
        s_ref[...] = jnp.zeros(s_ref.shape, F32)

    ri = lax.broadcasted_iota(jnp.int32, (CHUNK, CHUNK), 0)
    ci = lax.broadcasted_iota(jnp.int32, (CHUNK, CHUNK), 1)
    dd = (ri - ci) * (1 - 2 * dirn)
    incl = dd >= 0
    strict = dd > 0
    nt = (((1,), (1,)), ((), ()))

    def chunk_matrices(c, carry):
        r0 = pl.multiple_of(c * CHUNK, CHUNK)
        gates = gcol_ref[0, pl.ds(r0, CHUNK), :]
        rows = grow_ref[0, c]
        for h in range(GDN_HEADS):
            def pick(base):
                return jnp.where(fwd, gates[:, base + h:base + h + 1], gates[:, base + 4 + h:base + 5 + h])
            gc, beta, gdec, gl = pick(0), pick(8), pick(16), pick(24)
            grow = jnp.where(fwd, rows[h:h + 1, :], rows[4 + h:5 + h, :])
            hs = slice(h * GDN_HEAD_DIM, (h + 1) * GDN_HEAD_DIM)
            qc = q_ref[0, pl.ds(r0, CHUNK), hs]
            kc = k_ref[0, pl.ds(r0, CHUNK), hs]
            vc = v_ref[0, pl.ds(r0, CHUNK), hs]
            dec = jnp.where(incl, jnp.exp(jnp.where(incl, gc - grow, 0.0)), 0.0)
            kb = kc * beta
            qk = lax.dot_general(jnp.concatenate([kb, qc], axis=0).astype(BF16), kc.astype(BF16), nt,
                                 preferred_element_type=F32)
            amat = jnp.where(strict, qk[:CHUNK] * dec, 0.0)
            intra = qk[CHUNK:] * dec
            eg = jnp.exp(gc)
            r = jnp.concatenate([vc * beta, kb * eg], axis=1)
            p = -amat
            nsq = int(math.log2(CHUNK))
            for t in range(nsq):
                pb = p.astype(BF16)
                r = r + _dot(pb, r.astype(BF16))
                if t < nsq - 1:
                    p = _dot(pb, pb)
            u_ref[c, h] = r[:, :GDN_HEAD_DIM]
            wqg_ref[c, h] = jnp.concatenate([r[:, GDN_HEAD_DIM:], qc * eg], axis=0).astype(BF16)
            kdec = kc * jnp.exp(gdec)
            ik_ref[c, h] = jnp.concatenate([intra, kdec.T], axis=0).astype(BF16)
            eg_ref[c, h] = jnp.broadcast_to(jnp.exp(gl[0:1, :]), (8, GDN_HEAD_DIM))
        return carry

    lax.fori_loop(0, nc, chunk_matrices, 0)

    def scan_step(i, carry):
        c = i + dirn * (nc - 1 - 2 * i)
        r0 = pl.multiple_of(c * CHUNK, CHUNK)
        for h in range(GDN_HEADS):
            hs = slice(h * GDN_HEAD_DIM, (h + 1) * GDN_HEAD_DIM)
            state = s_ref[h]
            m1 = _dot(wqg_ref[c, h], state.astype(BF16))
            v_new = u_ref[c, h] - m1[:CHUNK]
            m2 = _dot(ik_ref[c, h], v_new.astype(BF16))
            o_ref[0, 0, pl.ds(r0, CHUNK), hs] = m1[CHUNK:] + m2[:CHUNK]
            s_ref[h] = state * eg_ref[c, h][0:1, :] + m2[CHUNK:]
        return carry

    lax.fori_loop(0, nc, scan_step, 0)


def _gdn_scan(q, k, v, gcol, grow, *, ts):
    b, s, w = q.shape
    nblk = s // ts
    nc = ts // CHUNK
    blk = lambda i, d, j: (i, j + d * (nblk - 1 - 2 * j), 0)
    return pl.pallas_call(
        functools.partial(_gdn_scan_kernel, nc=nc),
        grid=(b, 2, nblk),
        in_specs=[pl.BlockSpec((1, ts, w), blk)] * 3
                 + [pl.BlockSpec((1, ts, LANES), blk),
                    pl.BlockSpec((1, nc, 8, CHUNK), lambda i, d, j: blk(i, d, j) + (0,))],
        out_specs=pl.BlockSpec((1, 1, ts, w), lambda i, d, j: (d,) + blk(i, d, j)),
        out_shape=jax.ShapeDtypeStruct((2, b, s, w), F32),
        scratch_shapes=[pltpu.VMEM((GDN_HEADS, GDN_HEAD_DIM, GDN_HEAD_DIM), F32),
                        pltpu.VMEM((nc, GDN_HEADS, 2 * CHUNK, GDN_HEAD_DIM), BF16),
                        pltpu.VMEM((nc, GDN_HEADS, CHUNK, GDN_HEAD_DIM), F32),
                        pltpu.VMEM((nc, GDN_HEADS, CHUNK + GDN_HEAD_DIM, CHUNK), BF16),
                        pltpu.VMEM((nc, GDN_HEADS, 8, GDN_HEAD_DIM), F32)],
        compiler_params=_cparams(("arbitrary", "arbitrary", "arbitrary")),
        name="gdn_scan",
    )(q, k, v, gcol, grow)


SWA_TQ = 128
SWA_TK = SWA_TQ + 2 * BAND_RADIUS
SWA_HALO = BAND_RADIUS * max(DILATIONS)
NORM_ROWS = 512


def _swa_kernel(q_ref, k_ref, v_ref, qw_ref, kw_ref, bias_ref, o_ref,
                qn_ref, kn_ref, vp_ref, m_ref, l_ref, acc_ref, *, seq, qb):
    qi = pl.program_id(2)
    lane = lax.broadcasted_iota(jnp.int32, (1, LANES), 1)
    low = lane < SWA_HEAD_DIM
    nt = (((1,), (1,)), ((), ()))

    @pl.when(qi == 0)
    def _():
        zeros = jnp.zeros((SWA_HALO, LANES), F32)
        for ref in (kn_ref, vp_ref):
            ref[0:SWA_HALO, :] = zeros
            ref[SWA_HALO + seq:2 * SWA_HALO + seq, :] = zeros

        def head_rms(x, w):
            x2 = x * x
            s0 = jnp.sum(jnp.where(low, x2, 0.0), axis=-1, keepdims=True)
            s1 = jnp.sum(jnp.where(low, 0.0, x2), axis=-1, keepdims=True)
            ms = jnp.where(low, s0, s1) * (1.0 / SWA_HEAD_DIM)
            return x * lax.rsqrt(ms + EPS) * w

        def norm_rows(i, carry):
            r0 = pl.multiple_of(i * NORM_ROWS, NORM_ROWS)
            qn_ref[pl.ds(r0, NORM_ROWS), :] = head_rms(q_ref[0, pl.ds(r0, NORM_ROWS), :], qw_ref[...]) * (SWA_HEAD_DIM ** -0.5)
            kn_ref[pl.ds(SWA_HALO + r0, NORM_ROWS), :] = head_rms(k_ref[0, pl.ds(r0, NORM_ROWS), :], kw_ref[...])
            vp_ref[pl.ds(SWA_HALO + r0, NORM_ROWS), :] = v_ref[0, pl.ds(r0, NORM_ROWS), :]
            return carry

        lax.fori_loop(0, seq // NORM_ROWS, norm_rows, 0)

    kcol = lax.broadcasted_iota(jnp.int32, (1, SWA_TK), 1)

    def rows(ref, start, size, stride):
        if stride == 1:
            return ref[pl.ds(start, size), :]
        return ref[pl.ds(start, size, stride=stride), :]

    def put(ref, start, size, stride, val):
        if stride == 1:
            ref[pl.ds(start, size), :] = val
        else:
            ref[pl.ds(start, size, stride=stride), :] = val

    for pi, dil in enumerate(DILATIONS):
        sub_len = seq // dil
        tiles_per_res = qb // dil // SWA_TQ
        shift = int(math.log2(tiles_per_res))
        first = pi == 0

        def tile(i, carry, pi=pi, dil=dil, sub_len=sub_len, tiles_per_res=tiles_per_res, shift=shift, first=first):
            res = lax.shift_right_logical(i, shift)
            t = jnp.bitwise_and(i, tiles_per_res - 1)
            tau0 = qi * (qb // dil) + t * SWA_TQ
            loc0 = res + dil * (t * SWA_TQ)
            krow = SWA_HALO + qi * qb + loc0 - dil * BAND_RADIUS
            qt = rows(qn_ref, qi * qb + loc0, SWA_TQ, dil)
            kt = rows(kn_ref, krow, SWA_TK, dil).astype(BF16)
            vt = rows(vp_ref, krow, SWA_TK, dil).astype(BF16)
            kidx = kcol + (tau0 - BAND_RADIUS)
            kvalid = (kidx >= 0) & (kidx < sub_len)
            if not first:
                m_old = rows(m_ref, loc0, SWA_TQ, dil)
                l_old = rows(l_ref, loc0, SWA_TQ, dil)
                acc_old = rows(acc_ref, loc0, SWA_TQ, dil)
            m_new, l_new, alpha, pv = [], [], [], []
            for hh in range(2):
                mine = low if hh == 0 else jnp.logical_not(low)
                qh = jnp.where(mine, qt, 0.0).astype(BF16)
                s = lax.dot_general(qh, kt, nt, preferred_element_type=F32)
                s = jnp.where(kvalid, s + bias_ref[0, pi, hh], NEG_BIG)
                mt = jnp.max(s, axis=-1, keepdims=True)
                if first:
                    mn = mt
                else:
                    c0 = hh * SWA_HEAD_DIM
                    mo = m_old[:, c0:c0 + 1]
                    mn = jnp.maximum(mo, mt)
                    al = jnp.exp(mo - mn)
                    alpha.append(al)
                p = jnp.exp(s - mn)
                lt = jnp.sum(p, axis=-1, keepdims=True)
                l_new.append(lt if first else al * l_old[:, c0:c0 + 1] + lt)
                m_new.append(mn)
                pv.append(_dot(p.astype(BF16), vt))
            both = lambda pair: jnp.where(low, pair[0], pair[1])
            put(m_ref, loc0, SWA_TQ, dil, both(m_new))
            put(l_ref, loc0, SWA_TQ, dil, both(l_new))
            put(acc_ref, loc0, SWA_TQ, dil, both(pv) if first else acc_old * both(alpha) + both(pv))
            return carry

        lax.fori_loop(0, qb // SWA_TQ, tile, 0)

    o_ref[0] = acc_ref[...] / l_ref[...]


def _t5_bucket(rel):
    nb = REL_BUCKETS // 2
    bucket = (rel > 0).astype(np.int32) * nb
    n = np.abs(rel)
    max_exact = nb // 2
    large = max_exact + (np.log(np.maximum(n, 1) / max_exact)
                         / math.log(REL_MAX_DISTANCE / max_exact) * (nb - max_exact)).astype(np.int32)
    large = np.minimum(large, nb - 1)
    return (bucket + np.where(n < max_exact, n, large)).astype(np.int32)


def _band_bias(rel_bias):
    rel = np.arange(SWA_TK)[None, :] - BAND_RADIUS - np.arange(SWA_TQ)[:, None]
    inside = np.abs(rel) <= BAND_RADIUS
    tabs = []
    for dil in DILATIONS:
        tab = jnp.transpose(rel_bias[_t5_bucket(rel * dil)], (2, 0, 1))
        tabs.append(jnp.where(inside[None], tab, NEG_BIG))
    bias = jnp.stack(tabs, axis=0)
    bias = bias.reshape(len(DILATIONS), SWA_HEADS // 2, 2, SWA_TQ, SWA_TK)
    return jnp.transpose(bias, (1, 0, 2, 3, 4)).astype(F32)


def _swa(qkvb, qw, kw, bias, *, qb):
    b, s, _ = qkvb.shape
    pairs = SWA_HEADS // 2
    assert s % qb == 0 and qb % (max(DILATIONS) * SWA_TQ) == 0
    col = lambda base: pl.BlockSpec((1, s, LANES), lambda i, p, j: (i, 0, base + p))
    padded = s + 2 * SWA_HALO
    return pl.pallas_call(
        functools.partial(_swa_kernel, seq=s, qb=qb),
        grid=(b, pairs, s // qb),
        in_specs=[col(0), col(pairs), col(2 * pairs), _resident((1, LANES)), _resident((1, LANES)),
                  pl.BlockSpec((1,) + bias.shape[1:], lambda i, p, j: (p, 0, 0, 0, 0))],
        out_specs=pl.BlockSpec((1, qb, LANES), lambda i, p, j: (i, j, p)),
        out_shape=jax.ShapeDtypeStruct((b, s, SWA_WIDTH), F32),
        scratch_shapes=[pltpu.VMEM((s, LANES), F32), pltpu.VMEM((padded, LANES), F32),
                        pltpu.VMEM((padded, LANES), F32)] + [pltpu.VMEM((qb, LANES), F32)] * 3,
        compiler_params=_cparams(("arbitrary", "arbitrary", "arbitrary")),
        name="swa",
    )(qkvb, qkvb, qkvb, qw, kw, bias)


FFN_TM = 512
FFN_FCHUNK = 256
PREP_TS = 512
SCAN_TS = 1024
SWA_QB = 2048


def _lane_row(vals, copies_at):
    row = jnp.zeros((LANES,), F32)
    for off in copies_at:
        row = row.at[off:off + vals.shape[0]].set(vals)
    return row[None, :]


def kernel(x, ffn1_norm, ffn1_w_gate, ffn1_w_up, ffn1_w_down, mix_norm, w_in, conv_w, a_log, dt_bias, gdn_norm_w, q_norm_w, k_norm_w, rel_bias, w_out, ffn2_norm, ffn2_w_gate, ffn2_w_up, ffn2_w_down, final_norm):
    b, s, d = x.shape
    n = b * s
    x2d = x.reshape(n, d)
    ng = 2 * GDN_HEADS
    c_z = 3 * GDN_WIDTH
    c_a = c_z + GDN_WIDTH
    c_b = c_a + ng
    c_qkvb = c_b + ng
    bias = _band_bias(rel_bias)
    for l in range(ffn1_norm.shape[0]):
        wl = w_in[l]
        w_a = wl[:, c_a:c_b]
        w_gates = jnp.concatenate([w_a, wl[:, c_b:c_qkvb], w_a, w_a, jnp.zeros((d, LANES - 4 * ng), F32)], axis=1)
        win = jnp.concatenate([wl[:, :c_a], w_gates, wl[:, c_qkvb:]], axis=1).astype(BF16)
        copies = (0, 2 * ng, 3 * ng)
        alog = _lane_row(a_log[l].reshape(ng), copies)
        dtb = _lane_row(dt_bias[l].reshape(ng), copies)
        x1, qkva, z, ab, qkvb = _ffn1_proj(
            x2d, ffn1_norm[l][None, :], ffn1_w_gate[l].astype(BF16), ffn1_w_up[l].astype(BF16),
            ffn1_w_down[l].astype(BF16), mix_norm[l][None, :], win, tm=FFN_TM, fchunk=FFN_FCHUNK)
        q, k, v, gcol, grow = _gdn_prep(
            qkva.reshape(b, s, -1), ab.reshape(b, s, LANES), jnp.transpose(conv_w[l]), alog, dtb, ts=PREP_TS)
        o_dirs = _gdn_scan(q, k, v, gcol, grow, ts=SCAN_TS)
        attn = _swa(qkvb.reshape(b, s, -1), jnp.tile(q_norm_w[l], 2)[None, :], jnp.tile(k_norm_w[l], 2)[None, :],
                    bias, qb=SWA_QB)
        x2d = _out_ffn2(
            x1, o_dirs.reshape(2, n, -1), z, attn.reshape(n, -1),
            gdn_norm_w[l][None, :], w_out[l].astype(BF16), ffn2_norm[l][None, :],
            ffn2_w_gate[l].astype(BF16), ffn2_w_up[l].astype(BF16), ffn2_w_down[l].astype(BF16),
            final_norm[l][None, :], tm=FFN_TM, fchunk=FFN_FCHUNK)
    return x2d.reshape(b, s, d)
```

```python
import functools
import math

import numpy as np
import jax
import jax.numpy as jnp
from jax import lax
from jax.experimental import pallas as pl
from jax.experimental.pallas import tpu as pltpu

F32 = jnp.float32
BF16 = jnp.bfloat16
EPS = 1e-6
NEG_BIG = -1e30

LANES = 128
GDN_HEADS = 4
GDN_HEAD_DIM = 128
GDN_WIDTH = GDN_HEADS * GDN_HEAD_DIM
CONV_WIDTH = 5
CHUNK = 64
SWA_HEADS = 8
SWA_HEAD_DIM = 64
SWA_WIDTH = SWA_HEADS * SWA_HEAD_DIM
DILATIONS = (1, 4, 16)
BAND_RADIUS = 64
REL_BUCKETS = 32
REL_MAX_DISTANCE = 1024
VMEM_LIMIT = 56 * 1024 * 1024


def _cparams(sem):
    return pltpu.CompilerParams(dimension_semantics=sem, vmem_limit_bytes=VMEM_LIMIT)


def _resident(shape):
    zeros = (0,) * len(shape)
    return pl.BlockSpec(shape, lambda *_: zeros, pipeline_mode=pl.Buffered(1))


def _rms(x, w):
    return x * lax.rsqrt(jnp.mean(x * x, axis=-1, keepdims=True) + EPS) * w


def _silu(x):
    return x * (1.0 / (1.0 + jnp.exp(-x)))


def _dot(a, b):
    return jnp.dot(a, b, preferred_element_type=F32)


def _swiglu(h, wg_ref, wu_ref, wd_ref, fchunk):
    acc = None
    for c0 in range(0, wg_ref.shape[1], fchunk):
        g = _dot(h, wg_ref[:, c0:c0 + fchunk])
        u = _dot(h, wu_ref[:, c0:c0 + fchunk])
        a = (_silu(g) * u).astype(BF16)
        d = _dot(a, wd_ref[c0:c0 + fchunk, :])
        acc = d if acc is None else acc + d
    return acc


def _ffn1_proj_kernel(x_ref, n1_ref, wg_ref, wu_ref, wd_ref, nm_ref, win_ref,
                      x1_ref, qkva_ref, z_ref, ab_ref, qkvb_ref, *, fchunk):
    x = x_ref[...]
    h = _rms(x, n1_ref[...]).astype(BF16)
    x1 = x + 0.5 * _swiglu(h, wg_ref, wu_ref, wd_ref, fchunk)
    x1_ref[...] = x1
    h2 = _rms(x1, nm_ref[...]).astype(BF16)
    c = 0
    for ref in (qkva_ref, z_ref, ab_ref, qkvb_ref):
        n = ref.shape[1]
        ref[...] = _dot(h2, win_ref[:, c:c + n])
        c += n


def _ffn1_proj(x2d, n1, wg, wu, wd, nm, win, *, tm, fchunk):
    n, d = x2d.shape
    f = wg.shape[1]
    widths = (3 * GDN_WIDTH, GDN_WIDTH, LANES, 3 * SWA_WIDTH)
    assert win.shape[1] == sum(widths) and n % tm == 0
    row = lambda w: pl.BlockSpec((tm, w), lambda i: (i, 0))
    return pl.pallas_call(
        functools.partial(_ffn1_proj_kernel, fchunk=fchunk),
        grid=(n // tm,),
        in_specs=[row(d), _resident((1, d)), _resident((d, f)), _resident((d, f)), _resident((f, d)),
                  _resident((1, d)), _resident(win.shape)],
        out_specs=[row(d)] + [row(w) for w in widths],
        out_shape=[jax.ShapeDtypeStruct((n, d), F32)] + [jax.ShapeDtypeStruct((n, w), F32) for w in widths],
        compiler_params=_cparams(("arbitrary",)),
        name="ffn1_proj",
    )(x2d, n1, wg, wu, wd, nm, win)


def _out_ffn2_kernel(x1_ref, of_ref, ob_ref, z_ref, attn_ref, gnw_ref, wout_ref, n2_ref,
                     wg_ref, wu_ref, wd_ref, nf_ref, out_ref, *, fchunk):
    o = of_ref[0] + ob_ref[0]
    z = z_ref[...]
    gnw = gnw_ref[...]
    heads = []
    for h in range(GDN_HEADS):
        sl = slice(h * GDN_HEAD_DIM, (h + 1) * GDN_HEAD_DIM)
        heads.append(_rms(o[:, sl], gnw) * _silu(z[:, sl]))
    oa = jnp.concatenate(heads, axis=1).astype(BF16)
    mix = _dot(oa, wout_ref[:GDN_WIDTH, :]) + _dot(attn_ref[...].astype(BF16), wout_ref[GDN_WIDTH:, :])
    x2 = x1_ref[...] + mix
    h2 = _rms(x2, n2_ref[...]).astype(BF16)
    x3 = x2 + 0.5 * _swiglu(h2, wg_ref, wu_ref, wd_ref, fchunk)
    out_ref[...] = _rms(x3, nf_ref[...])


def _out_ffn2(x1, o_dirs, z, attn, gnw, wout, n2, wg, wu, wd, nf, *, tm, fchunk):
    n, d = x1.shape
    f = wg.shape[1]
    row = lambda w: pl.BlockSpec((tm, w), lambda i: (i, 0))
    direction = lambda k: pl.BlockSpec((1, tm, GDN_WIDTH), lambda i: (k, i, 0))
    return pl.pallas_call(
        functools.partial(_out_ffn2_kernel, fchunk=fchunk),
        grid=(n // tm,),
        in_specs=[row(d), direction(0), direction(1), row(GDN_WIDTH), row(SWA_WIDTH),
                  _resident((1, GDN_HEAD_DIM)), _resident(wout.shape), _resident((1, d)),
                  _resident((d, f)), _resident((d, f)), _resident((f, d)), _resident((1, d))],
        out_specs=row(d),
        out_shape=jax.ShapeDtypeStruct((n, d), F32),
        compiler_params=_cparams(("arbitrary",)),
        name="out_ffn2",
    )(x1, o_dirs, o_dirs, z, attn, gnw, wout, n2, wg, wu, wd, nf)


def _gdn_prep_kernel(cur_ref, prev_ref, next_ref, ab_ref, cw_ref, alog_ref, dtb_ref,
                     q_ref, k_ref, v_ref, gcol_ref, grow_ref, pad_ref, *, ts):
    j = pl.program_id(1)
    nj = pl.num_programs(1)
    pad_ref[0:8, :] = jnp.where(j > 0, prev_ref[0], 0.0)
    pad_ref[8:8 + ts, :] = cur_ref[0]
    pad_ref[8 + ts:16 + ts, :] = jnp.where(j < nj - 1, next_ref[0], 0.0)
    for cb in range(3 * GDN_HEADS):
        cs = slice(cb * LANES, (cb + 1) * LANES)
        acc = None
        for t in range(CONV_WIDTH):
            term = pad_ref[6 + t:6 + t + ts, cs] * cw_ref[t:t + 1, cs]
            acc = term if acc is None else acc + term
        y = _silu(acc)
        if cb < 2 * GDN_HEADS:
            y = y * lax.rsqrt(jnp.sum(y * y, axis=-1, keepdims=True) + EPS)
        if cb < GDN_HEADS:
            y = y * (GDN_HEAD_DIM ** -0.5)
        dst = (q_ref, k_ref, v_ref)[cb // GDN_HEADS]
        hs = slice((cb % GDN_HEADS) * LANES, (cb % GDN_HEADS + 1) * LANES)
        dst[0, :, hs] = y

    ab = ab_ref[0]
    xs = ab + dtb_ref[...]
    softplus = jnp.maximum(xs, 0.0) + jnp.log(1.0 + jnp.exp(-jnp.abs(xs)))
    g = -jnp.exp(alog_ref[...]) * softplus
    beta = 1.0 / (1.0 + jnp.exp(-ab))
    g1 = g.astype(BF16)
    r1 = g - g1.astype(F32)
    g2 = r1.astype(BF16)
    g3 = (r1 - g2.astype(F32)).astype(BF16)
    ri = lax.broadcasted_iota(jnp.int32, (LANES, LANES), 0)
    ci = lax.broadcasted_iota(jnp.int32, (LANES, LANES), 1)
    same = (ri // CHUNK) == (ci // CHUNK)
    lower = jnp.where(same & (ri >= ci), 1.0, 0.0).astype(BF16)
    upper = jnp.where(same & (ri <= ci), 1.0, 0.0).astype(BF16)
    lane = lax.broadcasted_iota(jnp.int32, (1, LANES), 1)
    for s in range(ts // LANES):
        rs = slice(s * LANES, (s + 1) * LANES)
        pre = _dot(lower, g1[rs]) + _dot(lower, g2[rs]) + _dot(lower, g3[rs])
        suf = _dot(upper, g1[rs]) + _dot(upper, g2[rs]) + _dot(upper, g3[rs])
        gs = g[rs]
        col = jnp.where(lane < 4, pre,
              jnp.where(lane < 8, suf,
              jnp.where(lane < 16, beta[rs],
              jnp.where(lane < 20, suf - gs,
              jnp.where(lane < 24, pre - gs, pre + suf - gs)))))
        gcol_ref[0, rs, :] = col
        colt = col.T
        grow_ref[0, 2 * s] = colt[0:8, 0:CHUNK]
        grow_ref[0, 2 * s + 1] = colt[0:8, CHUNK:2 * CHUNK]


def _gdn_prep(qkva, ab, cw, alog, dtb, *, ts):
    b, s, w = qkva.shape
    nblk = s // ts
    r8 = ts // 8
    last8 = s // 8 - 1
    f = lambda shape: jax.ShapeDtypeStruct(shape, F32)
    return pl.pallas_call(
        functools.partial(_gdn_prep_kernel, ts=ts),
        grid=(b, nblk),
        in_specs=[pl.BlockSpec((1, ts, w), lambda i, j: (i, j, 0)),
                  pl.BlockSpec((1, 8, w), lambda i, j: (i, jnp.maximum(j * r8 - 1, 0), 0)),
                  pl.BlockSpec((1, 8, w), lambda i, j: (i, jnp.minimum((j + 1) * r8, last8), 0)),
                  pl.BlockSpec((1, ts, LANES), lambda i, j: (i, j, 0)),
                  _resident(cw.shape), _resident((1, LANES)), _resident((1, LANES))],
        out_specs=[pl.BlockSpec((1, ts, GDN_WIDTH), lambda i, j: (i, j, 0))] * 3
                  + [pl.BlockSpec((1, ts, LANES), lambda i, j: (i, j, 0)),
                     pl.BlockSpec((1, ts // CHUNK, 8, CHUNK), lambda i, j: (i, j, 0, 0))],
        out_shape=[f((b, s, GDN_WIDTH))] * 3 + [f((b, s, LANES)), f((b, s // CHUNK, 8, CHUNK))],
        scratch_shapes=[pltpu.VMEM((ts + 16, w), F32)],
        compiler_params=_cparams(("arbitrary", "arbitrary")),
        name="gdn_prep",
    )(qkva, qkva, qkva, ab, cw, alog, dtb)


CHUNKS_PER_STEP = 2

def _gdn_scan_kernel(q_ref, k_ref, v_ref, gcol_ref, grow_ref, o_ref,
                     s_ref, wqg_ref, u_ref, ik_ref, eg_ref, *, nc):
    dirn = pl.program_id(1)
    fwd = dirn == 0

    @pl.when(pl.program_id(2) == 0)
    def _():
        s_ref[...] = jnp.zeros(s_ref.shape, F32)

    ri = lax.broadcasted_iota(jnp.int32, (CHUNK, CHUNK), 0)
    ci = lax.broadcasted_iota(jnp.int32, (CHUNK, CHUNK), 1)
    dd = (ri - ci) * (1 - 2 * dirn)
    incl = dd >= 0
    strict = dd > 0
    nt = (((1,), (1,)), ((), ()))

    nsq = int(math.log2(CHUNK))

    def chunk_matrices(ci, carry):
        probs = [(ci * CHUNKS_PER_STEP + cc, h) for cc in range(CHUNKS_PER_STEP) for h in range(GDN_HEADS)]
        qs, ks, egs, rs, ps, intras = [], [], [], [], [], []
        for c, h in probs:
            r0 = pl.multiple_of(c * CHUNK, CHUNK)
            gates = gcol_ref[0, pl.ds(r0, CHUNK), :]
            rows = grow_ref[0, c]

            def pick(base):
                return jnp.where(fwd, gates[:, base + h:base + h + 1], gates[:, base + 4 + h:base + 5 + h])
            gc, beta, gdec, gl = pick(0), pick(8), pick(16), pick(24)
            grow = jnp.where(fwd, rows[h:h + 1, :], rows[4 + h:5 + h, :])
            hs = slice(h * GDN_HEAD_DIM, (h + 1) * GDN_HEAD_DIM)
            qc = q_ref[0, pl.ds(r0, CHUNK), hs]
            kc = k_ref[0, pl.ds(r0, CHUNK), hs]
            vc = v_ref[0, pl.ds(r0, CHUNK), hs]
            dec = jnp.where(incl, jnp.exp(jnp.where(incl, gc - grow, 0.0)), 0.0)
            kb = kc * beta
            qk = lax.dot_general(jnp.concatenate([kb, qc], axis=0).astype(BF16), kc.astype(BF16), nt,
                                 preferred_element_type=F32)
            ps.append(-jnp.where(strict, qk[:CHUNK] * dec, 0.0))
            intras.append(qk[CHUNK:] * dec)
            eg = jnp.exp(gc)
            rs.append(jnp.concatenate([vc * beta, kb * eg], axis=1))
            qs.append(qc * eg)
            ks.append(kc * jnp.exp(gdec))
            egs.append(jnp.exp(gl[0:1, :]))
        for t in range(nsq):
            for n in range(len(probs)):
                pb = ps[n].astype(BF16)
                rs[n] = rs[n] + _dot(pb, rs[n].astype(BF16))
                if t < nsq - 1:
                    ps[n] = _dot(pb, pb)
        for n, (c, h) in enumerate(probs):
            u_ref[c, h] = rs[n][:, :GDN_HEAD_DIM]
            wqg_ref[c, h] = jnp.concatenate([rs[n][:, GDN_HEAD_DIM:], qs[n]], axis=0).astype(BF16)
            ik_ref[c, h] = jnp.concatenate([intras[n], ks[n].T], axis=0).astype(BF16)
            eg_ref[c, h] = jnp.broadcast_to(egs[n], (8, GDN_HEAD_DIM))
        return carry

    lax.fori_loop(0, nc // CHUNKS_PER_STEP, chunk_matrices, 0)

    def scan_step(i, carry):
        c = i + dirn * (nc - 1 - 2 * i)
        r0 = pl.multiple_of(c * CHUNK, CHUNK)
        states = [s_ref[h] for h in range(GDN_HEADS)]
        m1 = [_dot(wqg_ref[c, h], states[h].astype(BF16)) for h in range(GDN_HEADS)]
        v_new = [(u_ref[c, h] - m1[h][:CHUNK]).astype(BF16) for h in range(GDN_HEADS)]
        m2 = [_dot(ik_ref[c, h], v_new[h]) for h in range(GDN_HEADS)]
        for h in range(GDN_HEADS):
            hs = slice(h * GDN_HEAD_DIM, (h + 1) * GDN_HEAD_DIM)
            o_ref[0, 0, pl.ds(r0, CHUNK), hs] = m1[h][CHUNK:] + m2[h][:CHUNK]
            s_ref[h] = states[h] * eg_ref[c, h][0:1, :] + m2[h][CHUNK:]
        return carry

    lax.fori_loop(0, nc, scan_step, 0)


def _gdn_scan(q, k, v, gcol, grow, *, ts):
    b, s, w = q.shape
    nblk = s // ts
    nc = ts // CHUNK
    blk = lambda i, d, j: (i, j + d * (nblk - 1 - 2 * j), 0)
    return pl.pallas_call(
        functools.partial(_gdn_scan_kernel, nc=nc),
        grid=(b, 2, nblk),
        in_specs=[pl.BlockSpec((1, ts, w), blk)] * 3
                 + [pl.BlockSpec((1, ts, LANES), blk),
                    pl.BlockSpec((1, nc, 8, CHUNK), lambda i, d, j: blk(i, d, j) + (0,))],
        out_specs=pl.BlockSpec((1, 1, ts, w), lambda i, d, j: (d,) + blk(i, d, j)),
        out_shape=jax.ShapeDtypeStruct((2, b, s, w), F32),
        scratch_shapes=[pltpu.VMEM((GDN_HEADS, GDN_HEAD_DIM, GDN_HEAD_DIM), F32),
                        pltpu.VMEM((nc, GDN_HEADS, 2 * CHUNK, GDN_HEAD_DIM), BF16),
                        pltpu.VMEM((nc, GDN_HEADS, CHUNK, GDN_HEAD_DIM), F32),
                        pltpu.VMEM((nc, GDN_HEADS, CHUNK + GDN_HEAD_DIM, CHUNK), BF16),
                        pltpu.VMEM((nc, GDN_HEADS, 8, GDN_HEAD_DIM), F32)],
        compiler_params=_cparams(("arbitrary", "arbitrary", "arbitrary")),
        name="gdn_scan",
    )(q, k, v, gcol, grow)


SWA_TQ = 128
SWA_TK = SWA_TQ + 2 * BAND_RADIUS
SWA_HALO = BAND_RADIUS * max(DILATIONS)
NORM_ROWS = 512
SWA_TILES_PER_STEP = 2


def _swa_kernel(q_ref, k_ref, v_ref, qw_ref, kw_ref, bias_ref, o_ref,
                qn_ref, kn_ref, vp_ref, m_ref, l_ref, acc_ref, bias_scr, *, seq, qb):
    qi = pl.program_id(2)
    lane = lax.broadcasted_iota(jnp.int32, (1, LANES), 1)
    low = lane < SWA_HEAD_DIM
    nt = (((1,), (1,)), ((), ()))

    @pl.when(qi == 0)
    def _():
        for pi in range(len(DILATIONS)):
            for hh in range(2):
                full = jnp.broadcast_to(bias_ref[0, pi, hh], (SWA_TQ, SWA_TK))
                bias_scr[pi, hh] = pltpu.roll(full, 0, 1, stride=1, stride_axis=0)

        zeros = jnp.zeros((SWA_HALO, LANES), F32)
        for ref in (kn_ref, vp_ref):
            ref[0:SWA_HALO, :] = zeros
            ref[SWA_HALO + seq:2 * SWA_HALO + seq, :] = zeros

        def head_rms(x, w):
            x2 = x * x
            s0 = jnp.sum(jnp.where(low, x2, 0.0), axis=-1, keepdims=True)
            s1 = jnp.sum(jnp.where(low, 0.0, x2), axis=-1, keepdims=True)
            ms = jnp.where(low, s0, s1) * (1.0 / SWA_HEAD_DIM)
            return x * lax.rsqrt(ms + EPS) * w

        def norm_rows(i, carry):
            r0 = pl.multiple_of(i * NORM_ROWS, NORM_ROWS)
            qn_ref[pl.ds(r0, NORM_ROWS), :] = head_rms(q_ref[0, pl.ds(r0, NORM_ROWS), :], qw_ref[...]) * (SWA_HEAD_DIM ** -0.5)
            kn_ref[pl.ds(SWA_HALO + r0, NORM_ROWS), :] = head_rms(k_ref[0, pl.ds(r0, NORM_ROWS), :], kw_ref[...])
            vp_ref[pl.ds(SWA_HALO + r0, NORM_ROWS), :] = v_ref[0, pl.ds(r0, NORM_ROWS), :]
            return carry

        lax.fori_loop(0, seq // NORM_ROWS, norm_rows, 0)

    kcol = lax.broadcasted_iota(jnp.int32, (1, SWA_TK), 1)

    def rows(ref, start, size, stride):
        if stride == 1:
            return ref[pl.ds(start, size), :]
        return ref[pl.ds(start, size, stride=stride), :]

    def put(ref, start, size, stride, val):
        if stride == 1:
            ref[pl.ds(start, size), :] = val
        else:
            ref[pl.ds(start, size, stride=stride), :] = val

    for pi, dil in enumerate(DILATIONS):
        sub_len = seq // dil
        tiles_per_res = qb // dil // SWA_TQ
        shift = int(math.log2(tiles_per_res))
        first = pi == 0

        def tiles(it, carry, pi=pi, dil=dil, sub_len=sub_len, tiles_per_res=tiles_per_res, shift=shift, first=first):
            loc, vts, old, ss = [], [], [], []
            for u in range(SWA_TILES_PER_STEP):
                i = it * SWA_TILES_PER_STEP + u
                res = lax.shift_right_logical(i, shift)
                t = jnp.bitwise_and(i, tiles_per_res - 1)
                tau0 = qi * (qb // dil) + t * SWA_TQ
                loc0 = res + dil * (t * SWA_TQ)
                krow = SWA_HALO + qi * qb + loc0 - dil * BAND_RADIUS
                qt = rows(qn_ref, qi * qb + loc0, SWA_TQ, dil)
                kt = rows(kn_ref, krow, SWA_TK, dil).astype(BF16)
                vts.append(rows(vp_ref, krow, SWA_TK, dil).astype(BF16))
                kidx = kcol + (tau0 - BAND_RADIUS)
                kvalid = (kidx >= 0) & (kidx < sub_len)
                loc.append(loc0)
                if not first:
                    old.append(tuple(rows(ref, loc0, SWA_TQ, dil) for ref in (m_ref, l_ref, acc_ref)))
                for hh in range(2):
                    mine = low if hh == 0 else jnp.logical_not(low)
                    qh = jnp.where(mine, qt, 0.0).astype(BF16)
                    s = lax.dot_general(qh, kt, nt, preferred_element_type=F32)
                    ss.append(jnp.where(kvalid, s + bias_scr[pi, hh], NEG_BIG))
            m_new, l_new, alpha, ps = [], [], [], []
            for n, s in enumerate(ss):
                u, hh = divmod(n, 2)
                mt = jnp.max(s, axis=-1, keepdims=True)
                if first:
                    mn = mt
                else:
                    c0 = hh * SWA_HEAD_DIM
                    mo = old[u][0][:, c0:c0 + 1]
                    mn = jnp.maximum(mo, mt)
                    al = jnp.exp(mo - mn)
                    alpha.append(al)
                p = jnp.exp(s - mn)
                lt = jnp.sum(p, axis=-1, keepdims=True)
                l_new.append(lt if first else al * old[u][1][:, c0:c0 + 1] + lt)
                m_new.append(mn)
                ps.append(p.astype(BF16))
            pv = [_dot(p, vts[n // 2]) for n, p in enumerate(ps)]
            for u in range(SWA_TILES_PER_STEP):
                both = lambda vals: jnp.where(low, vals[2 * u], vals[2 * u + 1])
                put(m_ref, loc[u], SWA_TQ, dil, both(m_new))
                put(l_ref, loc[u], SWA_TQ, dil, both(l_new))
                put(acc_ref, loc[u], SWA_TQ, dil, both(pv) if first else old[u][2] * both(alpha) + both(pv))
            return carry

        lax.fori_loop(0, qb // SWA_TQ // SWA_TILES_PER_STEP, tiles, 0)

    o_ref[0] = acc_ref[...] / l_ref[...]


def _t5_bucket(rel):
    nb = REL_BUCKETS // 2
    bucket = (rel > 0).astype(np.int32) * nb
    n = np.abs(rel)
    max_exact = nb // 2
    large = max_exact + (np.log(np.maximum(n, 1) / max_exact)
                         / math.log(REL_MAX_DISTANCE / max_exact) * (nb - max_exact)).astype(np.int32)
    large = np.minimum(large, nb - 1)
    return (bucket + np.where(n < max_exact, n, large)).astype(np.int32)


def _band_bias(rel_bias):
    rel = np.arange(2 * BAND_RADIUS + 1) - BAND_RADIUS
    rows = []
    for dil in DILATIONS:
        inside = jnp.transpose(rel_bias[_t5_bucket(rel * dil)])
        rows.append(jnp.pad(inside, ((0, 0), (0, SWA_TK - inside.shape[1])), constant_values=NEG_BIG))
    bias = jnp.stack(rows, axis=0)
    bias = bias.reshape(len(DILATIONS), SWA_HEADS // 2, 2, 1, SWA_TK)
    return jnp.transpose(bias, (1, 0, 2, 3, 4)).astype(F32)


def _swa(qkvb, qw, kw, bias, *, qb):
    b, s, _ = qkvb.shape
    pairs = SWA_HEADS // 2
    assert s % qb == 0 and qb % (max(DILATIONS) * SWA_TQ) == 0
    col = lambda base: pl.BlockSpec((1, s, LANES), lambda i, p, j: (i, 0, base + p))
    padded = s + 2 * SWA_HALO
    return pl.pallas_call(
        functools.partial(_swa_kernel, seq=s, qb=qb),
        grid=(b, pairs, s // qb),
        in_specs=[col(0), col(pairs), col(2 * pairs), _resident((1, LANES)), _resident((1, LANES)),
                  pl.BlockSpec((1,) + bias.shape[1:], lambda i, p, j: (p, 0, 0, 0, 0))],
        out_specs=pl.BlockSpec((1, qb, LANES), lambda i, p, j: (i, j, p)),
        out_shape=jax.ShapeDtypeStruct((b, s, SWA_WIDTH), F32),
        scratch_shapes=[pltpu.VMEM((s, LANES), F32), pltpu.VMEM((padded, LANES), F32),
                        pltpu.VMEM((padded, LANES), F32)] + [pltpu.VMEM((qb, LANES), F32)] * 3
                       + [pltpu.VMEM((len(DILATIONS), 2, SWA_TQ, SWA_TK), F32)],
        compiler_params=_cparams(("arbitrary", "arbitrary", "arbitrary")),
        name="swa",
    )(qkvb, qkvb, qkvb, qw, kw, bias)


FFN_TM = 512
FFN_FCHUNK = 256
PREP_TS = 512
SCAN_TS = 1024
SWA_QB = 2048


def _lane_row(vals, copies_at):
    row = jnp.zeros((LANES,), F32)
    for off in copies_at:
        row = row.at[off:off + vals.shape[0]].set(vals)
    return row[None, :]


def kernel(x, ffn1_norm, ffn1_w_gate, ffn1_w_up, ffn1_w_down, mix_norm, w_in, conv_w, a_log, dt_bias, gdn_norm_w, q_norm_w, k_norm_w, rel_bias, w_out, ffn2_norm, ffn2_w_gate, ffn2_w_up, ffn2_w_down, final_norm):
    b, s, d = x.shape
    n = b * s
    x2d = x.reshape(n, d)
    ng = 2 * GDN_HEADS
    c_z = 3 * GDN_WIDTH
    c_a = c_z + GDN_WIDTH
    c_b = c_a + ng
    c_qkvb = c_b + ng
    bias = _band_bias(rel_bias)
    for l in range(ffn1_norm.shape[0]):
        wl = w_in[l]
        w_a = wl[:, c_a:c_b]
        w_gates = jnp.concatenate([w_a, wl[:, c_b:c_qkvb], w_a, w_a, jnp.zeros((d, LANES - 4 * ng), F32)], axis=1)
        win = jnp.concatenate([wl[:, :c_a], w_gates, wl[:, c_qkvb:]], axis=1).astype(BF16)
        copies = (0, 2 * ng, 3 * ng)
        alog = _lane_row(a_log[l].reshape(ng), copies)
        dtb = _lane_row(dt_bias[l].reshape(ng), copies)
        x1, qkva, z, ab, qkvb = _ffn1_proj(
            x2d, ffn1_norm[l][None, :], ffn1_w_gate[l].astype(BF16), ffn1_w_up[l].astype(BF16),
            ffn1_w_down[l].astype(BF16), mix_norm[l][None, :], win, tm=FFN_TM, fchunk=FFN_FCHUNK)
        q, k, v, gcol, grow = _gdn_prep(
            qkva.reshape(b, s, -1), ab.reshape(b, s, LANES), jnp.transpose(conv_w[l]), alog, dtb, ts=PREP_TS)
        o_dirs = _gdn_scan(q, k, v, gcol, grow, ts=SCAN_TS)
        attn = _swa(qkvb.reshape(b, s, -1), jnp.tile(q_norm_w[l], 2)[None, :], jnp.tile(k_norm_w[l], 2)[None, :],
                    bias, qb=SWA_QB)
        x2d = _out_ffn2(
            x1, o_dirs.reshape(2, n, -1), z, attn.reshape(n, -1),
            gdn_norm_w[l][None, :], w_out[l].astype(BF16), ffn2_norm[l][None, :],
            ffn2_w_gate[l].astype(BF16), ffn2_w_up[l].astype(BF16), ffn2_w_down[l].astype(BF16),
            final_norm[l][None, :], tm=FFN_TM, fchunk=FFN_FCHUNK)
    return x2d.reshape(b, s, d)
```

```python
import functools
import math

import numpy as np
import jax
import jax.numpy as jnp
from jax import lax
from jax.experimental import pallas as pl
from jax.experimental.pallas import tpu as pltpu

F32 = jnp.float32
BF16 = jnp.bfloat16
EPS = 1e-6
NEG_BIG = -1e30
LOG2E = math.log2(math.e)

LANES = 128
GDN_HEADS = 4
GDN_HEAD_DIM = 128
GDN_WIDTH = GDN_HEADS * GDN_HEAD_DIM
CONV_WIDTH = 5
CHUNK = 64
SWA_HEADS = 8
SWA_HEAD_DIM = 64
SWA_WIDTH = SWA_HEADS * SWA_HEAD_DIM
DILATIONS = (1, 4, 16)
BAND_RADIUS = 64
REL_BUCKETS = 32
REL_MAX_DISTANCE = 1024
VMEM_LIMIT = 56 * 1024 * 1024


def _cparams(sem):
    return pltpu.CompilerParams(dimension_semantics=sem, vmem_limit_bytes=VMEM_LIMIT)


def _resident(shape):
    zeros = (0,) * len(shape)
    return pl.BlockSpec(shape, lambda *_: zeros, pipeline_mode=pl.Buffered(1))


def _rms(x, w):
    return x * lax.rsqrt(jnp.mean(x * x, axis=-1, keepdims=True) + EPS) * w


def _silu(x):
    return x * (1.0 / (1.0 + jnp.exp(-x)))


def _dot(a, b):
    return jnp.dot(a, b, preferred_element_type=F32)


def _swiglu(h, wg_ref, wu_ref, wd_ref, fchunk):
    acc = None
    for c0 in range(0, wg_ref.shape[1], fchunk):
        g = _dot(h, wg_ref[:, c0:c0 + fchunk])
        u = _dot(h, wu_ref[:, c0:c0 + fchunk])
        a = (_silu(g) * u).astype(BF16)
        d = _dot(a, wd_ref[c0:c0 + fchunk, :])
        acc = d if acc is None else acc + d
    return acc


def _ffn1_proj_kernel(x_ref, n1_ref, wg_ref, wu_ref, wd_ref, nm_ref, win_ref,
                      x1_ref, qkva_ref, z_ref, ab_ref, qkvb_ref, *, fchunk):
    x = x_ref[...]
    h = _rms(x, n1_ref[...]).astype(BF16)
    x1 = x + 0.5 * _swiglu(h, wg_ref, wu_ref, wd_ref, fchunk)
    x1_ref[...] = x1
    h2 = _rms(x1, nm_ref[...]).astype(BF16)
    c = 0
    for ref in (qkva_ref, z_ref, ab_ref, qkvb_ref):
        n = ref.shape[1]
        ref[...] = _dot(h2, win_ref[:, c:c + n])
        c += n


def _ffn1_proj(x2d, n1, wg, wu, wd, nm, win, *, tm, fchunk):
    n, d = x2d.shape
    f = wg.shape[1]
    widths = (3 * GDN_WIDTH, GDN_WIDTH, LANES, 3 * SWA_WIDTH)
    assert win.shape[1] == sum(widths) and n % tm == 0
    row = lambda w: pl.BlockSpec((tm, w), lambda i: (i, 0))
    return pl.pallas_call(
        functools.partial(_ffn1_proj_kernel, fchunk=fchunk),
        grid=(n // tm,),
        in_specs=[row(d), _resident((1, d)), _resident((d, f)), _resident((d, f)), _resident((f, d)),
                  _resident((1, d)), _resident(win.shape)],
        out_specs=[row(d)] + [row(w) for w in widths],
        out_shape=[jax.ShapeDtypeStruct((n, d), F32)] + [jax.ShapeDtypeStruct((n, w), F32) for w in widths],
        compiler_params=_cparams(("arbitrary",)),
        name="ffn1_proj",
    )(x2d, n1, wg, wu, wd, nm, win)


def _out_ffn2_kernel(x1_ref, of_ref, ob_ref, z_ref, attn_ref, gnw_ref, wout_ref, n2_ref,
                     wg_ref, wu_ref, wd_ref, nf_ref, out_ref, *, fchunk):
    o = of_ref[0] + ob_ref[0]
    z = z_ref[...]
    gnw = gnw_ref[...]
    heads = []
    for h in range(GDN_HEADS):
        sl = slice(h * GDN_HEAD_DIM, (h + 1) * GDN_HEAD_DIM)
        heads.append(_rms(o[:, sl], gnw) * _silu(z[:, sl]))
    oa = jnp.concatenate(heads, axis=1).astype(BF16)
    mix = _dot(oa, wout_ref[:GDN_WIDTH, :]) + _dot(attn_ref[...].astype(BF16), wout_ref[GDN_WIDTH:, :])
    x2 = x1_ref[...] + mix
    h2 = _rms(x2, n2_ref[...]).astype(BF16)
    x3 = x2 + 0.5 * _swiglu(h2, wg_ref, wu_ref, wd_ref, fchunk)
    out_ref[...] = _rms(x3, nf_ref[...])


def _out_ffn2(x1, o_dirs, z, attn, gnw, wout, n2, wg, wu, wd, nf, *, tm, fchunk):
    n, d = x1.shape
    f = wg.shape[1]
    row = lambda w: pl.BlockSpec((tm, w), lambda i: (i, 0))
    direction = lambda k: pl.BlockSpec((1, tm, GDN_WIDTH), lambda i: (k, i, 0))
    return pl.pallas_call(
        functools.partial(_out_ffn2_kernel, fchunk=fchunk),
        grid=(n // tm,),
        in_specs=[row(d), direction(0), direction(1), row(GDN_WIDTH), row(SWA_WIDTH),
                  _resident((1, GDN_HEAD_DIM)), _resident(wout.shape), _resident((1, d)),
                  _resident((d, f)), _resident((d, f)), _resident((f, d)), _resident((1, d))],
        out_specs=row(d),
        out_shape=jax.ShapeDtypeStruct((n, d), F32),
        compiler_params=_cparams(("arbitrary",)),
        name="out_ffn2",
    )(x1, o_dirs, o_dirs, z, attn, gnw, wout, n2, wg, wu, wd, nf)


def _gdn_prep_kernel(cur_ref, prev_ref, next_ref, ab_ref, cw_ref, alog_ref, dtb_ref,
                     q_ref, k_ref, v_ref, gcol_ref, grow_ref, pad_ref, *, ts):
    j = pl.program_id(1)
    nj = pl.num_programs(1)
    half = ts // 2
    for cb in range(3 * GDN_HEADS):
        cs = slice(cb * LANES, (cb + 1) * LANES)
        pad_ref[cb, 0:8, :] = jnp.where(j > 0, prev_ref[0, :, cs], 0.0)
        pad_ref[cb, 8:8 + ts, :] = cur_ref[0, :, cs]
        pad_ref[cb, 8 + ts:16 + ts, :] = jnp.where(j < nj - 1, next_ref[0, :, cs], 0.0)
        dst = (q_ref, k_ref, v_ref)[cb // GDN_HEADS]
        for par in range(2):
            acc = None
            for t in range(CONV_WIDTH):
                term = pad_ref[cb, pl.ds(6 + t + par, half, stride=2), :] * cw_ref[t:t + 1, cs]
                acc = term if acc is None else acc + term
            y = _silu(acc)
            if cb < 2 * GDN_HEADS:
                y = y * lax.rsqrt(jnp.sum(y * y, axis=-1, keepdims=True) + EPS)
            if cb < GDN_HEADS:
                y = y * (GDN_HEAD_DIM ** -0.5)
            dst[0, cb % GDN_HEADS, pl.ds(par, half, stride=2), :] = y

    ab = ab_ref[0]
    xs = ab + dtb_ref[...]
    softplus = jnp.maximum(xs, 0.0) + jnp.log(1.0 + jnp.exp(-jnp.abs(xs)))
    g = -jnp.exp(alog_ref[...]) * softplus
    beta = 1.0 / (1.0 + jnp.exp(-ab))
    g1 = g.astype(BF16)
    r1 = g - g1.astype(F32)
    g2 = r1.astype(BF16)
    g3 = (r1 - g2.astype(F32)).astype(BF16)
    ri = lax.broadcasted_iota(jnp.int32, (LANES, LANES), 0)
    ci = lax.broadcasted_iota(jnp.int32, (LANES, LANES), 1)
    same = (ri // CHUNK) == (ci // CHUNK)
    lower = jnp.where(same & (ri >= ci), 1.0, 0.0).astype(BF16)
    upper = jnp.where(same & (ri <= ci), 1.0, 0.0).astype(BF16)
    lane = lax.broadcasted_iota(jnp.int32, (1, LANES), 1)
    for s in range(ts // LANES):
        rs = slice(s * LANES, (s + 1) * LANES)
        pre = _dot(lower, g1[rs]) + _dot(lower, g2[rs]) + _dot(lower, g3[rs])
        suf = _dot(upper, g1[rs]) + _dot(upper, g2[rs]) + _dot(upper, g3[rs])
        gs = g[rs]
        col = jnp.where(lane < 4, pre,
              jnp.where(lane < 8, suf,
              jnp.where(lane < 16, beta[rs],
              jnp.where(lane < 20, suf - gs,
              jnp.where(lane < 24, pre - gs, pre + suf - gs)))))
        gcol_ref[0, rs, :] = col
        colt = col.T
        grow_ref[0, 2 * s] = colt[0:8, 0:CHUNK]
        grow_ref[0, 2 * s + 1] = colt[0:8, CHUNK:2 * CHUNK]


def _gdn_prep(qkva, ab, cw, alog, dtb, *, ts):
    b, s, w = qkva.shape
    nblk = s // ts
    r8 = ts // 8
    last8 = s // 8 - 1
    f = lambda shape: jax.ShapeDtypeStruct(shape, F32)
    return pl.pallas_call(
        functools.partial(_gdn_prep_kernel, ts=ts),
        grid=(b, nblk),
        in_specs=[pl.BlockSpec((1, ts, w), lambda i, j: (i, j, 0)),
                  pl.BlockSpec((1, 8, w), lambda i, j: (i, jnp.maximum(j * r8 - 1, 0), 0)),
                  pl.BlockSpec((1, 8, w), lambda i, j: (i, jnp.minimum((j + 1) * r8, last8), 0)),
                  pl.BlockSpec((1, ts, LANES), lambda i, j: (i, j, 0)),
                  _resident(cw.shape), _resident((1, LANES)), _resident((1, LANES))],
        out_specs=[pl.BlockSpec((1, GDN_HEADS, ts, GDN_HEAD_DIM), lambda i, j: (i, 0, j, 0))] * 3
                  + [pl.BlockSpec((1, ts, LANES), lambda i, j: (i, j, 0)),
                     pl.BlockSpec((1, ts // CHUNK, 8, CHUNK), lambda i, j: (i, j, 0, 0))],
        out_shape=[f((b, GDN_HEADS, s, GDN_HEAD_DIM))] * 3 + [f((b, s, LANES)), f((b, s // CHUNK, 8, CHUNK))],
        scratch_shapes=[pltpu.VMEM((w // LANES, ts + 16, LANES), F32)],
        compiler_params=_cparams(("arbitrary", "arbitrary")),
        name="gdn_prep",
    )(qkva, qkva, qkva, ab, cw, alog, dtb)


CHUNKS_PER_STEP = 4

def _gdn_scan_kernel(q_ref, k_ref, v_ref, gcol_ref, grow_ref, o_ref,
                     s_ref, wqg_ref, u_ref, ik_ref, eg_ref, *, nc):
    dirn = pl.program_id(1)
    fwd = dirn == 0

    @pl.when(pl.program_id(2) == 0)
    def _():
        s_ref[...] = jnp.zeros(s_ref.shape, F32)

    ri = lax.broadcasted_iota(jnp.int32, (CHUNK, CHUNK), 0)
    ci = lax.broadcasted_iota(jnp.int32, (CHUNK, CHUNK), 1)
    dd = (ri - ci) * (1 - 2 * dirn)
    incl = dd >= 0
    strict = dd > 0
    eye = jnp.where(ri == ci, 1.0, 0.0)
    left = lax.broadcasted_iota(jnp.int32, (CHUNK, 2 * CHUNK), 1) < CHUNK
    nt = (((1,), (1,)), ((), ()))

    nsq = int(math.log2(CHUNK))

    def chunk_matrices(ci, carry):
        probs = [(ci * CHUNKS_PER_STEP + cc, h) for cc in range(CHUNKS_PER_STEP) for h in range(GDN_HEADS)]
        qs, ks, egs, rs, ps, intras = [], [], [], [], [], []
        for c, h in probs:
            r0 = pl.multiple_of(c * CHUNK, CHUNK)
            gates = gcol_ref[0, pl.ds(r0, CHUNK), :]
            rows = grow_ref[0, c]

            def pick(base):
                return jnp.where(fwd, gates[:, base + h:base + h + 1], gates[:, base + 4 + h:base + 5 + h])
            gc, beta, gdec, gl = pick(0), pick(8), pick(16), pick(24)
            grow = jnp.where(fwd, rows[h:h + 1, :], rows[4 + h:5 + h, :])
            qc = q_ref[0, h, pl.ds(r0, CHUNK), :]
            kc = k_ref[0, h, pl.ds(r0, CHUNK), :]
            vc = v_ref[0, h, pl.ds(r0, CHUNK), :]
            dec = jnp.where(incl, jnp.exp(jnp.where(incl, gc - grow, 0.0)), 0.0)
            kb = kc * beta
            qk = lax.dot_general(jnp.concatenate([kb, qc], axis=0).astype(BF16), kc.astype(BF16), nt,
                                 preferred_element_type=F32)
            ps.append(-jnp.where(strict, qk[:CHUNK] * dec, 0.0))
            intras.append(qk[CHUNK:] * dec)
            eg = jnp.exp(gc)
            rs.append(jnp.concatenate([vc * beta, kb * eg], axis=1))
            qs.append(qc * eg)
            ks.append(kc * jnp.exp(gdec))
            egs.append(jnp.exp(gl[0:1, :]))
        xs = [jnp.concatenate([p, eye], axis=1) for p in ps]
        for t in range(nsq):
            for n in range(len(probs)):
                xb = xs[n].astype(BF16)
                y = _dot(xb[:, :CHUNK], xb)
                xs[n] = jnp.where(left, y, xs[n] + y)
        for n in range(len(probs)):
            t_off = (xs[n][:, CHUNK:] - eye).astype(BF16)
            rs[n] = rs[n] + _dot(t_off, rs[n].astype(BF16))
        for n, (c, h) in enumerate(probs):
            u_ref[c, h] = rs[n][:, :GDN_HEAD_DIM]
            wqg_ref[c, h] = jnp.concatenate([rs[n][:, GDN_HEAD_DIM:], qs[n]], axis=0).astype(BF16)
            ik_ref[c, h] = jnp.concatenate([intras[n], ks[n].T], axis=0).astype(BF16)
            eg_ref[c, h] = jnp.broadcast_to(egs[n], (8, GDN_HEAD_DIM))
        return carry

    lax.fori_loop(0, nc // CHUNKS_PER_STEP, chunk_matrices, 0)

    def scan_step(i, carry):
        c = i + dirn * (nc - 1 - 2 * i)
        r0 = pl.multiple_of(c * CHUNK, CHUNK)
        states = [s_ref[h] for h in range(GDN_HEADS)]
        m1 = [_dot(wqg_ref[c, h], states[h].astype(BF16)) for h in range(GDN_HEADS)]
        v_new = [(u_ref[c, h] - m1[h][:CHUNK]).astype(BF16) for h in range(GDN_HEADS)]
        m2 = [_dot(ik_ref[c, h], v_new[h]) for h in range(GDN_HEADS)]
        for h in range(GDN_HEADS):
            hs = slice(h * GDN_HEAD_DIM, (h + 1) * GDN_HEAD_DIM)
            o_ref[0, 0, pl.ds(r0, CHUNK), hs] = m1[h][CHUNK:] + m2[h][:CHUNK]
            s_ref[h] = states[h] * eg_ref[c, h][0:1, :] + m2[h][CHUNK:]
        return carry

    lax.fori_loop(0, nc, scan_step, 0)


def _gdn_scan(q, k, v, gcol, grow, *, ts):
    b, nh, s, hd = q.shape
    w = nh * hd
    nblk = s // ts
    nc = ts // CHUNK
    blk = lambda i, d, j: (i, j + d * (nblk - 1 - 2 * j), 0)
    return pl.pallas_call(
        functools.partial(_gdn_scan_kernel, nc=nc),
        grid=(b, 2, nblk),
        in_specs=[pl.BlockSpec((1, nh, ts, hd), lambda i, d, j: (i, 0, j + d * (nblk - 1 - 2 * j), 0))] * 3
                 + [pl.BlockSpec((1, ts, LANES), blk),
                    pl.BlockSpec((1, nc, 8, CHUNK), lambda i, d, j: blk(i, d, j) + (0,))],
        out_specs=pl.BlockSpec((1, 1, ts, w), lambda i, d, j: (d,) + blk(i, d, j)),
        out_shape=jax.ShapeDtypeStruct((2, b, s, w), F32),
        scratch_shapes=[pltpu.VMEM((GDN_HEADS, GDN_HEAD_DIM, GDN_HEAD_DIM), F32),
                        pltpu.VMEM((nc, GDN_HEADS, 2 * CHUNK, GDN_HEAD_DIM), BF16),
                        pltpu.VMEM((nc, GDN_HEADS, CHUNK, GDN_HEAD_DIM), F32),
                        pltpu.VMEM((nc, GDN_HEADS, CHUNK + GDN_HEAD_DIM, CHUNK), BF16),
                        pltpu.VMEM((nc, GDN_HEADS, 8, GDN_HEAD_DIM), F32)],
        compiler_params=_cparams(("arbitrary", "arbitrary", "arbitrary")),
        name="gdn_scan",
    )(q, k, v, gcol, grow)


SWA_TQ = 128
SWA_TK = SWA_TQ + 2 * BAND_RADIUS
SWA_HALO = BAND_RADIUS * max(DILATIONS)
NORM_ROWS = 512
SWA_TILES_PER_STEP = 4


def _swa_kernel(q_ref, k_ref, v_ref, qw_ref, kw_ref, bias_ref, o_ref,
                qn_ref, kn_ref, vp_ref, m0_ref, m1_ref, l_ref, acc_ref, bias_scr, *, seq, qb):
    qi = pl.program_id(2)
    lane = lax.broadcasted_iota(jnp.int32, (1, LANES), 1)
    low = lane < SWA_HEAD_DIM
    nt = (((1,), (1,)), ((), ()))

    @pl.when(qi == 0)
    def _():
        for pi in range(len(DILATIONS)):
            for hh in range(2):
                full = jnp.broadcast_to(bias_ref[0, pi, hh], (SWA_TQ, SWA_TK))
                bias_scr[pi, hh] = pltpu.roll(full, 0, 1, stride=1, stride_axis=0)

        zeros = jnp.zeros((SWA_HALO, LANES), F32)
        for ref in (kn_ref, vp_ref):
            ref[0:SWA_HALO, :] = zeros
            ref[SWA_HALO + seq:2 * SWA_HALO + seq, :] = zeros

        def head_rms(x, w):
            x2 = x * x
            s0 = jnp.sum(jnp.where(low, x2, 0.0), axis=-1, keepdims=True)
            s1 = jnp.sum(jnp.where(low, 0.0, x2), axis=-1, keepdims=True)
            ms = jnp.where(low, s0, s1) * (1.0 / SWA_HEAD_DIM)
            return x * lax.rsqrt(ms + EPS) * w

        def norm_rows(i, carry):
            r0 = pl.multiple_of(i * NORM_ROWS, NORM_ROWS)
            qn_ref[pl.ds(r0, NORM_ROWS), :] = (head_rms(q_ref[0, pl.ds(r0, NORM_ROWS), :], qw_ref[...])
                                               * (SWA_HEAD_DIM ** -0.5 * LOG2E))
            kn_ref[pl.ds(SWA_HALO + r0, NORM_ROWS), :] = head_rms(k_ref[0, pl.ds(r0, NORM_ROWS), :], kw_ref[...])
            vp_ref[pl.ds(SWA_HALO + r0, NORM_ROWS), :] = v_ref[0, pl.ds(r0, NORM_ROWS), :]
            return carry

        lax.fori_loop(0, seq // NORM_ROWS, norm_rows, 0)

    kcol = lax.broadcasted_iota(jnp.int32, (1, SWA_TK), 1)

    def rows(ref, start, size, stride):
        if stride == 1:
            return ref[pl.ds(start, size), :]
        return ref[pl.ds(start, size, stride=stride), :]

    def put(ref, start, size, stride, val):
        if stride == 1:
            ref[pl.ds(start, size), :] = val
        else:
            ref[pl.ds(start, size, stride=stride), :] = val

    for pi, dil in enumerate(DILATIONS):
        sub_len = seq // dil
        tiles_per_res = qb // dil // SWA_TQ
        shift = int(math.log2(tiles_per_res))
        first = pi == 0

        def tiles(it, carry, pi=pi, dil=dil, sub_len=sub_len, tiles_per_res=tiles_per_res, shift=shift, first=first):
            loc, vts, old, ss = [], [], [], []
            for u in range(SWA_TILES_PER_STEP):
                i = it * SWA_TILES_PER_STEP + u
                res = lax.shift_right_logical(i, shift)
                t = jnp.bitwise_and(i, tiles_per_res - 1)
                tau0 = qi * (qb // dil) + t * SWA_TQ
                loc0 = res + dil * (t * SWA_TQ)
                krow = SWA_HALO + qi * qb + loc0 - dil * BAND_RADIUS
                qt = rows(qn_ref, qi * qb + loc0, SWA_TQ, dil)
                kt = rows(kn_ref, krow, SWA_TK, dil).astype(BF16)
                vt = rows(vp_ref, krow, SWA_TK, dil)
                vts.append((jnp.where(low, vt, 1.0).astype(BF16), jnp.where(low, 1.0, vt).astype(BF16)))
                kidx = kcol + (tau0 - BAND_RADIUS)
                kvalid = (kidx >= 0) & (kidx < sub_len)
                loc.append(loc0)
                if not first:
                    old.append(tuple(rows(ref, loc0, SWA_TQ, dil) for ref in (m0_ref, m1_ref, l_ref, acc_ref)))
                for hh in range(2):
                    mine = low if hh == 0 else jnp.logical_not(low)
                    qh = jnp.where(mine, qt, 0.0).astype(BF16)
                    s = lax.dot_general(qh, kt, nt, preferred_element_type=F32)
                    ss.append(jnp.where(kvalid, s + bias_scr[pi, hh], NEG_BIG))
            m_new, alpha, ps = [], [], []
            for n, s in enumerate(ss):
                u, hh = divmod(n, 2)
                mt = jnp.max(s, axis=-1, keepdims=True)
                if first:
                    mn = jnp.broadcast_to(mt, (SWA_TQ, LANES))
                else:
                    mo = old[u][hh]
                    mn = jnp.maximum(mo, mt)
                    alpha.append(jnp.exp2(mo - mn))
                m_new.append(mn)
                ps.append(jnp.exp2(s - jnp.concatenate([mn, mn], axis=1)).astype(BF16))
            pv = [_dot(p, vts[n // 2][n % 2]) for n, p in enumerate(ps)]
            for u in range(SWA_TILES_PER_STEP):
                pv0, pv1 = pv[2 * u], pv[2 * u + 1]
                put(m0_ref, loc[u], SWA_TQ, dil, m_new[2 * u])
                put(m1_ref, loc[u], SWA_TQ, dil, m_new[2 * u + 1])
                lsum = jnp.where(low, pv1, pv0)
                pvv = jnp.where(low, pv0, pv1)
                if first:
                    put(l_ref, loc[u], SWA_TQ, dil, lsum)
                    put(acc_ref, loc[u], SWA_TQ, dil, pvv)
                else:
                    a0, a1 = alpha[2 * u], alpha[2 * u + 1]
                    put(l_ref, loc[u], SWA_TQ, dil, old[u][2] * jnp.where(low, a1, a0) + lsum)
                    put(acc_ref, loc[u], SWA_TQ, dil, old[u][3] * jnp.where(low, a0, a1) + pvv)
            return carry

        lax.fori_loop(0, qb // SWA_TQ // SWA_TILES_PER_STEP, tiles, 0)

    o_ref[0] = acc_ref[...] / pltpu.roll(l_ref[...], SWA_HEAD_DIM, 1)


def _t5_bucket(rel):
    nb = REL_BUCKETS // 2
    bucket = (rel > 0).astype(np.int32) * nb
    n = np.abs(rel)
    max_exact = nb // 2
    large = max_exact + (np.log(np.maximum(n, 1) / max_exact)
                         / math.log(REL_MAX_DISTANCE / max_exact) * (nb - max_exact)).astype(np.int32)
    large = np.minimum(large, nb - 1)
    return (bucket + np.where(n < max_exact, n, large)).astype(np.int32)


def _band_bias(rel_bias):
    rel = np.arange(2 * BAND_RADIUS + 1) - BAND_RADIUS
    rows = []
    for dil in DILATIONS:
        inside = jnp.transpose(rel_bias[_t5_bucket(rel * dil)]) * LOG2E
        rows.append(jnp.pad(inside, ((0, 0), (0, SWA_TK - inside.shape[1])), constant_values=NEG_BIG))
    bias = jnp.stack(rows, axis=0)
    bias = bias.reshape(len(DILATIONS), SWA_HEADS // 2, 2, 1, SWA_TK)
    return jnp.transpose(bias, (1, 0, 2, 3, 4)).astype(F32)


def _swa(qkvb, qw, kw, bias, *, qb):
    b, s, _ = qkvb.shape
    pairs = SWA_HEADS // 2
    assert s % qb == 0 and qb % (max(DILATIONS) * SWA_TQ) == 0
    col = lambda base: pl.BlockSpec((1, s, LANES), lambda i, p, j: (i, 0, base + p))
    padded = s + 2 * SWA_HALO
    return pl.pallas_call(
        functools.partial(_swa_kernel, seq=s, qb=qb),
        grid=(b, pairs, s // qb),
        in_specs=[col(0), col(pairs), col(2 * pairs), _resident((1, LANES)), _resident((1, LANES)),
                  pl.BlockSpec((1,) + bias.shape[1:], lambda i, p, j: (p, 0, 0, 0, 0))],
        out_specs=pl.BlockSpec((1, qb, LANES), lambda i, p, j: (i, j, p)),
        out_shape=jax.ShapeDtypeStruct((b, s, SWA_WIDTH), F32),
        scratch_shapes=[pltpu.VMEM((s, LANES), F32), pltpu.VMEM((padded, LANES), F32),
                        pltpu.VMEM((padded, LANES), F32)] + [pltpu.VMEM((qb, LANES), F32)] * 4
                       + [pltpu.VMEM((len(DILATIONS), 2, SWA_TQ, SWA_TK), F32)],
        compiler_params=_cparams(("arbitrary", "arbitrary", "arbitrary")),
        name="swa",
    )(qkvb, qkvb, qkvb, qw, kw, bias)


FFN_TM = 512
FFN_FCHUNK = 256
PREP_TS = 512
SCAN_TS = 1024
SWA_QB = 2048


def _lane_row(vals, copies_at):
    row = jnp.zeros((LANES,), F32)
    for off in copies_at:
        row = row.at[off:off + vals.shape[0]].set(vals)
    return row[None, :]


def kernel(x, ffn1_norm, ffn1_w_gate, ffn1_w_up, ffn1_w_down, mix_norm, w_in, conv_w, a_log, dt_bias, gdn_norm_w, q_norm_w, k_norm_w, rel_bias, w_out, ffn2_norm, ffn2_w_gate, ffn2_w_up, ffn2_w_down, final_norm):
    b, s, d = x.shape
    n = b * s
    x2d = x.reshape(n, d)
    ng = 2 * GDN_HEADS
    c_z = 3 * GDN_WIDTH
    c_a = c_z + GDN_WIDTH
    c_b = c_a + ng
    c_qkvb = c_b + ng
    bias = _band_bias(rel_bias)
    for l in range(ffn1_norm.shape[0]):
        wl = w_in[l]
        w_a = wl[:, c_a:c_b]
        w_gates = jnp.concatenate([w_a, wl[:, c_b:c_qkvb], w_a, w_a, jnp.zeros((d, LANES - 4 * ng), F32)], axis=1)
        win = jnp.concatenate([wl[:, :c_a], w_gates, wl[:, c_qkvb:]], axis=1).astype(BF16)
        copies = (0, 2 * ng, 3 * ng)
        alog = _lane_row(a_log[l].reshape(ng), copies)
        dtb = _lane_row(dt_bias[l].reshape(ng), copies)
        x1, qkva, z, ab, qkvb = _ffn1_proj(
            x2d, ffn1_norm[l][None, :], ffn1_w_gate[l].astype(BF16), ffn1_w_up[l].astype(BF16),
            ffn1_w_down[l].astype(BF16), mix_norm[l][None, :], win, tm=FFN_TM, fchunk=FFN_FCHUNK)
        q, k, v, gcol, grow = _gdn_prep(
            qkva.reshape(b, s, -1), ab.reshape(b, s, LANES), jnp.transpose(conv_w[l]), alog, dtb, ts=PREP_TS)
        o_dirs = _gdn_scan(q, k, v, gcol, grow, ts=SCAN_TS)
        attn = _swa(qkvb.reshape(b, s, -1), jnp.tile(q_norm_w[l], 2)[None, :], jnp.tile(k_norm_w[l], 2)[None, :],
                    bias, qb=SWA_QB)
        x2d = _out_ffn2(
            x1, o_dirs.reshape(2, n, -1), z, attn.reshape(n, -1),
            gdn_norm_w[l][None, :], w_out[l].astype(BF16), ffn2_norm[l][None, :],
            ffn2_w_gate[l].astype(BF16), ffn2_w_up[l].astype(BF16), ffn2_w_down[l].astype(BF16),
            final_norm[l][None, :], tm=FFN_TM, fchunk=FFN_FCHUNK)
    return x2d.reshape(b, s, d)
```

```python
import functools
import math

import numpy as np
import jax
import jax.numpy as jnp
from jax import lax
from jax.experimental import pallas as pl
from jax.experimental.pallas import tpu as pltpu

F32 = jnp.float32
BF16 = jnp.bfloat16
EPS = 1e-6
NEG_BIG = -1e30
LOG2E = math.log2(math.e)

LANES = 128
GDN_HEADS = 4
GDN_HEAD_DIM = 128
GDN_WIDTH = GDN_HEADS * GDN_HEAD_DIM
CONV_WIDTH = 5
CHUNK = 64
SWA_HEADS = 8
SWA_HEAD_DIM = 64
SWA_WIDTH = SWA_HEADS * SWA_HEAD_DIM
DILATIONS = (1, 4, 16)
BAND_RADIUS = 64
REL_BUCKETS = 32
REL_MAX_DISTANCE = 1024
VMEM_LIMIT = 56 * 1024 * 1024


def _cparams(sem):
    return pltpu.CompilerParams(dimension_semantics=sem, vmem_limit_bytes=VMEM_LIMIT)


def _resident(shape):
    zeros = (0,) * len(shape)
    return pl.BlockSpec(shape, lambda *_: zeros, pipeline_mode=pl.Buffered(1))


def _rms(x, w):
    return x * lax.rsqrt(jnp.mean(x * x, axis=-1, keepdims=True) + EPS) * w


def _silu(x):
    return x * (1.0 / (1.0 + jnp.exp(-x)))


def _dot(a, b):
    return jnp.dot(a, b, preferred_element_type=F32)


def _swiglu(h, wg_ref, wu_ref, wd_ref, fchunk):
    acc = None
    for c0 in range(0, wg_ref.shape[1], fchunk):
        g = _dot(h, wg_ref[:, c0:c0 + fchunk])
        u = _dot(h, wu_ref[:, c0:c0 + fchunk])
        a = (_silu(g) * u).astype(BF16)
        d = _dot(a, wd_ref[c0:c0 + fchunk, :])
        acc = d if acc is None else acc + d
    return acc


def _ffn1_proj_kernel(x_ref, n1_ref, wg_ref, wu_ref, wd_ref, nm_ref, win_ref,
                      x1_ref, qkva_ref, z_ref, ab_ref, qkvb_ref, *, fchunk):
    x = x_ref[...]
    h = _rms(x, n1_ref[...]).astype(BF16)
    x1 = x + 0.5 * _swiglu(h, wg_ref, wu_ref, wd_ref, fchunk)
    x1_ref[...] = x1
    h2 = _rms(x1, nm_ref[...]).astype(BF16)
    c = 0
    for ref in (qkva_ref, z_ref, ab_ref, qkvb_ref):
        n = ref.shape[1]
        ref[...] = _dot(h2, win_ref[:, c:c + n])
        c += n


def _ffn1_proj(x2d, n1, wg, wu, wd, nm, win, *, tm, fchunk):
    n, d = x2d.shape
    f = wg.shape[1]
    widths = (3 * GDN_WIDTH, GDN_WIDTH, LANES, 3 * SWA_WIDTH)
    assert win.shape[1] == sum(widths) and n % tm == 0
    row = lambda w: pl.BlockSpec((tm, w), lambda i: (i, 0))
    return pl.pallas_call(
        functools.partial(_ffn1_proj_kernel, fchunk=fchunk),
        grid=(n // tm,),
        in_specs=[row(d), _resident((1, d)), _resident((d, f)), _resident((d, f)), _resident((f, d)),
                  _resident((1, d)), _resident(win.shape)],
        out_specs=[row(d)] + [row(w) for w in widths],
        out_shape=[jax.ShapeDtypeStruct((n, d), F32)] + [jax.ShapeDtypeStruct((n, w), F32) for w in widths],
        compiler_params=_cparams(("arbitrary",)),
        name="ffn1_proj",
    )(x2d, n1, wg, wu, wd, nm, win)


def _out_ffn2_kernel(x1_ref, of_ref, ob_ref, z_ref, attn_ref, gnw_ref, wout_ref, n2_ref,
                     wg_ref, wu_ref, wd_ref, nf_ref, out_ref, *, fchunk):
    o = of_ref[0] + ob_ref[0]
    z = z_ref[...]
    gnw = gnw_ref[...]
    heads = []
    for h in range(GDN_HEADS):
        sl = slice(h * GDN_HEAD_DIM, (h + 1) * GDN_HEAD_DIM)
        heads.append(_rms(o[:, sl], gnw) * _silu(z[:, sl]))
    oa = jnp.concatenate(heads, axis=1).astype(BF16)
    mix = _dot(oa, wout_ref[:GDN_WIDTH, :]) + _dot(attn_ref[...].astype(BF16), wout_ref[GDN_WIDTH:, :])
    x2 = x1_ref[...] + mix
    h2 = _rms(x2, n2_ref[...]).astype(BF16)
    x3 = x2 + 0.5 * _swiglu(h2, wg_ref, wu_ref, wd_ref, fchunk)
    out_ref[...] = _rms(x3, nf_ref[...])


def _out_ffn2(x1, o_dirs, z, attn, gnw, wout, n2, wg, wu, wd, nf, *, tm, fchunk):
    n, d = x1.shape
    f = wg.shape[1]
    row = lambda w: pl.BlockSpec((tm, w), lambda i: (i, 0))
    direction = lambda k: pl.BlockSpec((1, tm, GDN_WIDTH), lambda i: (k, i, 0))
    return pl.pallas_call(
        functools.partial(_out_ffn2_kernel, fchunk=fchunk),
        grid=(n // tm,),
        in_specs=[row(d), direction(0), direction(1), row(GDN_WIDTH), row(SWA_WIDTH),
                  _resident((1, GDN_HEAD_DIM)), _resident(wout.shape), _resident((1, d)),
                  _resident((d, f)), _resident((d, f)), _resident((f, d)), _resident((1, d))],
        out_specs=row(d),
        out_shape=jax.ShapeDtypeStruct((n, d), F32),
        compiler_params=_cparams(("arbitrary",)),
        name="out_ffn2",
    )(x1, o_dirs, o_dirs, z, attn, gnw, wout, n2, wg, wu, wd, nf)


def _gdn_prep_kernel(cur_ref, prev_ref, next_ref, ab_ref, cw_ref, alog_ref, dtb_ref,
                     q_ref, k_ref, v_ref, gcol_ref, grow_ref, pad_ref, *, ts):
    j = pl.program_id(1)
    nj = pl.num_programs(1)
    half = ts // 2
    for cb in range(3 * GDN_HEADS):
        cs = slice(cb * LANES, (cb + 1) * LANES)
        pad_ref[cb, 0:8, :] = jnp.where(j > 0, prev_ref[0, :, cs], 0.0)
        pad_ref[cb, 8:8 + ts, :] = cur_ref[0, :, cs]
        pad_ref[cb, 8 + ts:16 + ts, :] = jnp.where(j < nj - 1, next_ref[0, :, cs], 0.0)
        dst = (q_ref, k_ref, v_ref)[cb // GDN_HEADS]
        for par in range(2):
            acc = None
            for t in range(CONV_WIDTH):
                term = pad_ref[cb, pl.ds(6 + t + par, half, stride=2), :] * cw_ref[t:t + 1, cs]
                acc = term if acc is None else acc + term
            y = _silu(acc)
            if cb < 2 * GDN_HEADS:
                y = y * lax.rsqrt(jnp.sum(y * y, axis=-1, keepdims=True) + EPS)
            if cb < GDN_HEADS:
                y = y * (GDN_HEAD_DIM ** -0.5)
            dst[0, cb % GDN_HEADS, pl.ds(par, half, stride=2), :] = y

    ab = ab_ref[0]
    xs = ab + dtb_ref[...]
    softplus = jnp.maximum(xs, 0.0) + jnp.log(1.0 + jnp.exp(-jnp.abs(xs)))
    g = -jnp.exp(alog_ref[...]) * softplus
    beta = 1.0 / (1.0 + jnp.exp(-ab))
    g1 = g.astype(BF16)
    r1 = g - g1.astype(F32)
    g2 = r1.astype(BF16)
    g3 = (r1 - g2.astype(F32)).astype(BF16)
    ri = lax.broadcasted_iota(jnp.int32, (LANES, LANES), 0)
    ci = lax.broadcasted_iota(jnp.int32, (LANES, LANES), 1)
    same = (ri // CHUNK) == (ci // CHUNK)
    lower = jnp.where(same & (ri >= ci), 1.0, 0.0).astype(BF16)
    upper = jnp.where(same & (ri <= ci), 1.0, 0.0).astype(BF16)
    lane = lax.broadcasted_iota(jnp.int32, (1, LANES), 1)
    for s in range(ts // LANES):
        rs = slice(s * LANES, (s + 1) * LANES)
        pre = _dot(lower, g1[rs]) + _dot(lower, g2[rs]) + _dot(lower, g3[rs])
        suf = _dot(upper, g1[rs]) + _dot(upper, g2[rs]) + _dot(upper, g3[rs])
        gs = g[rs]
        col = jnp.where(lane < 4, pre,
              jnp.where(lane < 8, suf,
              jnp.where(lane < 16, beta[rs],
              jnp.where(lane < 20, suf - gs,
              jnp.where(lane < 24, pre - gs, pre + suf - gs)))))
        gcol_ref[0, rs, :] = col
        colt = col.T
        grow_ref[0, 2 * s] = colt[0:8, 0:CHUNK]
        grow_ref[0, 2 * s + 1] = colt[0:8, CHUNK:2 * CHUNK]


def _gdn_prep(qkva, ab, cw, alog, dtb, *, ts):
    b, s, w = qkva.shape
    nblk = s // ts
    r8 = ts // 8
    last8 = s // 8 - 1
    f = lambda shape: jax.ShapeDtypeStruct(shape, F32)
    return pl.pallas_call(
        functools.partial(_gdn_prep_kernel, ts=ts),
        grid=(b, nblk),
        in_specs=[pl.BlockSpec((1, ts, w), lambda i, j: (i, j, 0)),
                  pl.BlockSpec((1, 8, w), lambda i, j: (i, jnp.maximum(j * r8 - 1, 0), 0)),
                  pl.BlockSpec((1, 8, w), lambda i, j: (i, jnp.minimum((j + 1) * r8, last8), 0)),
                  pl.BlockSpec((1, ts, LANES), lambda i, j: (i, j, 0)),
                  _resident(cw.shape), _resident((1, LANES)), _resident((1, LANES))],
        out_specs=[pl.BlockSpec((1, GDN_HEADS, ts, GDN_HEAD_DIM), lambda i, j: (i, 0, j, 0))] * 3
                  + [pl.BlockSpec((1, ts, LANES), lambda i, j: (i, j, 0)),
                     pl.BlockSpec((1, ts // CHUNK, 8, CHUNK), lambda i, j: (i, j, 0, 0))],
        out_shape=[f((b, GDN_HEADS, s, GDN_HEAD_DIM))] * 3 + [f((b, s, LANES)), f((b, s // CHUNK, 8, CHUNK))],
        scratch_shapes=[pltpu.VMEM((w // LANES, ts + 16, LANES), F32)],
        compiler_params=_cparams(("arbitrary", "arbitrary")),
        name="gdn_prep",
    )(qkva, qkva, qkva, ab, cw, alog, dtb)


CHUNKS_PER_STEP = 4

def _gdn_scan_kernel(q_ref, k_ref, v_ref, gcol_ref, grow_ref, o_ref,
                     s_ref, wqg_ref, u_ref, ik_ref, eg_ref, *, nc, nblk):
    dirn = pl.program_id(1)
    j = pl.program_id(2)
    fwd = dirn == 0
    slot_build = lax.rem(j, 2)
    slot_scan = 1 - slot_build

    ri = lax.broadcasted_iota(jnp.int32, (CHUNK, CHUNK), 0)
    ci = lax.broadcasted_iota(jnp.int32, (CHUNK, CHUNK), 1)
    dd = (ri - ci) * (1 - 2 * dirn)
    incl = dd >= 0
    strict = dd > 0
    eye = jnp.where(ri == ci, 1.0, 0.0)
    left = lax.broadcasted_iota(jnp.int32, (CHUNK, 2 * CHUNK), 1) < CHUNK
    nt = (((1,), (1,)), ((), ()))

    nsq = int(math.log2(CHUNK))

    def chunk_matrices(ci):
        probs = [(ci * CHUNKS_PER_STEP + cc, h) for cc in range(CHUNKS_PER_STEP) for h in range(GDN_HEADS)]
        qs, ks, egs, rs, ps, intras = [], [], [], [], [], []
        for c, h in probs:
            r0 = pl.multiple_of(c * CHUNK, CHUNK)
            gates = gcol_ref[0, pl.ds(r0, CHUNK), :]
            rows = grow_ref[0, c]

            def pick(base):
                return jnp.where(fwd, gates[:, base + h:base + h + 1], gates[:, base + 4 + h:base + 5 + h])
            gc, beta, gdec, gl = pick(0), pick(8), pick(16), pick(24)
            grow = jnp.where(fwd, rows[h:h + 1, :], rows[4 + h:5 + h, :])
            qc = q_ref[0, h, pl.ds(r0, CHUNK), :]
            kc = k_ref[0, h, pl.ds(r0, CHUNK), :]
            vc = v_ref[0, h, pl.ds(r0, CHUNK), :]
            dec = jnp.where(incl, jnp.exp(jnp.where(incl, gc - grow, 0.0)), 0.0)
            kb = kc * beta
            qk = lax.dot_general(jnp.concatenate([kb, qc], axis=0).astype(BF16), kc.astype(BF16), nt,
                                 preferred_element_type=F32)
            ps.append(-jnp.where(strict, qk[:CHUNK] * dec, 0.0))
            intras.append(qk[CHUNK:] * dec)
            eg = jnp.exp(gc)
            rs.append(jnp.concatenate([vc * beta, kb * eg], axis=1))
            qs.append(qc * eg)
            ks.append(kc * jnp.exp(gdec))
            egs.append(jnp.exp(gl[0:1, :]))
        yield
        xs = [jnp.concatenate([p, eye], axis=1) for p in ps]
        for t in range(nsq):
            for n in range(len(probs)):
                xb = xs[n].astype(BF16)
                y = _dot(xb[:, :CHUNK], xb)
                xs[n] = jnp.where(left, y, xs[n] + y)
            yield
        for n in range(len(probs)):
            t_off = (xs[n][:, CHUNK:] - eye).astype(BF16)
            rs[n] = rs[n] + _dot(t_off, rs[n].astype(BF16))
        for n, (c, h) in enumerate(probs):
            u_ref[slot_build, c, h] = rs[n][:, :GDN_HEAD_DIM]
            wqg_ref[slot_build, c, h] = jnp.concatenate([rs[n][:, GDN_HEAD_DIM:], qs[n]], axis=0).astype(BF16)
            ik_ref[slot_build, c, h] = jnp.concatenate([intras[n], ks[n].T], axis=0).astype(BF16)
            eg_ref[slot_build, c, h] = jnp.broadcast_to(egs[n], (8, GDN_HEAD_DIM))
        yield

    def scan_chunks(ci):
        for cc in range(CHUNKS_PER_STEP):
            i = ci * CHUNKS_PER_STEP + cc
            c = i + dirn * (nc - 1 - 2 * i)
            r0 = pl.multiple_of(c * CHUNK, CHUNK)
            states = [s_ref[h] for h in range(GDN_HEADS)]
            m1 = [_dot(wqg_ref[slot_scan, c, h], states[h].astype(BF16)) for h in range(GDN_HEADS)]
            yield
            v_new = [(u_ref[slot_scan, c, h] - m1[h][:CHUNK]).astype(BF16) for h in range(GDN_HEADS)]
            m2 = [_dot(ik_ref[slot_scan, c, h], v_new[h]) for h in range(GDN_HEADS)]
            for h in range(GDN_HEADS):
                hs = slice(h * GDN_HEAD_DIM, (h + 1) * GDN_HEAD_DIM)
                o_ref[0, 0, pl.ds(r0, CHUNK), hs] = m1[h][CHUNK:] + m2[h][:CHUNK]
                s_ref[h] = states[h] * eg_ref[slot_scan, c, h][0:1, :] + m2[h][CHUNK:]
            yield

    def run(*makers):
        def body(ci, carry):
            live = [m(ci) for m in makers]
            while live:
                live = [g for g in live if next(g, StopIteration) is not StopIteration]
            return carry
        lax.fori_loop(0, nc // CHUNKS_PER_STEP, body, 0)

    @pl.when(j == 0)
    def _():
        s_ref[...] = jnp.zeros(s_ref.shape, F32)
        run(chunk_matrices)

    @pl.when((j > 0) & (j < nblk))
    def _():
        run(chunk_matrices, scan_chunks)

    @pl.when(j == nblk)
    def _():
        run(scan_chunks)


def _gdn_scan(q, k, v, gcol, grow, *, ts):
    b, nh, s, hd = q.shape
    w = nh * hd
    nblk = s // ts
    nc = ts // CHUNK
    order = lambda d, j: j + d * (nblk - 1 - 2 * j)
    src = lambda d, j: order(d, jnp.minimum(j, nblk - 1))
    dst = lambda d, j: order(d, jnp.maximum(j - 1, 0))
    return pl.pallas_call(
        functools.partial(_gdn_scan_kernel, nc=nc, nblk=nblk),
        grid=(b, 2, nblk + 1),
        in_specs=[pl.BlockSpec((1, nh, ts, hd), lambda i, d, j: (i, 0, src(d, j), 0))] * 3
                 + [pl.BlockSpec((1, ts, LANES), lambda i, d, j: (i, src(d, j), 0)),
                    pl.BlockSpec((1, nc, 8, CHUNK), lambda i, d, j: (i, src(d, j), 0, 0))],
        out_specs=pl.BlockSpec((1, 1, ts, w), lambda i, d, j: (d, i, dst(d, j), 0)),
        out_shape=jax.ShapeDtypeStruct((2, b, s, w), F32),
        scratch_shapes=[pltpu.VMEM((GDN_HEADS, GDN_HEAD_DIM, GDN_HEAD_DIM), F32),
                        pltpu.VMEM((2, nc, GDN_HEADS, 2 * CHUNK, GDN_HEAD_DIM), BF16),
                        pltpu.VMEM((2, nc, GDN_HEADS, CHUNK, GDN_HEAD_DIM), F32),
                        pltpu.VMEM((2, nc, GDN_HEADS, CHUNK + GDN_HEAD_DIM, CHUNK), BF16),
                        pltpu.VMEM((2, nc, GDN_HEADS, 8, GDN_HEAD_DIM), F32)],
        compiler_params=_cparams(("arbitrary", "arbitrary", "arbitrary")),
        name="gdn_scan",
    )(q, k, v, gcol, grow)


SWA_TQ = 128
SWA_TK = SWA_TQ + 2 * BAND_RADIUS
SWA_HALO = BAND_RADIUS * max(DILATIONS)
NORM_ROWS = 512
SWA_TILES_PER_STEP = 4


def _swa_kernel(q_ref, k_ref, v_ref, qw_ref, kw_ref, bias_ref, o_ref,
                qn_ref, kn_ref, vp_ref, m0_ref, m1_ref, l_ref, acc_ref, bias_scr, *, seq, qb):
    qi = pl.program_id(2)
    lane = lax.broadcasted_iota(jnp.int32, (1, LANES), 1)
    low = lane < SWA_HEAD_DIM
    nt = (((1,), (1,)), ((), ()))

    @pl.when(qi == 0)
    def _():
        for pi in range(len(DILATIONS)):
            for hh in range(2):
                full = jnp.broadcast_to(bias_ref[0, pi, hh], (SWA_TQ, SWA_TK))
                bias_scr[pi, hh] = pltpu.roll(full, 0, 1, stride=1, stride_axis=0)

        zeros = jnp.zeros((SWA_HALO, LANES), F32)
        for ref in (kn_ref, vp_ref):
            ref[0:SWA_HALO, :] = zeros
            ref[SWA_HALO + seq:2 * SWA_HALO + seq, :] = zeros

        def head_rms(x, w):
            x2 = x * x
            s0 = jnp.sum(jnp.where(low, x2, 0.0), axis=-1, keepdims=True)
            s1 = jnp.sum(jnp.where(low, 0.0, x2), axis=-1, keepdims=True)
            ms = jnp.where(low, s0, s1) * (1.0 / SWA_HEAD_DIM)
            return x * lax.rsqrt(ms + EPS) * w

        def norm_rows(i, carry):
            r0 = pl.multiple_of(i * NORM_ROWS, NORM_ROWS)
            qn_ref[pl.ds(r0, NORM_ROWS), :] = (head_rms(q_ref[0, pl.ds(r0, NORM_ROWS), :], qw_ref[...])
                                               * (SWA_HEAD_DIM ** -0.5 * LOG2E))
            kn_ref[pl.ds(SWA_HALO + r0, NORM_ROWS), :] = head_rms(k_ref[0, pl.ds(r0, NORM_ROWS), :], kw_ref[...])
            vp_ref[pl.ds(SWA_HALO + r0, NORM_ROWS), :] = v_ref[0, pl.ds(r0, NORM_ROWS), :]
            return carry

        lax.fori_loop(0, seq // NORM_ROWS, norm_rows, 0)

    kcol = lax.broadcasted_iota(jnp.int32, (1, SWA_TK), 1)

    def rows(ref, start, size, stride):
        if stride == 1:
            return ref[pl.ds(start, size), :]
        return ref[pl.ds(start, size, stride=stride), :]

    def put(ref, start, size, stride, val):
        if stride == 1:
            ref[pl.ds(start, size), :] = val
        else:
            ref[pl.ds(start, size, stride=stride), :] = val

    for pi, dil in enumerate(DILATIONS):
        sub_len = seq // dil
        tiles_per_res = qb // dil // SWA_TQ
        shift = int(math.log2(tiles_per_res))
        first = pi == 0

        def tiles(it, carry, pi=pi, dil=dil, sub_len=sub_len, tiles_per_res=tiles_per_res, shift=shift, first=first):
            loc, vts, old, ss = [], [], [], []
            for u in range(SWA_TILES_PER_STEP):
                i = it * SWA_TILES_PER_STEP + u
                res = lax.shift_right_logical(i, shift)
                t = jnp.bitwise_and(i, tiles_per_res - 1)
                tau0 = qi * (qb // dil) + t * SWA_TQ
                loc0 = res + dil * (t * SWA_TQ)
                krow = SWA_HALO + qi * qb + loc0 - dil * BAND_RADIUS
                qt = rows(qn_ref, qi * qb + loc0, SWA_TQ, dil)
                kt = rows(kn_ref, krow, SWA_TK, dil).astype(BF16)
                vt = rows(vp_ref, krow, SWA_TK, dil)
                vts.append((jnp.where(low, vt, 1.0).astype(BF16), jnp.where(low, 1.0, vt).astype(BF16)))
                kidx = kcol + (tau0 - BAND_RADIUS)
                kvalid = (kidx >= 0) & (kidx < sub_len)
                loc.append(loc0)
                if not first:
                    old.append(tuple(rows(ref, loc0, SWA_TQ, dil) for ref in (m0_ref, m1_ref, l_ref, acc_ref)))
                for hh in range(2):
                    mine = low if hh == 0 else jnp.logical_not(low)
                    qh = jnp.where(mine, qt, 0.0).astype(BF16)
                    s = lax.dot_general(qh, kt, nt, preferred_element_type=F32)
                    ss.append(jnp.where(kvalid, s + bias_scr[pi, hh], NEG_BIG))
            m_new, alpha, ps = [], [], []
            for n, s in enumerate(ss):
                u, hh = divmod(n, 2)
                mt = jnp.max(s, axis=-1, keepdims=True)
                if first:
                    mn = jnp.broadcast_to(mt, (SWA_TQ, LANES))
                else:
                    mo = old[u][hh]
                    mn = jnp.maximum(mo, mt)
                    alpha.append(jnp.exp2(mo - mn))
                m_new.append(mn)
                ps.append(jnp.exp2(s - jnp.concatenate([mn, mn], axis=1)).astype(BF16))
            pv = [_dot(p, vts[n // 2][n % 2]) for n, p in enumerate(ps)]
            for u in range(SWA_TILES_PER_STEP):
                pv0, pv1 = pv[2 * u], pv[2 * u + 1]
                put(m0_ref, loc[u], SWA_TQ, dil, m_new[2 * u])
                put(m1_ref, loc[u], SWA_TQ, dil, m_new[2 * u + 1])
                lsum = jnp.where(low, pv1, pv0)
                pvv = jnp.where(low, pv0, pv1)
                if first:
                    put(l_ref, loc[u], SWA_TQ, dil, lsum)
                    put(acc_ref, loc[u], SWA_TQ, dil, pvv)
                else:
                    a0, a1 = alpha[2 * u], alpha[2 * u + 1]
                    put(l_ref, loc[u], SWA_TQ, dil, old[u][2] * jnp.where(low, a1, a0) + lsum)
                    put(acc_ref, loc[u], SWA_TQ, dil, old[u][3] * jnp.where(low, a0, a1) + pvv)
            return carry

        lax.fori_loop(0, qb // SWA_TQ // SWA_TILES_PER_STEP, tiles, 0)

    o_ref[0] = acc_ref[...] / pltpu.roll(l_ref[...], SWA_HEAD_DIM, 1)


def _t5_bucket(rel):
    nb = REL_BUCKETS // 2
    bucket = (rel > 0).astype(np.int32) * nb
    n = np.abs(rel)
    max_exact = nb // 2
    large = max_exact + (np.log(np.maximum(n, 1) / max_exact)
                         / math.log(REL_MAX_DISTANCE / max_exact) * (nb - max_exact)).astype(np.int32)
    large = np.minimum(large, nb - 1)
    return (bucket + np.where(n < max_exact, n, large)).astype(np.int32)


def _band_bias(rel_bias):
    rel = np.arange(2 * BAND_RADIUS + 1) - BAND_RADIUS
    rows = []
    for dil in DILATIONS:
        inside = jnp.transpose(rel_bias[_t5_bucket(rel * dil)]) * LOG2E
        rows.append(jnp.pad(inside, ((0, 0), (0, SWA_TK - inside.shape[1])), constant_values=NEG_BIG))
    bias = jnp.stack(rows, axis=0)
    bias = bias.reshape(len(DILATIONS), SWA_HEADS // 2, 2, 1, SWA_TK)
    return jnp.transpose(bias, (1, 0, 2, 3, 4)).astype(F32)


def _swa(qkvb, qw, kw, bias, *, qb):
    b, s, _ = qkvb.shape
    pairs = SWA_HEADS // 2
    assert s % qb == 0 and qb % (max(DILATIONS) * SWA_TQ) == 0
    col = lambda base: pl.BlockSpec((1, s, LANES), lambda i, p, j: (i, 0, base + p))
    padded = s + 2 * SWA_HALO
    return pl.pallas_call(
        functools.partial(_swa_kernel, seq=s, qb=qb),
        grid=(b, pairs, s // qb),
        in_specs=[col(0), col(pairs), col(2 * pairs), _resident((1, LANES)), _resident((1, LANES)),
                  pl.BlockSpec((1,) + bias.shape[1:], lambda i, p, j: (p, 0, 0, 0, 0))],
        out_specs=pl.BlockSpec((1, qb, LANES), lambda i, p, j: (i, j, p)),
        out_shape=jax.ShapeDtypeStruct((b, s, SWA_WIDTH), F32),
        scratch_shapes=[pltpu.VMEM((s, LANES), F32), pltpu.VMEM((padded, LANES), F32),
                        pltpu.VMEM((padded, LANES), F32)] + [pltpu.VMEM((qb, LANES), F32)] * 4
                       + [pltpu.VMEM((len(DILATIONS), 2, SWA_TQ, SWA_TK), F32)],
        compiler_params=_cparams(("arbitrary", "arbitrary", "arbitrary")),
        name="swa",
    )(qkvb, qkvb, qkvb, qw, kw, bias)


FFN_TM = 512
FFN_FCHUNK = 256
PREP_TS = 512
SCAN_TS = 1024
SWA_QB = 2048


def _lane_row(vals, copies_at):
    row = jnp.zeros((LANES,), F32)
    for off in copies_at:
        row = row.at[off:off + vals.shape[0]].set(vals)
    return row[None, :]


def kernel(x, ffn1_norm, ffn1_w_gate, ffn1_w_up, ffn1_w_down, mix_norm, w_in, conv_w, a_log, dt_bias, gdn_norm_w, q_norm_w, k_norm_w, rel_bias, w_out, ffn2_norm, ffn2_w_gate, ffn2_w_up, ffn2_w_down, final_norm):
    b, s, d = x.shape
    n = b * s
    x2d = x.reshape(n, d)
    ng = 2 * GDN_HEADS
    c_z = 3 * GDN_WIDTH
    c_a = c_z + GDN_WIDTH
    c_b = c_a + ng
    c_qkvb = c_b + ng
    bias = _band_bias(rel_bias)
    for l in range(ffn1_norm.shape[0]):
        wl = w_in[l]
        w_a = wl[:, c_a:c_b]
        w_gates = jnp.concatenate([w_a, wl[:, c_b:c_qkvb], w_a, w_a, jnp.zeros((d, LANES - 4 * ng), F32)], axis=1)
        win = jnp.concatenate([wl[:, :c_a], w_gates, wl[:, c_qkvb:]], axis=1).astype(BF16)
        copies = (0, 2 * ng, 3 * ng)
        alog = _lane_row(a_log[l].reshape(ng), copies)
        dtb = _lane_row(dt_bias[l].reshape(ng), copies)
        x1, qkva, z, ab, qkvb = _ffn1_proj(
            x2d, ffn1_norm[l][None, :], ffn1_w_gate[l].astype(BF16), ffn1_w_up[l].astype(BF16),
            ffn1_w_down[l].astype(BF16), mix_norm[l][None, :], win, tm=FFN_TM, fchunk=FFN_FCHUNK)
        q, k, v, gcol, grow = _gdn_prep(
            qkva.reshape(b, s, -1), ab.reshape(b, s, LANES), jnp.transpose(conv_w[l]), alog, dtb, ts=PREP_TS)
        o_dirs = _gdn_scan(q, k, v, gcol, grow, ts=SCAN_TS)
        attn = _swa(qkvb.reshape(b, s, -1), jnp.tile(q_norm_w[l], 2)[None, :], jnp.tile(k_norm_w[l], 2)[None, :],
                    bias, qb=SWA_QB)
        x2d = _out_ffn2(
            x1, o_dirs.reshape(2, n, -1), z, attn.reshape(n, -1),
            gdn_norm_w[l][None, :], w_out[l].astype(BF16), ffn2_norm[l][None, :],
            ffn2_w_gate[l].astype(BF16), ffn2_w_up[l].astype(BF16), ffn2_w_down[l].astype(BF16),
            final_norm[l][None, :], tm=FFN_TM, fchunk=FFN_FCHUNK)
    return x2d.reshape(b, s, d)
```

```python
import functools
import math

import numpy as np
import jax
import jax.numpy as jnp
from jax import lax
from jax.experimental import pallas as pl
from jax.experimental.pallas import tpu as pltpu

F32 = jnp.float32
BF16 = jnp.bfloat16
EPS = 1e-6
NEG_BIG = -1e30
LOG2E = math.log2(math.e)

LANES = 128
GDN_HEADS = 4
GDN_HEAD_DIM = 128
GDN_WIDTH = GDN_HEADS * GDN_HEAD_DIM
CONV_WIDTH = 5
CHUNK = 64
SWA_HEADS = 8
SWA_HEAD_DIM = 64
SWA_WIDTH = SWA_HEADS * SWA_HEAD_DIM
DILATIONS = (1, 4, 16)
BAND_RADIUS = 64
REL_BUCKETS = 32
REL_MAX_DISTANCE = 1024
VMEM_LIMIT = 56 * 1024 * 1024


def _cparams(sem):
    return pltpu.CompilerParams(dimension_semantics=sem, vmem_limit_bytes=VMEM_LIMIT)


def _resident(shape):
    zeros = (0,) * len(shape)
    return pl.BlockSpec(shape, lambda *_: zeros, pipeline_mode=pl.Buffered(1))


def _rms(x, w):
    return x * lax.rsqrt(jnp.mean(x * x, axis=-1, keepdims=True) + EPS) * w


def _silu(x):
    return x * (1.0 / (1.0 + jnp.exp(-x)))


def _dot(a, b):
    return jnp.dot(a, b, preferred_element_type=F32)


def _swiglu(h, wg_ref, wu_ref, wd_ref, fchunk):
    acc = None
    for c0 in range(0, wg_ref.shape[1], fchunk):
        g = _dot(h, wg_ref[:, c0:c0 + fchunk])
        u = _dot(h, wu_ref[:, c0:c0 + fchunk])
        a = (_silu(g) * u).astype(BF16)
        d = _dot(a, wd_ref[c0:c0 + fchunk, :])
        acc = d if acc is None else acc + d
    return acc


def _ffn1_proj_kernel(x_ref, n1_ref, wg_ref, wu_ref, wd_ref, nm_ref, win_ref,
                      x1_ref, qkva_ref, z_ref, ab_ref, qkvb_ref, *, fchunk):
    x = x_ref[...]
    h = _rms(x, n1_ref[...]).astype(BF16)
    x1 = x + 0.5 * _swiglu(h, wg_ref, wu_ref, wd_ref, fchunk)
    x1_ref[...] = x1
    h2 = _rms(x1, nm_ref[...]).astype(BF16)
    c = 0
    for ref in (qkva_ref, z_ref, ab_ref, qkvb_ref):
        n = ref.shape[1]
        ref[...] = _dot(h2, win_ref[:, c:c + n])
        c += n


def _ffn1_proj(x2d, n1, wg, wu, wd, nm, win, *, tm, fchunk):
    n, d = x2d.shape
    f = wg.shape[1]
    widths = (3 * GDN_WIDTH, GDN_WIDTH, LANES, 3 * SWA_WIDTH)
    assert win.shape[1] == sum(widths) and n % tm == 0
    row = lambda w: pl.BlockSpec((tm, w), lambda i: (i, 0))
    return pl.pallas_call(
        functools.partial(_ffn1_proj_kernel, fchunk=fchunk),
        grid=(n // tm,),
        in_specs=[row(d), _resident((1, d)), _resident((d, f)), _resident((d, f)), _resident((f, d)),
                  _resident((1, d)), _resident(win.shape)],
        out_specs=[row(d)] + [row(w) for w in widths],
        out_shape=[jax.ShapeDtypeStruct((n, d), F32)] + [jax.ShapeDtypeStruct((n, w), F32) for w in widths],
        compiler_params=_cparams(("arbitrary",)),
        name="ffn1_proj",
    )(x2d, n1, wg, wu, wd, nm, win)


def _out_ffn2_kernel(x1_ref, of_ref, ob_ref, z_ref, attn_ref, gnw_ref, wout_ref, n2_ref,
                     wg_ref, wu_ref, wd_ref, nf_ref, out_ref, *, fchunk):
    o = of_ref[0] + ob_ref[0]
    z = z_ref[...]
    gnw = gnw_ref[...]
    heads = []
    for h in range(GDN_HEADS):
        sl = slice(h * GDN_HEAD_DIM, (h + 1) * GDN_HEAD_DIM)
        heads.append(_rms(o[:, sl], gnw) * _silu(z[:, sl]))
    oa = jnp.concatenate(heads, axis=1).astype(BF16)
    mix = _dot(oa, wout_ref[:GDN_WIDTH, :]) + _dot(attn_ref[...].astype(BF16), wout_ref[GDN_WIDTH:, :])
    x2 = x1_ref[...] + mix
    h2 = _rms(x2, n2_ref[...]).astype(BF16)
    x3 = x2 + 0.5 * _swiglu(h2, wg_ref, wu_ref, wd_ref, fchunk)
    out_ref[...] = _rms(x3, nf_ref[...])


def _out_ffn2(x1, o_dirs, z, attn, gnw, wout, n2, wg, wu, wd, nf, *, tm, fchunk):
    n, d = x1.shape
    f = wg.shape[1]
    row = lambda w: pl.BlockSpec((tm, w), lambda i: (i, 0))
    direction = lambda k: pl.BlockSpec((1, tm, GDN_WIDTH), lambda i: (k, i, 0))
    return pl.pallas_call(
        functools.partial(_out_ffn2_kernel, fchunk=fchunk),
        grid=(n // tm,),
        in_specs=[row(d), direction(0), direction(1), row(GDN_WIDTH), row(SWA_WIDTH),
                  _resident((1, GDN_HEAD_DIM)), _resident(wout.shape), _resident((1, d)),
                  _resident((d, f)), _resident((d, f)), _resident((f, d)), _resident((1, d))],
        out_specs=row(d),
        out_shape=jax.ShapeDtypeStruct((n, d), F32),
        compiler_params=_cparams(("arbitrary",)),
        name="out_ffn2",
    )(x1, o_dirs, o_dirs, z, attn, gnw, wout, n2, wg, wu, wd, nf)


def _gdn_prep_kernel(cur_ref, prev_ref, next_ref, ab_ref, cw_ref, alog_ref, dtb_ref,
                     q_ref, k_ref, v_ref, gcol_ref, grow_ref, pad_ref, *, ts):
    j = pl.program_id(1)
    nj = pl.num_programs(1)
    half = ts // 2
    for cb in range(3 * GDN_HEADS):
        cs = slice(cb * LANES, (cb + 1) * LANES)
        pad_ref[cb, 0:8, :] = jnp.where(j > 0, prev_ref[0, :, cs], 0.0)
        pad_ref[cb, 8:8 + ts, :] = cur_ref[0, :, cs]
        pad_ref[cb, 8 + ts:16 + ts, :] = jnp.where(j < nj - 1, next_ref[0, :, cs], 0.0)
        dst = (q_ref, k_ref, v_ref)[cb // GDN_HEADS]
        for par in range(2):
            acc = None
            for t in range(CONV_WIDTH):
                term = pad_ref[cb, pl.ds(6 + t + par, half, stride=2), :] * cw_ref[t:t + 1, cs]
                acc = term if acc is None else acc + term
            y = _silu(acc)
            if cb < 2 * GDN_HEADS:
                y = y * lax.rsqrt(jnp.sum(y * y, axis=-1, keepdims=True) + EPS)
            if cb < GDN_HEADS:
                y = y * (GDN_HEAD_DIM ** -0.5)
            dst[0, cb % GDN_HEADS, pl.ds(par, half, stride=2), :] = y

    ab = ab_ref[0]
    xs = ab + dtb_ref[...]
    softplus = jnp.maximum(xs, 0.0) + jnp.log(1.0 + jnp.exp(-jnp.abs(xs)))
    g = -jnp.exp(alog_ref[...]) * softplus
    beta = 1.0 / (1.0 + jnp.exp(-ab))
    g1 = g.astype(BF16)
    r1 = g - g1.astype(F32)
    g2 = r1.astype(BF16)
    g3 = (r1 - g2.astype(F32)).astype(BF16)
    ri = lax.broadcasted_iota(jnp.int32, (LANES, LANES), 0)
    ci = lax.broadcasted_iota(jnp.int32, (LANES, LANES), 1)
    same = (ri // CHUNK) == (ci // CHUNK)
    lower = jnp.where(same & (ri >= ci), 1.0, 0.0).astype(BF16)
    upper = jnp.where(same & (ri <= ci), 1.0, 0.0).astype(BF16)
    lane = lax.broadcasted_iota(jnp.int32, (1, LANES), 1)
    for s in range(ts // LANES):
        rs = slice(s * LANES, (s + 1) * LANES)
        pre = _dot(lower, g1[rs]) + _dot(lower, g2[rs]) + _dot(lower, g3[rs])
        suf = _dot(upper, g1[rs]) + _dot(upper, g2[rs]) + _dot(upper, g3[rs])
        gs = g[rs]
        col = jnp.where(lane < 4, pre,
              jnp.where(lane < 8, suf,
              jnp.where(lane < 16, beta[rs],
              jnp.where(lane < 20, suf - gs,
              jnp.where(lane < 24, pre - gs, pre + suf - gs)))))
        gcol_ref[0, rs, :] = col
        colt = col.T
        grow_ref[0, 2 * s] = colt[0:8, 0:CHUNK]
        grow_ref[0, 2 * s + 1] = colt[0:8, CHUNK:2 * CHUNK]


def _gdn_prep(qkva, ab, cw, alog, dtb, *, ts):
    b, s, w = qkva.shape
    nblk = s // ts
    r8 = ts // 8
    last8 = s // 8 - 1
    f = lambda shape: jax.ShapeDtypeStruct(shape, F32)
    return pl.pallas_call(
        functools.partial(_gdn_prep_kernel, ts=ts),
        grid=(b, nblk),
        in_specs=[pl.BlockSpec((1, ts, w), lambda i, j: (i, j, 0)),
                  pl.BlockSpec((1, 8, w), lambda i, j: (i, jnp.maximum(j * r8 - 1, 0), 0)),
                  pl.BlockSpec((1, 8, w), lambda i, j: (i, jnp.minimum((j + 1) * r8, last8), 0)),
                  pl.BlockSpec((1, ts, LANES), lambda i, j: (i, j, 0)),
                  _resident(cw.shape), _resident((1, LANES)), _resident((1, LANES))],
        out_specs=[pl.BlockSpec((1, GDN_HEADS, ts, GDN_HEAD_DIM), lambda i, j: (i, 0, j, 0))] * 3
                  + [pl.BlockSpec((1, ts, LANES), lambda i, j: (i, j, 0)),
                     pl.BlockSpec((1, ts // CHUNK, 8, CHUNK), lambda i, j: (i, j, 0, 0))],
        out_shape=[f((b, GDN_HEADS, s, GDN_HEAD_DIM))] * 3 + [f((b, s, LANES)), f((b, s // CHUNK, 8, CHUNK))],
        scratch_shapes=[pltpu.VMEM((w // LANES, ts + 16, LANES), F32)],
        compiler_params=_cparams(("arbitrary", "arbitrary")),
        name="gdn_prep",
    )(qkva, qkva, qkva, ab, cw, alog, dtb)


CHUNKS_PER_STEP = 4

def _gdn_scan_kernel(q_ref, k_ref, v_ref, gcol_ref, grow_ref, o_ref,
                     s_ref, wqg_ref, u_ref, ik_ref, eg_ref, *, nc, nblk):
    dirn = pl.program_id(1)
    j = pl.program_id(2)
    fwd = dirn == 0
    slot_build = lax.rem(j, 2)
    slot_scan = 1 - slot_build

    ri = lax.broadcasted_iota(jnp.int32, (CHUNK, CHUNK), 0)
    ci = lax.broadcasted_iota(jnp.int32, (CHUNK, CHUNK), 1)
    dd = (ri - ci) * (1 - 2 * dirn)
    incl = dd >= 0
    strict = dd > 0
    eye = jnp.where(ri == ci, 1.0, 0.0)
    left = lax.broadcasted_iota(jnp.int32, (CHUNK, 2 * CHUNK), 1) < CHUNK
    nt = (((1,), (1,)), ((), ()))

    nsq = int(math.log2(CHUNK))

    def chunk_matrices(ci):
        probs = [(ci * CHUNKS_PER_STEP + cc, h) for cc in range(CHUNKS_PER_STEP) for h in range(GDN_HEADS)]
        qs, ks, egs, rs, ps, intras = [], [], [], [], [], []
        for c, h in probs:
            r0 = pl.multiple_of(c * CHUNK, CHUNK)
            gates = gcol_ref[0, pl.ds(r0, CHUNK), :]
            rows = grow_ref[0, c]

            def pick(base):
                return jnp.where(fwd, gates[:, base + h:base + h + 1], gates[:, base + 4 + h:base + 5 + h])
            gc, beta, gdec, gl = pick(0), pick(8), pick(16), pick(24)
            grow = jnp.where(fwd, rows[h:h + 1, :], rows[4 + h:5 + h, :])
            qc = q_ref[0, h, pl.ds(r0, CHUNK), :]
            kc = k_ref[0, h, pl.ds(r0, CHUNK), :]
            vc = v_ref[0, h, pl.ds(r0, CHUNK), :]
            dec = jnp.where(incl, jnp.exp(jnp.where(incl, gc - grow, 0.0)), 0.0)
            kb = kc * beta
            qk = lax.dot_general(jnp.concatenate([kb, qc], axis=0).astype(BF16), kc.astype(BF16), nt,
                                 preferred_element_type=F32)
            ps.append(-jnp.where(strict, qk[:CHUNK] * dec, 0.0))
            intras.append(qk[CHUNK:] * dec)
            eg = jnp.exp(gc)
            rs.append(jnp.concatenate([vc * beta, kb * eg], axis=1))
            qs.append(qc * eg)
            ks.append(kc * jnp.exp(gdec))
            egs.append(jnp.exp(gl[0:1, :]))
        yield
        xs = [jnp.concatenate([p, eye], axis=1) for p in ps]
        for t in range(nsq):
            for n in range(len(probs)):
                xb = xs[n].astype(BF16)
                y = _dot(xb[:, :CHUNK], xb)
                xs[n] = jnp.where(left, y, xs[n] + y)
            yield
        for n in range(len(probs)):
            t_off = (xs[n][:, CHUNK:] - eye).astype(BF16)
            rs[n] = rs[n] + _dot(t_off, rs[n].astype(BF16))
        for n, (c, h) in enumerate(probs):
            u_ref[slot_build, c, h] = rs[n][:, :GDN_HEAD_DIM]
            wqg_ref[slot_build, c, h] = jnp.concatenate([rs[n][:, GDN_HEAD_DIM:], qs[n]], axis=0).astype(BF16)
            ik_ref[slot_build, c, h] = jnp.concatenate([intras[n], ks[n].T], axis=0).astype(BF16)
            eg_ref[slot_build, c, h] = jnp.broadcast_to(egs[n], (8, GDN_HEAD_DIM))
        yield

    def scan_chunks(ci):
        for cc in range(CHUNKS_PER_STEP):
            i = ci * CHUNKS_PER_STEP + cc
            c = i + dirn * (nc - 1 - 2 * i)
            r0 = pl.multiple_of(c * CHUNK, CHUNK)
            states = [s_ref[h] for h in range(GDN_HEADS)]
            m1 = [_dot(wqg_ref[slot_scan, c, h], states[h].astype(BF16)) for h in range(GDN_HEADS)]
            yield
            v_new = [(u_ref[slot_scan, c, h] - m1[h][:CHUNK]).astype(BF16) for h in range(GDN_HEADS)]
            m2 = [_dot(ik_ref[slot_scan, c, h], v_new[h]) for h in range(GDN_HEADS)]
            for h in range(GDN_HEADS):
                hs = slice(h * GDN_HEAD_DIM, (h + 1) * GDN_HEAD_DIM)
                o_ref[0, 0, pl.ds(r0, CHUNK), hs] = m1[h][CHUNK:] + m2[h][:CHUNK]
                s_ref[h] = states[h] * eg_ref[slot_scan, c, h][0:1, :] + m2[h][CHUNK:]
            yield

    def run(*makers):
        def body(ci, carry):
            live = [m(ci) for m in makers]
            while live:
                live = [g for g in live if next(g, StopIteration) is not StopIteration]
            return carry
        lax.fori_loop(0, nc // CHUNKS_PER_STEP, body, 0)

    @pl.when(j == 0)
    def _():
        s_ref[...] = jnp.zeros(s_ref.shape, F32)
        run(chunk_matrices)

    @pl.when((j > 0) & (j < nblk))
    def _():
        run(chunk_matrices, scan_chunks)

    @pl.when(j == nblk)
    def _():
        run(scan_chunks)


def _gdn_scan(q, k, v, gcol, grow, *, ts):
    b, nh, s, hd = q.shape
    w = nh * hd
    nblk = s // ts
    nc = ts // CHUNK
    order = lambda d, j: j + d * (nblk - 1 - 2 * j)
    src = lambda d, j: order(d, jnp.minimum(j, nblk - 1))
    dst = lambda d, j: order(d, jnp.maximum(j - 1, 0))
    return pl.pallas_call(
        functools.partial(_gdn_scan_kernel, nc=nc, nblk=nblk),
        grid=(b, 2, nblk + 1),
        in_specs=[pl.BlockSpec((1, nh, ts, hd), lambda i, d, j: (i, 0, src(d, j), 0))] * 3
                 + [pl.BlockSpec((1, ts, LANES), lambda i, d, j: (i, src(d, j), 0)),
                    pl.BlockSpec((1, nc, 8, CHUNK), lambda i, d, j: (i, src(d, j), 0, 0))],
        out_specs=pl.BlockSpec((1, 1, ts, w), lambda i, d, j: (d, i, dst(d, j), 0)),
        out_shape=jax.ShapeDtypeStruct((2, b, s, w), F32),
        scratch_shapes=[pltpu.VMEM((GDN_HEADS, GDN_HEAD_DIM, GDN_HEAD_DIM), F32),
                        pltpu.VMEM((2, nc, GDN_HEADS, 2 * CHUNK, GDN_HEAD_DIM), BF16),
                        pltpu.VMEM((2, nc, GDN_HEADS, CHUNK, GDN_HEAD_DIM), F32),
                        pltpu.VMEM((2, nc, GDN_HEADS, CHUNK + GDN_HEAD_DIM, CHUNK), BF16),
                        pltpu.VMEM((2, nc, GDN_HEADS, 8, GDN_HEAD_DIM), F32)],
        compiler_params=_cparams(("arbitrary", "arbitrary", "arbitrary")),
        name="gdn_scan",
    )(q, k, v, gcol, grow)


SWA_TQ = 128
SWA_TK = SWA_TQ + 2 * BAND_RADIUS
SWA_HALO = BAND_RADIUS * max(DILATIONS)
NORM_ROWS = 512
SWA_TILES_PER_STEP = 4


def _swa_kernel(q_ref, k_ref, v_ref, qw_ref, kw_ref, bias_ref, o_ref,
                qn_ref, kn_ref, vp_ref, m0_ref, m1_ref, l_ref, acc_ref, bias_scr, *, seq, qb):
    qi = pl.program_id(2)
    lane = lax.broadcasted_iota(jnp.int32, (1, LANES), 1)
    low = lane < SWA_HEAD_DIM
    nt = (((1,), (1,)), ((), ()))

    @pl.when(qi == 0)
    def _():
        for pi in range(len(DILATIONS)):
            for hh in range(2):
                full = jnp.broadcast_to(bias_ref[0, pi, hh], (SWA_TQ, SWA_TK))
                bias_scr[pi, hh] = pltpu.roll(full, 0, 1, stride=1, stride_axis=0)

        zeros = jnp.zeros((SWA_HALO, LANES), F32)
        for ref in (kn_ref, vp_ref):
            ref[0:SWA_HALO, :] = zeros
            ref[SWA_HALO + seq:2 * SWA_HALO + seq, :] = zeros

        def head_rms(x, w):
            x2 = x * x
            s0 = jnp.sum(jnp.where(low, x2, 0.0), axis=-1, keepdims=True)
            s1 = jnp.sum(jnp.where(low, 0.0, x2), axis=-1, keepdims=True)
            ms = jnp.where(low, s0, s1) * (1.0 / SWA_HEAD_DIM)
            return x * lax.rsqrt(ms + EPS) * w

        def norm_rows(i, carry):
            r0 = pl.multiple_of(i * NORM_ROWS, NORM_ROWS)
            qn_ref[pl.ds(r0, NORM_ROWS), :] = (head_rms(q_ref[0, pl.ds(r0, NORM_ROWS), :], qw_ref[...])
                                               * (SWA_HEAD_DIM ** -0.5 * LOG2E))
            kn_ref[pl.ds(SWA_HALO + r0, NORM_ROWS), :] = head_rms(k_ref[0, pl.ds(r0, NORM_ROWS), :], kw_ref[...])
            vp_ref[pl.ds(SWA_HALO + r0, NORM_ROWS), :] = v_ref[0, pl.ds(r0, NORM_ROWS), :]
            return carry

        lax.fori_loop(0, seq // NORM_ROWS, norm_rows, 0)

    kcol = lax.broadcasted_iota(jnp.int32, (1, SWA_TK), 1)

    def rows(ref, start, size, stride):
        if stride == 1:
            return ref[pl.ds(start, size), :]
        return ref[pl.ds(start, size, stride=stride), :]

    def put(ref, start, size, stride, val):
        if stride == 1:
            ref[pl.ds(start, size), :] = val
        else:
            ref[pl.ds(start, size, stride=stride), :] = val

    def logits(pi, dil, g):
        sub_len = seq // dil
        tiles_per_res = qb // dil // SWA_TQ
        ss = []
        for u in range(SWA_TILES_PER_STEP):
            res, t = divmod(g * SWA_TILES_PER_STEP + u, tiles_per_res)
            tau0 = qi * (qb // dil) + t * SWA_TQ
            loc0 = res + dil * (t * SWA_TQ)
            krow = SWA_HALO + qi * qb + loc0 - dil * BAND_RADIUS
            qt = rows(qn_ref, qi * qb + loc0, SWA_TQ, dil)
            kt = rows(kn_ref, krow, SWA_TK, dil).astype(BF16)
            kidx = kcol + (tau0 - BAND_RADIUS)
            kvalid = (kidx >= 0) & (kidx < sub_len)
            for hh in range(2):
                mine = low if hh == 0 else jnp.logical_not(low)
                qh = jnp.where(mine, qt, 0.0).astype(BF16)
                s = lax.dot_general(qh, kt, nt, preferred_element_type=F32)
                ss.append(jnp.where(kvalid, s + bias_scr[pi, hh], NEG_BIG))
        return ss

    def accumulate(dil, g, ss, first):
        tiles_per_res = qb // dil // SWA_TQ
        loc, vts, old = [], [], []
        for u in range(SWA_TILES_PER_STEP):
            res, t = divmod(g * SWA_TILES_PER_STEP + u, tiles_per_res)
            loc0 = res + dil * (t * SWA_TQ)
            vt = rows(vp_ref, SWA_HALO + qi * qb + loc0 - dil * BAND_RADIUS, SWA_TK, dil)
            vts.append((jnp.where(low, vt, 1.0).astype(BF16), jnp.where(low, 1.0, vt).astype(BF16)))
            loc.append(loc0)
            if not first:
                old.append(tuple(rows(ref, loc0, SWA_TQ, dil) for ref in (m0_ref, m1_ref, l_ref, acc_ref)))
        m_new, alpha, ps = [], [], []
        for n, s in enumerate(ss):
            u, hh = divmod(n, 2)
            mt = jnp.max(s, axis=-1, keepdims=True)
            if first:
                mn = jnp.broadcast_to(mt, (SWA_TQ, LANES))
            else:
                mo = old[u][hh]
                mn = jnp.maximum(mo, mt)
                alpha.append(jnp.exp2(mo - mn))
            m_new.append(mn)
            ps.append(jnp.exp2(s - jnp.concatenate([mn, mn], axis=1)).astype(BF16))
        pv = [_dot(p, vts[n // 2][n % 2]) for n, p in enumerate(ps)]
        for u in range(SWA_TILES_PER_STEP):
            pv0, pv1 = pv[2 * u], pv[2 * u + 1]
            put(m0_ref, loc[u], SWA_TQ, dil, m_new[2 * u])
            put(m1_ref, loc[u], SWA_TQ, dil, m_new[2 * u + 1])
            lsum = jnp.where(low, pv1, pv0)
            pvv = jnp.where(low, pv0, pv1)
            if first:
                put(l_ref, loc[u], SWA_TQ, dil, lsum)
                put(acc_ref, loc[u], SWA_TQ, dil, pvv)
            else:
                a0, a1 = alpha[2 * u], alpha[2 * u + 1]
                put(l_ref, loc[u], SWA_TQ, dil, old[u][2] * jnp.where(low, a1, a0) + lsum)
                put(acc_ref, loc[u], SWA_TQ, dil, old[u][3] * jnp.where(low, a0, a1) + pvv)

    units = [(pi, dil, g) for pi, dil in enumerate(DILATIONS) for g in range(qb // SWA_TQ // SWA_TILES_PER_STEP)]
    pending = logits(*units[0])
    for k, (pi, dil, g) in enumerate(units):
        nxt = logits(*units[k + 1]) if k + 1 < len(units) else None
        accumulate(dil, g, pending, first=pi == 0)
        pending = nxt

    o_ref[0] = acc_ref[...] / pltpu.roll(l_ref[...], SWA_HEAD_DIM, 1)


def _t5_bucket(rel):
    nb = REL_BUCKETS // 2
    bucket = (rel > 0).astype(np.int32) * nb
    n = np.abs(rel)
    max_exact = nb // 2
    large = max_exact + (np.log(np.maximum(n, 1) / max_exact)
                         / math.log(REL_MAX_DISTANCE / max_exact) * (nb - max_exact)).astype(np.int32)
    large = np.minimum(large, nb - 1)
    return (bucket + np.where(n < max_exact, n, large)).astype(np.int32)


def _band_bias(rel_bias):
    rel = np.arange(2 * BAND_RADIUS + 1) - BAND_RADIUS
    rows = []
    for dil in DILATIONS:
        inside = jnp.transpose(rel_bias[_t5_bucket(rel * dil)]) * LOG2E
        rows.append(jnp.pad(inside, ((0, 0), (0, SWA_TK - inside.shape[1])), constant_values=NEG_BIG))
    bias = jnp.stack(rows, axis=0)
    bias = bias.reshape(len(DILATIONS), SWA_HEADS // 2, 2, 1, SWA_TK)
    return jnp.transpose(bias, (1, 0, 2, 3, 4)).astype(F32)


def _swa(qkvb, qw, kw, bias, *, qb):
    b, s, _ = qkvb.shape
    pairs = SWA_HEADS // 2
    assert s % qb == 0 and qb % (max(DILATIONS) * SWA_TQ) == 0
    col = lambda base: pl.BlockSpec((1, s, LANES), lambda i, p, j: (i, 0, base + p))
    padded = s + 2 * SWA_HALO
    return pl.pallas_call(
        functools.partial(_swa_kernel, seq=s, qb=qb),
        grid=(b, pairs, s // qb),
        in_specs=[col(0), col(pairs), col(2 * pairs), _resident((1, LANES)), _resident((1, LANES)),
                  pl.BlockSpec((1,) + bias.shape[1:], lambda i, p, j: (p, 0, 0, 0, 0))],
        out_specs=pl.BlockSpec((1, qb, LANES), lambda i, p, j: (i, j, p)),
        out_shape=jax.ShapeDtypeStruct((b, s, SWA_WIDTH), F32),
        scratch_shapes=[pltpu.VMEM((s, LANES), F32), pltpu.VMEM((padded, LANES), F32),
                        pltpu.VMEM((padded, LANES), F32)] + [pltpu.VMEM((qb, LANES), F32)] * 4
                       + [pltpu.VMEM((len(DILATIONS), 2, SWA_TQ, SWA_TK), F32)],
        compiler_params=_cparams(("arbitrary", "arbitrary", "arbitrary")),
        name="swa",
    )(qkvb, qkvb, qkvb, qw, kw, bias)


FFN_TM = 512
FFN_FCHUNK = 256
PREP_TS = 512
SCAN_TS = 1024
SWA_QB = 2048


def _lane_row(vals, copies_at):
    row = jnp.zeros((LANES,), F32)
    for off in copies_at:
        row = row.at[off:off + vals.shape[0]].set(vals)
    return row[None, :]


def kernel(x, ffn1_norm, ffn1_w_gate, ffn1_w_up, ffn1_w_down, mix_norm, w_in, conv_w, a_log, dt_bias, gdn_norm_w, q_norm_w, k_norm_w, rel_bias, w_out, ffn2_norm, ffn2_w_gate, ffn2_w_up, ffn2_w_down, final_norm):
    b, s, d = x.shape
    n = b * s
    x2d = x.reshape(n, d)
    ng = 2 * GDN_HEADS
    c_z = 3 * GDN_WIDTH
    c_a = c_z + GDN_WIDTH
    c_b = c_a + ng
    c_qkvb = c_b + ng
    bias = _band_bias(rel_bias)
    for l in range(ffn1_norm.shape[0]):
        wl = w_in[l]
        w_a = wl[:, c_a:c_b]
        w_gates = jnp.concatenate([w_a, wl[:, c_b:c_qkvb], w_a, w_a, jnp.zeros((d, LANES - 4 * ng), F32)], axis=1)
        win = jnp.concatenate([wl[:, :c_a], w_gates, wl[:, c_qkvb:]], axis=1).astype(BF16)
        copies = (0, 2 * ng, 3 * ng)
        alog = _lane_row(a_log[l].reshape(ng), copies)
        dtb = _lane_row(dt_bias[l].reshape(ng), copies)
        x1, qkva, z, ab, qkvb = _ffn1_proj(
            x2d, ffn1_norm[l][None, :], ffn1_w_gate[l].astype(BF16), ffn1_w_up[l].astype(BF16),
            ffn1_w_down[l].astype(BF16), mix_norm[l][None, :], win, tm=FFN_TM, fchunk=FFN_FCHUNK)
        q, k, v, gcol, grow = _gdn_prep(
            qkva.reshape(b, s, -1), ab.reshape(b, s, LANES), jnp.transpose(conv_w[l]), alog, dtb, ts=PREP_TS)
        o_dirs = _gdn_scan(q, k, v, gcol, grow, ts=SCAN_TS)
        attn = _swa(qkvb.reshape(b, s, -1), jnp.tile(q_norm_w[l], 2)[None, :], jnp.tile(k_norm_w[l], 2)[None, :],
                    bias, qb=SWA_QB)
        x2d = _out_ffn2(
            x1, o_dirs.reshape(2, n, -1), z, attn.reshape(n, -1),
            gdn_norm_w[l][None, :], w_out[l].astype(BF16), ffn2_norm[l][None, :],
            ffn2_w_gate[l].astype(BF16), ffn2_w_up[l].astype(BF16), ffn2_w_down[l].astype(BF16),
            final_norm[l][None, :], tm=FFN_TM, fchunk=FFN_FCHUNK)
    return x2d.reshape(b, s, d)
```

```python
import functools
import math

import numpy as np
import jax
import jax.numpy as jnp
from jax import lax
from jax.experimental import pallas as pl
from jax.experimental.pallas import tpu as pltpu

F32 = jnp.float32
BF16 = jnp.bfloat16
EPS = 1e-6
NEG_BIG = -1e30
LOG2E = math.log2(math.e)

LANES = 128
GDN_HEADS = 4
GDN_HEAD_DIM = 128
GDN_WIDTH = GDN_HEADS * GDN_HEAD_DIM
CONV_WIDTH = 5
CHUNK = 64
SWA_HEADS = 8
SWA_HEAD_DIM = 64
SWA_WIDTH = SWA_HEADS * SWA_HEAD_DIM
DILATIONS = (1, 4, 16)
BAND_RADIUS = 64
REL_BUCKETS = 32
REL_MAX_DISTANCE = 1024
VMEM_LIMIT = 56 * 1024 * 1024


def _cparams(sem):
    return pltpu.CompilerParams(dimension_semantics=sem, vmem_limit_bytes=VMEM_LIMIT)


def _resident(shape):
    zeros = (0,) * len(shape)
    return pl.BlockSpec(shape, lambda *_: zeros, pipeline_mode=pl.Buffered(1))


def _rms(x, w):
    return x * lax.rsqrt(jnp.mean(x * x, axis=-1, keepdims=True) + EPS) * w


def _silu(x):
    return x * (1.0 / (1.0 + jnp.exp(-x)))


def _dot(a, b):
    return jnp.dot(a, b, preferred_element_type=F32)


def _swiglu(h, wg_ref, wu_ref, wd_ref, fchunk):
    acc = None
    for c0 in range(0, wg_ref.shape[1], fchunk):
        g = _dot(h, wg_ref[:, c0:c0 + fchunk])
        u = _dot(h, wu_ref[:, c0:c0 + fchunk])
        a = (_silu(g) * u).astype(BF16)
        d = _dot(a, wd_ref[c0:c0 + fchunk, :])
        acc = d if acc is None else acc + d
    return acc


def _ffn1_proj_kernel(x_ref, n1_ref, wg_ref, wu_ref, wd_ref, nm_ref, win_ref, qw_ref, kw_ref,
                      x1_ref, qkva_ref, z_ref, ab_ref, qkvb_ref, *, fchunk):
    x = x_ref[...]
    h = _rms(x, n1_ref[...]).astype(BF16)
    x1 = x + 0.5 * _swiglu(h, wg_ref, wu_ref, wd_ref, fchunk)
    x1_ref[...] = x1
    h2 = _rms(x1, nm_ref[...]).astype(BF16)
    c = 0
    for ref in (qkva_ref, z_ref, ab_ref):
        n = ref.shape[1]
        ref[...] = _dot(h2, win_ref[:, c:c + n])
        c += n

    low = lax.broadcasted_iota(jnp.int32, (1, LANES), 1) < SWA_HEAD_DIM

    def head_rms(y, w):
        y2 = y * y
        s0 = jnp.sum(jnp.where(low, y2, 0.0), axis=-1, keepdims=True)
        s1 = jnp.sum(jnp.where(low, 0.0, y2), axis=-1, keepdims=True)
        ms = jnp.where(low, s0, s1) * (1.0 / SWA_HEAD_DIM)
        return y * lax.rsqrt(ms + EPS) * w

    qscale = qw_ref[...] * (SWA_HEAD_DIM ** -0.5 * LOG2E)
    wide = 2 * LANES
    for j in range(3 * SWA_WIDTH // wide):
        y = _dot(h2, win_ref[:, c + j * wide:c + (j + 1) * wide])
        for half in range(2):
            p = 2 * j + half
            yp = y[:, half * LANES:(half + 1) * LANES]
            if p < SWA_WIDTH // LANES:
                yp = head_rms(yp, qscale)
            elif p < 2 * SWA_WIDTH // LANES:
                yp = head_rms(yp, kw_ref[...])
            qkvb_ref[:, p * LANES:(p + 1) * LANES] = yp


def _ffn1_proj(x2d, n1, wg, wu, wd, nm, win, qw, kw, *, tm, fchunk):
    n, d = x2d.shape
    f = wg.shape[1]
    widths = (3 * GDN_WIDTH, GDN_WIDTH, LANES, 3 * SWA_WIDTH)
    assert win.shape[1] == sum(widths) and n % tm == 0
    row = lambda w: pl.BlockSpec((tm, w), lambda i: (i, 0))
    return pl.pallas_call(
        functools.partial(_ffn1_proj_kernel, fchunk=fchunk),
        grid=(n // tm,),
        in_specs=[row(d), _resident((1, d)), _resident((d, f)), _resident((d, f)), _resident((f, d)),
                  _resident((1, d)), _resident(win.shape), _resident((1, LANES)), _resident((1, LANES))],
        out_specs=[row(d)] + [row(w) for w in widths],
        out_shape=[jax.ShapeDtypeStruct((n, d), F32)] + [jax.ShapeDtypeStruct((n, w), F32) for w in widths],
        compiler_params=_cparams(("arbitrary",)),
        name="ffn1_proj",
    )(x2d, n1, wg, wu, wd, nm, win, qw, kw)


def _out_ffn2_kernel(x1_ref, of_ref, ob_ref, z_ref, attn_ref, gnw_ref, wout_ref, n2_ref,
                     wg_ref, wu_ref, wd_ref, nf_ref, out_ref, *, fchunk):
    o = of_ref[0] + ob_ref[0]
    z = z_ref[...]
    gnw = gnw_ref[...]
    heads = []
    for h in range(GDN_HEADS):
        sl = slice(h * GDN_HEAD_DIM, (h + 1) * GDN_HEAD_DIM)
        heads.append(_rms(o[:, sl], gnw) * _silu(z[:, sl]))
    oa = jnp.concatenate(heads, axis=1).astype(BF16)
    mix = _dot(oa, wout_ref[:GDN_WIDTH, :]) + _dot(attn_ref[...].astype(BF16), wout_ref[GDN_WIDTH:, :])
    x2 = x1_ref[...] + mix
    h2 = _rms(x2, n2_ref[...]).astype(BF16)
    x3 = x2 + 0.5 * _swiglu(h2, wg_ref, wu_ref, wd_ref, fchunk)
    out_ref[...] = _rms(x3, nf_ref[...])


def _out_ffn2(x1, o_dirs, z, attn, gnw, wout, n2, wg, wu, wd, nf, *, tm, fchunk):
    n, d = x1.shape
    f = wg.shape[1]
    row = lambda w: pl.BlockSpec((tm, w), lambda i: (i, 0))
    direction = lambda k: pl.BlockSpec((1, tm, GDN_WIDTH), lambda i: (k, i, 0))
    return pl.pallas_call(
        functools.partial(_out_ffn2_kernel, fchunk=fchunk),
        grid=(n // tm,),
        in_specs=[row(d), direction(0), direction(1), row(GDN_WIDTH), row(SWA_WIDTH),
                  _resident((1, GDN_HEAD_DIM)), _resident(wout.shape), _resident((1, d)),
                  _resident((d, f)), _resident((d, f)), _resident((f, d)), _resident((1, d))],
        out_specs=row(d),
        out_shape=jax.ShapeDtypeStruct((n, d), F32),
        compiler_params=_cparams(("arbitrary",)),
        name="out_ffn2",
    )(x1, o_dirs, o_dirs, z, attn, gnw, wout, n2, wg, wu, wd, nf)


def _gdn_prep_kernel(cur_ref, prev_ref, next_ref, ab_ref, cw_ref, alog_ref, dtb_ref,
                     q_ref, k_ref, v_ref, gcol_ref, grow_ref, pad_ref, *, ts):
    j = pl.program_id(1)
    nj = pl.num_programs(1)
    half = ts // 2
    for cb in range(3 * GDN_HEADS):
        cs = slice(cb * LANES, (cb + 1) * LANES)
        pad_ref[cb, 0:8, :] = jnp.where(j > 0, prev_ref[0, :, cs], 0.0)
        pad_ref[cb, 8:8 + ts, :] = cur_ref[0, :, cs]
        pad_ref[cb, 8 + ts:16 + ts, :] = jnp.where(j < nj - 1, next_ref[0, :, cs], 0.0)
        dst = (q_ref, k_ref, v_ref)[cb // GDN_HEADS]
        for par in range(2):
            acc = None
            for t in range(CONV_WIDTH):
                term = pad_ref[cb, pl.ds(6 + t + par, half, stride=2), :] * cw_ref[t:t + 1, cs]
                acc = term if acc is None else acc + term
            y = _silu(acc)
            if cb < 2 * GDN_HEADS:
                y = y * lax.rsqrt(jnp.sum(y * y, axis=-1, keepdims=True) + EPS)
            if cb < GDN_HEADS:
                y = y * (GDN_HEAD_DIM ** -0.5)
            dst[0, cb % GDN_HEADS, pl.ds(par, half, stride=2), :] = y

    ab = ab_ref[0]
    xs = ab + dtb_ref[...]
    softplus = jnp.maximum(xs, 0.0) + jnp.log(1.0 + jnp.exp(-jnp.abs(xs)))
    g = -jnp.exp(alog_ref[...]) * softplus
    beta = 1.0 / (1.0 + jnp.exp(-ab))
    g1 = g.astype(BF16)
    r1 = g - g1.astype(F32)
    g2 = r1.astype(BF16)
    g3 = (r1 - g2.astype(F32)).astype(BF16)
    ri = lax.broadcasted_iota(jnp.int32, (LANES, LANES), 0)
    ci = lax.broadcasted_iota(jnp.int32, (LANES, LANES), 1)
    same = (ri // CHUNK) == (ci // CHUNK)
    lower = jnp.where(same & (ri >= ci), 1.0, 0.0).astype(BF16)
    upper = jnp.where(same & (ri <= ci), 1.0, 0.0).astype(BF16)
    lane = lax.broadcasted_iota(jnp.int32, (1, LANES), 1)
    for s in range(ts // LANES):
        rs = slice(s * LANES, (s + 1) * LANES)
        pre = _dot(lower, g1[rs]) + _dot(lower, g2[rs]) + _dot(lower, g3[rs])
        suf = _dot(upper, g1[rs]) + _dot(upper, g2[rs]) + _dot(upper, g3[rs])
        gs = g[rs]
        col = jnp.where(lane < 4, pre,
              jnp.where(lane < 8, suf,
              jnp.where(lane < 16, beta[rs],
              jnp.where(lane < 20, suf - gs,
              jnp.where(lane < 24, pre - gs, pre + suf - gs)))))
        gcol_ref[0, rs, :] = col
        colt = col.T
        grow_ref[0, 2 * s] = colt[0:8, 0:CHUNK]
        grow_ref[0, 2 * s + 1] = colt[0:8, CHUNK:2 * CHUNK]


def _gdn_prep(qkva, ab, cw, alog, dtb, *, ts):
    b, s, w = qkva.shape
    nblk = s // ts
    r8 = ts // 8
    last8 = s // 8 - 1
    f = lambda shape: jax.ShapeDtypeStruct(shape, F32)
    return pl.pallas_call(
        functools.partial(_gdn_prep_kernel, ts=ts),
        grid=(b, nblk),
        in_specs=[pl.BlockSpec((1, ts, w), lambda i, j: (i, j, 0)),
                  pl.BlockSpec((1, 8, w), lambda i, j: (i, jnp.maximum(j * r8 - 1, 0), 0)),
                  pl.BlockSpec((1, 8, w), lambda i, j: (i, jnp.minimum((j + 1) * r8, last8), 0)),
                  pl.BlockSpec((1, ts, LANES), lambda i, j: (i, j, 0)),
                  _resident(cw.shape), _resident((1, LANES)), _resident((1, LANES))],
        out_specs=[pl.BlockSpec((1, GDN_HEADS, ts, GDN_HEAD_DIM), lambda i, j: (i, 0, j, 0))] * 3
                  + [pl.BlockSpec((1, ts, LANES), lambda i, j: (i, j, 0)),
                     pl.BlockSpec((1, ts // CHUNK, 8, CHUNK), lambda i, j: (i, j, 0, 0))],
        out_shape=[f((b, GDN_HEADS, s, GDN_HEAD_DIM))] * 3 + [f((b, s, LANES)), f((b, s // CHUNK, 8, CHUNK))],
        scratch_shapes=[pltpu.VMEM((w // LANES, ts + 16, LANES), F32)],
        compiler_params=_cparams(("arbitrary", "arbitrary")),
        name="gdn_prep",
    )(qkva, qkva, qkva, ab, cw, alog, dtb)


CHUNKS_PER_STEP = 4

def _gdn_scan_kernel(q_ref, k_ref, v_ref, gcol_ref, grow_ref, o_ref,
                     s_ref, wqg_ref, u_ref, ik_ref, eg_ref, *, nc, nblk):
    dirn = pl.program_id(1)
    j = pl.program_id(2)
    fwd = dirn == 0
    slot_build = lax.rem(j, 2)
    slot_scan = 1 - slot_build

    ri = lax.broadcasted_iota(jnp.int32, (CHUNK, CHUNK), 0)
    ci = lax.broadcasted_iota(jnp.int32, (CHUNK, CHUNK), 1)
    dd = (ri - ci) * (1 - 2 * dirn)
    incl = dd >= 0
    strict = dd > 0
    eye = jnp.where(ri == ci, 1.0, 0.0)
    left = lax.broadcasted_iota(jnp.int32, (CHUNK, 2 * CHUNK), 1) < CHUNK
    nt = (((1,), (1,)), ((), ()))

    nsq = int(math.log2(CHUNK))

    def chunk_matrices(ci):
        probs = [(ci * CHUNKS_PER_STEP + cc, h) for cc in range(CHUNKS_PER_STEP) for h in range(GDN_HEADS)]
        qs, ks, egs, rs, ps, intras = [], [], [], [], [], []
        for c, h in probs:
            r0 = pl.multiple_of(c * CHUNK, CHUNK)
            gates = gcol_ref[0, pl.ds(r0, CHUNK), :]
            rows = grow_ref[0, c]

            def pick(base):
                return jnp.where(fwd, gates[:, base + h:base + h + 1], gates[:, base + 4 + h:base + 5 + h])
            gc, beta, gdec, gl = pick(0), pick(8), pick(16), pick(24)
            grow = jnp.where(fwd, rows[h:h + 1, :], rows[4 + h:5 + h, :])
            qc = q_ref[0, h, pl.ds(r0, CHUNK), :]
            kc = k_ref[0, h, pl.ds(r0, CHUNK), :]
            vc = v_ref[0, h, pl.ds(r0, CHUNK), :]
            dec = jnp.where(incl, jnp.exp(jnp.where(incl, gc - grow, 0.0)), 0.0)
            kb = kc * beta
            qk = lax.dot_general(jnp.concatenate([kb, qc], axis=0).astype(BF16), kc.astype(BF16), nt,
                                 preferred_element_type=F32)
            ps.append(-jnp.where(strict, qk[:CHUNK] * dec, 0.0))
            intras.append(qk[CHUNK:] * dec)
            eg = jnp.exp(gc)
            rs.append(jnp.concatenate([vc * beta, kb * eg], axis=1))
            qs.append(qc * eg)
            ks.append(kc * jnp.exp(gdec))
            egs.append(jnp.exp(gl[0:1, :]))
        yield
        xs = [jnp.concatenate([p, eye], axis=1) for p in ps]
        for t in range(nsq):
            for n in range(len(probs)):
                xb = xs[n].astype(BF16)
                y = _dot(xb[:, :CHUNK], xb)
                xs[n] = jnp.where(left, y, xs[n] + y)
            yield
        for n in range(len(probs)):
            t_off = (xs[n][:, CHUNK:] - eye).astype(BF16)
            rs[n] = rs[n] + _dot(t_off, rs[n].astype(BF16))
        for n, (c, h) in enumerate(probs):
            u_ref[slot_build, c, h] = rs[n][:, :GDN_HEAD_DIM]
            wqg_ref[slot_build, c, h] = jnp.concatenate([rs[n][:, GDN_HEAD_DIM:], qs[n]], axis=0).astype(BF16)
            ik_ref[slot_build, c, h] = jnp.concatenate([intras[n], ks[n].T], axis=0).astype(BF16)
            eg_ref[slot_build, c, h] = jnp.broadcast_to(egs[n], (8, GDN_HEAD_DIM))
        yield

    def scan_chunks(ci):
        for cc in range(CHUNKS_PER_STEP):
            i = ci * CHUNKS_PER_STEP + cc
            c = i + dirn * (nc - 1 - 2 * i)
            r0 = pl.multiple_of(c * CHUNK, CHUNK)
            states = [s_ref[h] for h in range(GDN_HEADS)]
            m1 = [_dot(wqg_ref[slot_scan, c, h], states[h].astype(BF16)) for h in range(GDN_HEADS)]
            yield
            v_new = [(u_ref[slot_scan, c, h] - m1[h][:CHUNK]).astype(BF16) for h in range(GDN_HEADS)]
            m2 = [_dot(ik_ref[slot_scan, c, h], v_new[h]) for h in range(GDN_HEADS)]
            for h in range(GDN_HEADS):
                hs = slice(h * GDN_HEAD_DIM, (h + 1) * GDN_HEAD_DIM)
                o_ref[0, 0, pl.ds(r0, CHUNK), hs] = m1[h][CHUNK:] + m2[h][:CHUNK]
                s_ref[h] = states[h] * eg_ref[slot_scan, c, h][0:1, :] + m2[h][CHUNK:]
            yield

    def run(*makers):
        def body(ci, carry):
            live = [m(ci) for m in makers]
            while live:
                live = [g for g in live if next(g, StopIteration) is not StopIteration]
            return carry
        lax.fori_loop(0, nc // CHUNKS_PER_STEP, body, 0)

    @pl.when(j == 0)
    def _():
        s_ref[...] = jnp.zeros(s_ref.shape, F32)
        run(chunk_matrices)

    @pl.when((j > 0) & (j < nblk))
    def _():
        run(chunk_matrices, scan_chunks)

    @pl.when(j == nblk)
    def _():
        run(scan_chunks)


def _gdn_scan(q, k, v, gcol, grow, *, ts):
    b, nh, s, hd = q.shape
    w = nh * hd
    nblk = s // ts
    nc = ts // CHUNK
    order = lambda d, j: j + d * (nblk - 1 - 2 * j)
    src = lambda d, j: order(d, jnp.minimum(j, nblk - 1))
    dst = lambda d, j: order(d, jnp.maximum(j - 1, 0))
    return pl.pallas_call(
        functools.partial(_gdn_scan_kernel, nc=nc, nblk=nblk),
        grid=(b, 2, nblk + 1),
        in_specs=[pl.BlockSpec((1, nh, ts, hd), lambda i, d, j: (i, 0, src(d, j), 0))] * 3
                 + [pl.BlockSpec((1, ts, LANES), lambda i, d, j: (i, src(d, j), 0)),
                    pl.BlockSpec((1, nc, 8, CHUNK), lambda i, d, j: (i, src(d, j), 0, 0))],
        out_specs=pl.BlockSpec((1, 1, ts, w), lambda i, d, j: (d, i, dst(d, j), 0)),
        out_shape=jax.ShapeDtypeStruct((2, b, s, w), F32),
        scratch_shapes=[pltpu.VMEM((GDN_HEADS, GDN_HEAD_DIM, GDN_HEAD_DIM), F32),
                        pltpu.VMEM((2, nc, GDN_HEADS, 2 * CHUNK, GDN_HEAD_DIM), BF16),
                        pltpu.VMEM((2, nc, GDN_HEADS, CHUNK, GDN_HEAD_DIM), F32),
                        pltpu.VMEM((2, nc, GDN_HEADS, CHUNK + GDN_HEAD_DIM, CHUNK), BF16),
                        pltpu.VMEM((2, nc, GDN_HEADS, 8, GDN_HEAD_DIM), F32)],
        compiler_params=_cparams(("arbitrary", "arbitrary", "arbitrary")),
        name="gdn_scan",
    )(q, k, v, gcol, grow)


SWA_TQ = 128
SWA_TK = SWA_TQ + 2 * BAND_RADIUS
SWA_HALO = BAND_RADIUS * max(DILATIONS)
COPY_ROWS = 512
SWA_TILES_PER_STEP = 4


def _swa_kernel(q_ref, k_ref, v_ref, bias_ref, o_ref,
                kn_ref, vp_ref, m0_ref, m1_ref, l_ref, acc_ref, bias_scr, *, seq, qb):
    qi = pl.program_id(2)
    qn_ref = q_ref.at[0]
    lane = lax.broadcasted_iota(jnp.int32, (1, LANES), 1)
    low = lane < SWA_HEAD_DIM
    nt = (((1,), (1,)), ((), ()))

    @pl.when(qi == 0)
    def _():
        for pi in range(len(DILATIONS)):
            for hh in range(2):
                full = jnp.broadcast_to(bias_ref[0, pi, hh], (SWA_TQ, SWA_TK))
                bias_scr[pi, hh] = pltpu.roll(full, 0, 1, stride=1, stride_axis=0)

        zeros = jnp.zeros((SWA_HALO, LANES), F32)
        for ref in (kn_ref, vp_ref):
            ref[0:SWA_HALO, :] = zeros
            ref[SWA_HALO + seq:2 * SWA_HALO + seq, :] = zeros

        def copy_rows(i, carry):
            r0 = pl.multiple_of(i * COPY_ROWS, COPY_ROWS)
            kn_ref[pl.ds(SWA_HALO + r0, COPY_ROWS), :] = k_ref[0, pl.ds(r0, COPY_ROWS), :]
            vp_ref[pl.ds(SWA_HALO + r0, COPY_ROWS), :] = v_ref[0, pl.ds(r0, COPY_ROWS), :]
            return carry

        lax.fori_loop(0, seq // COPY_ROWS, copy_rows, 0)

    kcol = lax.broadcasted_iota(jnp.int32, (1, SWA_TK), 1)

    def rows(ref, start, size, stride):
        if stride == 1:
            return ref[pl.ds(start, size), :]
        return ref[pl.ds(start, size, stride=stride), :]

    def put(ref, start, size, stride, val):
        if stride == 1:
            ref[pl.ds(start, size), :] = val
        else:
            ref[pl.ds(start, size, stride=stride), :] = val

    def logits(pi, dil, g):
        sub_len = seq // dil
        tiles_per_res = qb // dil // SWA_TQ
        ss = []
        for u in range(SWA_TILES_PER_STEP):
            res, t = divmod(g * SWA_TILES_PER_STEP + u, tiles_per_res)
            tau0 = qi * (qb // dil) + t * SWA_TQ
            loc0 = res + dil * (t * SWA_TQ)
            krow = SWA_HALO + qi * qb + loc0 - dil * BAND_RADIUS
            qt = rows(qn_ref, qi * qb + loc0, SWA_TQ, dil)
            kt = rows(kn_ref, krow, SWA_TK, dil).astype(BF16)
            kidx = kcol + (tau0 - BAND_RADIUS)
            kvalid = (kidx >= 0) & (kidx < sub_len)
            for hh in range(2):
                mine = low if hh == 0 else jnp.logical_not(low)
                qh = jnp.where(mine, qt, 0.0).astype(BF16)
                s = lax.dot_general(qh, kt, nt, preferred_element_type=F32)
                ss.append(jnp.where(kvalid, s + bias_scr[pi, hh], NEG_BIG))
        return ss

    def accumulate(dil, g, ss, first):
        tiles_per_res = qb // dil // SWA_TQ
        loc, vts, old = [], [], []
        for u in range(SWA_TILES_PER_STEP):
            res, t = divmod(g * SWA_TILES_PER_STEP + u, tiles_per_res)
            loc0 = res + dil * (t * SWA_TQ)
            vt = rows(vp_ref, SWA_HALO + qi * qb + loc0 - dil * BAND_RADIUS, SWA_TK, dil)
            vts.append((jnp.where(low, vt, 1.0).astype(BF16), jnp.where(low, 1.0, vt).astype(BF16)))
            loc.append(loc0)
            if not first:
                old.append(tuple(rows(ref, loc0, SWA_TQ, dil) for ref in (m0_ref, m1_ref, l_ref, acc_ref)))
        m_new, alpha, ps = [], [], []
        for n, s in enumerate(ss):
            u, hh = divmod(n, 2)
            mt = jnp.max(s, axis=-1, keepdims=True)
            if first:
                mn = jnp.broadcast_to(mt, (SWA_TQ, LANES))
            else:
                mo = old[u][hh]
                mn = jnp.maximum(mo, mt)
                alpha.append(jnp.exp2(mo - mn))
            m_new.append(mn)
            ps.append(jnp.exp2(s - jnp.concatenate([mn, mn], axis=1)).astype(BF16))
        pv = [_dot(p, vts[n // 2][n % 2]) for n, p in enumerate(ps)]
        for u in range(SWA_TILES_PER_STEP):
            pv0, pv1 = pv[2 * u], pv[2 * u + 1]
            put(m0_ref, loc[u], SWA_TQ, dil, m_new[2 * u])
            put(m1_ref, loc[u], SWA_TQ, dil, m_new[2 * u + 1])
            lsum = jnp.where(low, pv1, pv0)
            pvv = jnp.where(low, pv0, pv1)
            if first:
                put(l_ref, loc[u], SWA_TQ, dil, lsum)
                put(acc_ref, loc[u], SWA_TQ, dil, pvv)
            else:
                a0, a1 = alpha[2 * u], alpha[2 * u + 1]
                put(l_ref, loc[u], SWA_TQ, dil, old[u][2] * jnp.where(low, a1, a0) + lsum)
                put(acc_ref, loc[u], SWA_TQ, dil, old[u][3] * jnp.where(low, a0, a1) + pvv)

    units = [(pi, dil, g) for pi, dil in enumerate(DILATIONS) for g in range(qb // SWA_TQ // SWA_TILES_PER_STEP)]
    pending = logits(*units[0])
    for k, (pi, dil, g) in enumerate(units):
        nxt = logits(*units[k + 1]) if k + 1 < len(units) else None
        accumulate(dil, g, pending, first=pi == 0)
        pending = nxt

    o_ref[0] = acc_ref[...] / pltpu.roll(l_ref[...], SWA_HEAD_DIM, 1)


def _t5_bucket(rel):
    nb = REL_BUCKETS // 2
    bucket = (rel > 0).astype(np.int32) * nb
    n = np.abs(rel)
    max_exact = nb // 2
    large = max_exact + (np.log(np.maximum(n, 1) / max_exact)
                         / math.log(REL_MAX_DISTANCE / max_exact) * (nb - max_exact)).astype(np.int32)
    large = np.minimum(large, nb - 1)
    return (bucket + np.where(n < max_exact, n, large)).astype(np.int32)


def _band_bias(rel_bias):
    rel = np.arange(2 * BAND_RADIUS + 1) - BAND_RADIUS
    rows = []
    for dil in DILATIONS:
        inside = jnp.transpose(rel_bias[_t5_bucket(rel * dil)]) * LOG2E
        rows.append(jnp.pad(inside, ((0, 0), (0, SWA_TK - inside.shape[1])), constant_values=NEG_BIG))
    bias = jnp.stack(rows, axis=0)
    bias = bias.reshape(len(DILATIONS), SWA_HEADS // 2, 2, 1, SWA_TK)
    return jnp.transpose(bias, (1, 0, 2, 3, 4)).astype(F32)


def _swa(qkvb, bias, *, qb):
    b, s, _ = qkvb.shape
    pairs = SWA_HEADS // 2
    assert s % qb == 0 and qb % (max(DILATIONS) * SWA_TQ) == 0
    col = lambda base: pl.BlockSpec((1, s, LANES), lambda i, p, j: (i, 0, base + p))
    padded = s + 2 * SWA_HALO
    return pl.pallas_call(
        functools.partial(_swa_kernel, seq=s, qb=qb),
        grid=(b, pairs, s // qb),
        in_specs=[col(0), col(pairs), col(2 * pairs),
                  pl.BlockSpec((1,) + bias.shape[1:], lambda i, p, j: (p, 0, 0, 0, 0))],
        out_specs=pl.BlockSpec((1, qb, LANES), lambda i, p, j: (i, j, p)),
        out_shape=jax.ShapeDtypeStruct((b, s, SWA_WIDTH), F32),
        scratch_shapes=[pltpu.VMEM((padded, LANES), F32), pltpu.VMEM((padded, LANES), F32)]
                       + [pltpu.VMEM((qb, LANES), F32)] * 4
                       + [pltpu.VMEM((len(DILATIONS), 2, SWA_TQ, SWA_TK), F32)],
        compiler_params=_cparams(("arbitrary", "arbitrary", "arbitrary")),
        name="swa",
    )(qkvb, qkvb, qkvb, bias)


FFN_TM = 512
FFN_FCHUNK = 256
PREP_TS = 512
SCAN_TS = 1024
SWA_QB = 2048


def _lane_row(vals, copies_at):
    row = jnp.zeros((LANES,), F32)
    for off in copies_at:
        row = row.at[off:off + vals.shape[0]].set(vals)
    return row[None, :]


def kernel(x, ffn1_norm, ffn1_w_gate, ffn1_w_up, ffn1_w_down, mix_norm, w_in, conv_w, a_log, dt_bias, gdn_norm_w, q_norm_w, k_norm_w, rel_bias, w_out, ffn2_norm, ffn2_w_gate, ffn2_w_up, ffn2_w_down, final_norm):
    b, s, d = x.shape
    n = b * s
    x2d = x.reshape(n, d)
    ng = 2 * GDN_HEADS
    c_z = 3 * GDN_WIDTH
    c_a = c_z + GDN_WIDTH
    c_b = c_a + ng
    c_qkvb = c_b + ng
    bias = _band_bias(rel_bias)
    for l in range(ffn1_norm.shape[0]):
        wl = w_in[l]
        w_a = wl[:, c_a:c_b]
        w_gates = jnp.concatenate([w_a, wl[:, c_b:c_qkvb], w_a, w_a, jnp.zeros((d, LANES - 4 * ng), F32)], axis=1)
        win = jnp.concatenate([wl[:, :c_a], w_gates, wl[:, c_qkvb:]], axis=1).astype(BF16)
        copies = (0, 2 * ng, 3 * ng)
        alog = _lane_row(a_log[l].reshape(ng), copies)
        dtb = _lane_row(dt_bias[l].reshape(ng), copies)
        x1, qkva, z, ab, qkvb = _ffn1_proj(
            x2d, ffn1_norm[l][None, :], ffn1_w_gate[l].astype(BF16), ffn1_w_up[l].astype(BF16),
            ffn1_w_down[l].astype(BF16), mix_norm[l][None, :], win,
            jnp.tile(q_norm_w[l], 2)[None, :], jnp.tile(k_norm_w[l], 2)[None, :], tm=FFN_TM, fchunk=FFN_FCHUNK)
        q, k, v, gcol, grow = _gdn_prep(
            qkva.reshape(b, s, -1), ab.reshape(b, s, LANES), jnp.transpose(conv_w[l]), alog, dtb, ts=PREP_TS)
        o_dirs = _gdn_scan(q, k, v, gcol, grow, ts=SCAN_TS)
        attn = _swa(qkvb.reshape(b, s, -1), bias, qb=SWA_QB)
        x2d = _out_ffn2(
            x1, o_dirs.reshape(2, n, -1), z, attn.reshape(n, -1),
            gdn_norm_w[l][None, :], w_out[l].astype(BF16), ffn2_norm[l][None, :],
            ffn2_w_gate[l].astype(BF16), ffn2_w_up[l].astype(BF16), ffn2_w_down[l].astype(BF16),
            final_norm[l][None, :], tm=FFN_TM, fchunk=FFN_FCHUNK)
    return x2d.reshape(b, s, d)
```

```python
import functools
import math

import numpy as np
import jax
import jax.numpy as jnp
from jax import lax
from jax.experimental import pallas as pl
from jax.experimental.pallas import tpu as pltpu

F32 = jnp.float32
BF16 = jnp.bfloat16
EPS = 1e-6
NEG_BIG = -1e30
LOG2E = math.log2(math.e)

LANES = 128
GDN_HEADS = 4
GDN_HEAD_DIM = 128
GDN_WIDTH = GDN_HEADS * GDN_HEAD_DIM
CONV_WIDTH = 5
CHUNK = 64
SWA_HEADS = 8
SWA_HEAD_DIM = 64
SWA_WIDTH = SWA_HEADS * SWA_HEAD_DIM
DILATIONS = (1, 4, 16)
BAND_RADIUS = 64
REL_BUCKETS = 32
REL_MAX_DISTANCE = 1024
VMEM_LIMIT = 56 * 1024 * 1024


def _cparams(sem):
    return pltpu.CompilerParams(dimension_semantics=sem, vmem_limit_bytes=VMEM_LIMIT)


def _resident(shape):
    zeros = (0,) * len(shape)
    return pl.BlockSpec(shape, lambda *_: zeros, pipeline_mode=pl.Buffered(1))


def _rms(x, w):
    return x * lax.rsqrt(jnp.mean(x * x, axis=-1, keepdims=True) + EPS) * w


def _silu(x):
    return x * (1.0 / (1.0 + jnp.exp(-x)))


def _dot(a, b):
    return jnp.dot(a, b, preferred_element_type=F32)


def _zero_after(values):
    s = values[0:8]
    for r in range(8, values.shape[0], 8):
        s = s + values[r:r + 8]
    bits = lax.bitcast_convert_type(s, jnp.int32)
    return lax.shift_right_logical(lax.shift_right_logical(bits, 16), 16).astype(F32)


def _swiglu(h, wg_ref, wu_ref, wd_ref, fchunk, order_after=()):
    acc = None
    for n, c0 in enumerate(range(0, wg_ref.shape[1], fchunk)):
        g = _dot(h, wg_ref[:, c0:c0 + fchunk])
        if n < len(order_after):
            first = jnp.concatenate([g[0:8, 0:LANES] + order_after[n], g[0:8, LANES:]], axis=1)
            g = jnp.concatenate([first, g[8:]], axis=0)
        u = _dot(h, wu_ref[:, c0:c0 + fchunk])
        a = (_silu(g) * u).astype(BF16)
        d = _dot(a, wd_ref[c0:c0 + fchunk, :])
        acc = d if acc is None else acc + d
    return acc


def _ffn1_proj_kernel(x_ref, n1_ref, wg_ref, wu_ref, wd_ref, nm_ref, win_ref, qw_ref, kw_ref,
                      cw_ref, alog_ref, dtb_ref,
                      x1_ref, z_ref, qkvb_ref, q_ref, k_ref, v_ref, gcol_ref, grow_ref, pad_ref, tail_ref,
                      *, fchunk, tiles_per_seq):
    i = pl.program_id(0)
    tm = x_ref.shape[0]
    heads_out = (q_ref, k_ref, v_ref)

    @pl.when(i == 0)
    def _():
        pad_ref[...] = jnp.zeros(pad_ref.shape, F32)

    conv_done = []
    for cb in range(3 * GDN_HEADS):
        even, odd = _gdn_conv_head(pad_ref.at[cb], cw_ref, cb, heads_out[cb // GDN_HEADS], 0, tm - CONV_TAIL)
        conv_done.append(_zero_after(even + odd))
    nchunks = wg_ref.shape[1] // fchunk
    order_after = conv_done[:nchunks - 1] + [functools.reduce(lambda a, b: a + b, conv_done[nchunks - 1:])]

    seq_start = lax.rem(i, tiles_per_seq) == 0
    x = x_ref[...]
    h = _rms(x, n1_ref[...]).astype(BF16)
    x1 = x + 0.5 * _swiglu(h, wg_ref, wu_ref, wd_ref, fchunk, order_after)
    x1_ref[...] = x1
    h2 = _rms(x1, nm_ref[...]).astype(BF16)
    c = 3 * GDN_WIDTH
    qkva = _dot(h2, win_ref[:, :c])
    z_ref[...] = _dot(h2, win_ref[:, c:c + GDN_WIDTH])
    c += GDN_WIDTH
    _gdn_gates(_dot(h2, win_ref[:, c:c + LANES]), alog_ref[...], dtb_ref[...], gcol_ref, grow_ref)
    c += LANES

    for cb in range(3 * GDN_HEADS):
        cs = slice(cb * LANES, (cb + 1) * LANES)
        tail_ref[cb, 0:8 + CONV_TAIL, :] = pad_ref[cb, tm - CONV_TAIL:8 + tm, :]
        tail_ref[cb, 8 + CONV_TAIL:16 + CONV_TAIL, :] = jnp.where(seq_start, 0.0, qkva[0:8, cs])
        _gdn_conv_head(tail_ref.at[cb], cw_ref, cb, heads_out[cb // GDN_HEADS], tm - CONV_TAIL, CONV_TAIL)
        pad_ref[cb, 0:8, :] = jnp.where(seq_start, 0.0, pad_ref[cb, tm:tm + 8, :])
        pad_ref[cb, 8:8 + tm, :] = qkva[:, cs]

    low = lax.broadcasted_iota(jnp.int32, (1, LANES), 1) < SWA_HEAD_DIM

    def head_rms(y, w):
        y2 = y * y
        s0 = jnp.sum(jnp.where(low, y2, 0.0), axis=-1, keepdims=True)
        s1 = jnp.sum(jnp.where(low, 0.0, y2), axis=-1, keepdims=True)
        ms = jnp.where(low, s0, s1) * (1.0 / SWA_HEAD_DIM)
        return y * lax.rsqrt(ms + EPS) * w

    qscale = qw_ref[...] * (SWA_HEAD_DIM ** -0.5 * LOG2E)
    wide = 2 * LANES
    for j in range(3 * SWA_WIDTH // wide):
        y = _dot(h2, win_ref[:, c + j * wide:c + (j + 1) * wide])
        for half in range(2):
            p = 2 * j + half
            yp = y[:, half * LANES:(half + 1) * LANES]
            if p < SWA_WIDTH // LANES:
                yp = head_rms(yp, qscale)
            elif p < 2 * SWA_WIDTH // LANES:
                yp = head_rms(yp, kw_ref[...])
            qkvb_ref[:, p * LANES:(p + 1) * LANES] = yp


def _ffn1_proj(x2d, n1, wg, wu, wd, nm, win, qw, kw, cw, alog, dtb, *, batch, tm, fchunk):
    n, d = x2d.shape
    f = wg.shape[1]
    seq = n // batch
    assert win.shape[1] == 4 * GDN_WIDTH + LANES + 3 * SWA_WIDTH and seq % tm == 0
    nt = n // tm
    tps = seq // tm
    cur = lambda i: jnp.minimum(i, nt - 1)
    late = lambda i: jnp.maximum(i - 1, 0)
    row = lambda w: pl.BlockSpec((tm, w), lambda i: (cur(i), 0))
    heads = pl.BlockSpec((1, GDN_HEADS, tm, GDN_HEAD_DIM), lambda i: (late(i) // tps, 0, late(i) % tps, 0))
    f32 = lambda *shape: jax.ShapeDtypeStruct(shape, F32)
    return pl.pallas_call(
        functools.partial(_ffn1_proj_kernel, fchunk=fchunk, tiles_per_seq=tps),
        grid=(nt + 1,),
        in_specs=[row(d), _resident((1, d)), _resident((d, f)), _resident((d, f)), _resident((f, d)),
                  _resident((1, d)), _resident(win.shape), _resident((1, LANES)), _resident((1, LANES)),
                  _resident(cw.shape), _resident((1, LANES)), _resident((1, LANES))],
        out_specs=[row(d), row(GDN_WIDTH), row(3 * SWA_WIDTH), heads, heads, heads, row(LANES),
                   pl.BlockSpec((tm // CHUNK, 8, CHUNK), lambda i: (cur(i), 0, 0))],
        out_shape=[f32(n, d), f32(n, GDN_WIDTH), f32(n, 3 * SWA_WIDTH)]
                  + [f32(batch, GDN_HEADS, seq, GDN_HEAD_DIM)] * 3 + [f32(n, LANES), f32(n // CHUNK, 8, CHUNK)],
        scratch_shapes=[pltpu.VMEM((3 * GDN_HEADS, tm + 8, LANES), F32),
                        pltpu.VMEM((3 * GDN_HEADS, CONV_TAIL + 16, LANES), F32)],
        compiler_params=_cparams(("arbitrary",)),
        name="ffn1_proj",
    )(x2d, n1, wg, wu, wd, nm, win, qw, kw, cw, alog, dtb)


def _out_ffn2_kernel(x1_ref, of_ref, ob_ref, z_ref, attn_ref, gnw_ref, wout_ref, n2_ref,
                     wg_ref, wu_ref, wd_ref, nf_ref, out_ref, *, fchunk):
    o = of_ref[0] + ob_ref[0]
    z = z_ref[...]
    gnw = gnw_ref[...]
    heads = []
    for h in range(GDN_HEADS):
        sl = slice(h * GDN_HEAD_DIM, (h + 1) * GDN_HEAD_DIM)
        heads.append(_rms(o[:, sl], gnw) * _silu(z[:, sl]))
    oa = jnp.concatenate(heads, axis=1).astype(BF16)
    mix = _dot(oa, wout_ref[:GDN_WIDTH, :]) + _dot(attn_ref[...].astype(BF16), wout_ref[GDN_WIDTH:, :])
    x2 = x1_ref[...] + mix
    h2 = _rms(x2, n2_ref[...]).astype(BF16)
    x3 = x2 + 0.5 * _swiglu(h2, wg_ref, wu_ref, wd_ref, fchunk)
    out_ref[...] = _rms(x3, nf_ref[...])


def _out_ffn2(x1, o_dirs, z, attn, gnw, wout, n2, wg, wu, wd, nf, *, tm, fchunk):
    n, d = x1.shape
    f = wg.shape[1]
    row = lambda w: pl.BlockSpec((tm, w), lambda i: (i, 0))
    direction = lambda k: pl.BlockSpec((1, tm, GDN_WIDTH), lambda i: (k, i, 0))
    return pl.pallas_call(
        functools.partial(_out_ffn2_kernel, fchunk=fchunk),
        grid=(n // tm,),
        in_specs=[row(d), direction(0), direction(1), row(GDN_WIDTH), row(SWA_WIDTH),
                  _resident((1, GDN_HEAD_DIM)), _resident(wout.shape), _resident((1, d)),
                  _resident((d, f)), _resident((d, f)), _resident((f, d)), _resident((1, d))],
        out_specs=row(d),
        out_shape=jax.ShapeDtypeStruct((n, d), F32),
        compiler_params=_cparams(("arbitrary",)),
        name="out_ffn2",
    )(x1, o_dirs, o_dirs, z, attn, gnw, wout, n2, wg, wu, wd, nf)


CONV_TAIL = 16


def _gdn_conv_head(src, cw_ref, cb, dst, out0, count):
    cs = slice(cb * LANES, (cb + 1) * LANES)
    half = count // 2
    done = []
    for par in range(2):
        acc = None
        for t in range(CONV_WIDTH):
            term = src[pl.ds(6 + t + par, half, stride=2), :] * cw_ref[t:t + 1, cs]
            acc = term if acc is None else acc + term
        y = _silu(acc)
        if cb < 2 * GDN_HEADS:
            y = y * lax.rsqrt(jnp.sum(y * y, axis=-1, keepdims=True) + EPS)
        if cb < GDN_HEADS:
            y = y * (GDN_HEAD_DIM ** -0.5)
        dst[0, cb % GDN_HEADS, pl.ds(out0 + par, half, stride=2), :] = y
        done.append(y)
    return done


def _gdn_gates(ab, alog, dtb, gcol_ref, grow_ref):
    ts = ab.shape[0]
    xs = ab + dtb
    softplus = jnp.maximum(xs, 0.0) + jnp.log(1.0 + jnp.exp(-jnp.abs(xs)))
    g = -jnp.exp(alog) * softplus
    beta = 1.0 / (1.0 + jnp.exp(-ab))
    g1 = g.astype(BF16)
    r1 = g - g1.astype(F32)
    g2 = r1.astype(BF16)
    g3 = (r1 - g2.astype(F32)).astype(BF16)
    ri = lax.broadcasted_iota(jnp.int32, (LANES, LANES), 0)
    ci = lax.broadcasted_iota(jnp.int32, (LANES, LANES), 1)
    same = (ri // CHUNK) == (ci // CHUNK)
    lower = jnp.where(same & (ri >= ci), 1.0, 0.0).astype(BF16)
    upper = jnp.where(same & (ri <= ci), 1.0, 0.0).astype(BF16)
    lane = lax.broadcasted_iota(jnp.int32, (1, LANES), 1)
    for s in range(ts // LANES):
        rs = slice(s * LANES, (s + 1) * LANES)
        pre = _dot(lower, g1[rs]) + _dot(lower, g2[rs]) + _dot(lower, g3[rs])
        suf = _dot(upper, g1[rs]) + _dot(upper, g2[rs]) + _dot(upper, g3[rs])
        gs = g[rs]
        col = jnp.where(lane < 4, pre,
              jnp.where(lane < 8, suf,
              jnp.where(lane < 16, beta[rs],
              jnp.where(lane < 20, suf - gs,
              jnp.where(lane < 24, pre - gs, pre + suf - gs)))))
        gcol_ref[rs, :] = col
        colt = col.T
        grow_ref[2 * s] = colt[0:8, 0:CHUNK]
        grow_ref[2 * s + 1] = colt[0:8, CHUNK:2 * CHUNK]


CHUNKS_PER_STEP = 4

def _gdn_scan_kernel(q_ref, k_ref, v_ref, gcol_ref, grow_ref, o_ref,
                     s_ref, wqg_ref, u_ref, ik_ref, eg_ref, *, nc, nblk):
    dirn = pl.program_id(1)
    j = pl.program_id(2)
    fwd = dirn == 0
    slot_build = lax.rem(j, 2)
    slot_scan = 1 - slot_build

    ri = lax.broadcasted_iota(jnp.int32, (CHUNK, CHUNK), 0)
    ci = lax.broadcasted_iota(jnp.int32, (CHUNK, CHUNK), 1)
    dd = (ri - ci) * (1 - 2 * dirn)
    incl = dd >= 0
    strict = dd > 0
    eye = jnp.where(ri == ci, 1.0, 0.0)
    left = lax.broadcasted_iota(jnp.int32, (CHUNK, 2 * CHUNK), 1) < CHUNK
    nt = (((1,), (1,)), ((), ()))

    nsq = int(math.log2(CHUNK))

    def chunk_matrices(ci):
        probs = [(ci * CHUNKS_PER_STEP + cc, h) for cc in range(CHUNKS_PER_STEP) for h in range(GDN_HEADS)]
        qs, ks, egs, rs, ps, intras = [], [], [], [], [], []
        for c, h in probs:
            r0 = pl.multiple_of(c * CHUNK, CHUNK)
            gates = gcol_ref[0, pl.ds(r0, CHUNK), :]
            rows = grow_ref[0, c]

            def pick(base):
                return jnp.where(fwd, gates[:, base + h:base + h + 1], gates[:, base + 4 + h:base + 5 + h])
            gc, beta, gdec, gl = pick(0), pick(8), pick(16), pick(24)
            grow = jnp.where(fwd, rows[h:h + 1, :], rows[4 + h:5 + h, :])
            qc = q_ref[0, h, pl.ds(r0, CHUNK), :]
            kc = k_ref[0, h, pl.ds(r0, CHUNK), :]
            vc = v_ref[0, h, pl.ds(r0, CHUNK), :]
            dec = jnp.where(incl, jnp.exp(jnp.where(incl, gc - grow, 0.0)), 0.0)
            kb = kc * beta
            qk = lax.dot_general(jnp.concatenate([kb, qc], axis=0).astype(BF16), kc.astype(BF16), nt,
                                 preferred_element_type=F32)
            ps.append(-jnp.where(strict, qk[:CHUNK] * dec, 0.0))
            intras.append(qk[CHUNK:] * dec)
            eg = jnp.exp(gc)
            rs.append(jnp.concatenate([vc * beta, kb * eg], axis=1))
            qs.append(qc * eg)
            ks.append(kc * jnp.exp(gdec))
            egs.append(jnp.exp(gl[0:1, :]))
        yield
        xs = [jnp.concatenate([p, eye], axis=1) for p in ps]
        for t in range(nsq):
            for n in range(len(probs)):
                xb = xs[n].astype(BF16)
                y = _dot(xb[:, :CHUNK], xb)
                xs[n] = jnp.where(left, y, xs[n] + y)
            yield
        for n in range(len(probs)):
            t_off = (xs[n][:, CHUNK:] - eye).astype(BF16)
            rs[n] = rs[n] + _dot(t_off, rs[n].astype(BF16))
        for n, (c, h) in enumerate(probs):
            u_ref[slot_build, c, h] = rs[n][:, :GDN_HEAD_DIM]
            wqg_ref[slot_build, c, h] = jnp.concatenate([rs[n][:, GDN_HEAD_DIM:], qs[n]], axis=0).astype(BF16)
            ik_ref[slot_build, c, h] = jnp.concatenate([intras[n], ks[n].T], axis=0).astype(BF16)
            eg_ref[slot_build, c, h] = jnp.broadcast_to(egs[n], (8, GDN_HEAD_DIM))
        yield

    def scan_chunks(ci):
        for cc in range(CHUNKS_PER_STEP):
            i = ci * CHUNKS_PER_STEP + cc
            c = i + dirn * (nc - 1 - 2 * i)
            r0 = pl.multiple_of(c * CHUNK, CHUNK)
            states = [s_ref[h] for h in range(GDN_HEADS)]
            m1 = [_dot(wqg_ref[slot_scan, c, h], states[h].astype(BF16)) for h in range(GDN_HEADS)]
            yield
            v_new = [(u_ref[slot_scan, c, h] - m1[h][:CHUNK]).astype(BF16) for h in range(GDN_HEADS)]
            m2 = [_dot(ik_ref[slot_scan, c, h], v_new[h]) for h in range(GDN_HEADS)]
            for h in range(GDN_HEADS):
                hs = slice(h * GDN_HEAD_DIM, (h + 1) * GDN_HEAD_DIM)
                o_ref[0, 0, pl.ds(r0, CHUNK), hs] = m1[h][CHUNK:] + m2[h][:CHUNK]
                s_ref[h] = states[h] * eg_ref[slot_scan, c, h][0:1, :] + m2[h][CHUNK:]
            yield

    def run(*makers):
        def body(ci, carry):
            live = [m(ci) for m in makers]
            while live:
                live = [g for g in live if next(g, StopIteration) is not StopIteration]
            return carry
        lax.fori_loop(0, nc // CHUNKS_PER_STEP, body, 0)

    @pl.when(j == 0)
    def _():
        s_ref[...] = jnp.zeros(s_ref.shape, F32)
        run(chunk_matrices)

    @pl.when((j > 0) & (j < nblk))
    def _():
        run(chunk_matrices, scan_chunks)

    @pl.when(j == nblk)
    def _():
        run(scan_chunks)


def _gdn_scan(q, k, v, gcol, grow, *, ts):
    b, nh, s, hd = q.shape
    w = nh * hd
    nblk = s // ts
    nc = ts // CHUNK
    order = lambda d, j: j + d * (nblk - 1 - 2 * j)
    src = lambda d, j: order(d, jnp.minimum(j, nblk - 1))
    dst = lambda d, j: order(d, jnp.maximum(j - 1, 0))
    return pl.pallas_call(
        functools.partial(_gdn_scan_kernel, nc=nc, nblk=nblk),
        grid=(b, 2, nblk + 1),
        in_specs=[pl.BlockSpec((1, nh, ts, hd), lambda i, d, j: (i, 0, src(d, j), 0))] * 3
                 + [pl.BlockSpec((1, ts, LANES), lambda i, d, j: (i, src(d, j), 0)),
                    pl.BlockSpec((1, nc, 8, CHUNK), lambda i, d, j: (i, src(d, j), 0, 0))],
        out_specs=pl.BlockSpec((1, 1, ts, w), lambda i, d, j: (d, i, dst(d, j), 0)),
        out_shape=jax.ShapeDtypeStruct((2, b, s, w), F32),
        scratch_shapes=[pltpu.VMEM((GDN_HEADS, GDN_HEAD_DIM, GDN_HEAD_DIM), F32),
                        pltpu.VMEM((2, nc, GDN_HEADS, 2 * CHUNK, GDN_HEAD_DIM), BF16),
                        pltpu.VMEM((2, nc, GDN_HEADS, CHUNK, GDN_HEAD_DIM), F32),
                        pltpu.VMEM((2, nc, GDN_HEADS, CHUNK + GDN_HEAD_DIM, CHUNK), BF16),
                        pltpu.VMEM((2, nc, GDN_HEADS, 8, GDN_HEAD_DIM), F32)],
        compiler_params=_cparams(("arbitrary", "arbitrary", "arbitrary")),
        name="gdn_scan",
    )(q, k, v, gcol, grow)


SWA_TQ = 128
SWA_TK = SWA_TQ + 2 * BAND_RADIUS
SWA_HALO = BAND_RADIUS * max(DILATIONS)
COPY_ROWS = 512
SWA_TILES_PER_STEP = 4


def _swa_kernel(q_ref, k_ref, v_ref, bias_ref, o_ref,
                kn_ref, vp_ref, m0_ref, m1_ref, l_ref, acc_ref, bias_scr, *, seq, qb):
    qi = pl.program_id(2)
    qn_ref = q_ref.at[0]
    lane = lax.broadcasted_iota(jnp.int32, (1, LANES), 1)
    low = lane < SWA_HEAD_DIM
    nt = (((1,), (1,)), ((), ()))

    @pl.when(qi == 0)
    def _():
        for pi in range(len(DILATIONS)):
            for hh in range(2):
                full = jnp.broadcast_to(bias_ref[0, pi, hh], (SWA_TQ, SWA_TK))
                bias_scr[pi, hh] = pltpu.roll(full, 0, 1, stride=1, stride_axis=0)

        zeros = jnp.zeros((SWA_HALO, LANES), F32)
        for ref in (kn_ref, vp_ref):
            ref[0:SWA_HALO, :] = zeros
            ref[SWA_HALO + seq:2 * SWA_HALO + seq, :] = zeros

        def copy_rows(i, carry):
            r0 = pl.multiple_of(i * COPY_ROWS, COPY_ROWS)
            kn_ref[pl.ds(SWA_HALO + r0, COPY_ROWS), :] = k_ref[0, pl.ds(r0, COPY_ROWS), :]
            vp_ref[pl.ds(SWA_HALO + r0, COPY_ROWS), :] = v_ref[0, pl.ds(r0, COPY_ROWS), :]
            return carry

        lax.fori_loop(0, seq // COPY_ROWS, copy_rows, 0)

    kcol = lax.broadcasted_iota(jnp.int32, (1, SWA_TK), 1)

    def rows(ref, start, size, stride):
        if stride == 1:
            return ref[pl.ds(start, size), :]
        return ref[pl.ds(start, size, stride=stride), :]

    def put(ref, start, size, stride, val):
        if stride == 1:
            ref[pl.ds(start, size), :] = val
        else:
            ref[pl.ds(start, size, stride=stride), :] = val

    def logits(pi, dil, g):
        sub_len = seq // dil
        tiles_per_res = qb // dil // SWA_TQ
        ss = []
        for u in range(SWA_TILES_PER_STEP):
            res, t = divmod(g * SWA_TILES_PER_STEP + u, tiles_per_res)
            tau0 = qi * (qb // dil) + t * SWA_TQ
            loc0 = res + dil * (t * SWA_TQ)
            krow = SWA_HALO + qi * qb + loc0 - dil * BAND_RADIUS
            qt = rows(qn_ref, qi * qb + loc0, SWA_TQ, dil)
            kt = rows(kn_ref, krow, SWA_TK, dil).astype(BF16)
            kidx = kcol + (tau0 - BAND_RADIUS)
            kvalid = (kidx >= 0) & (kidx < sub_len)
            for hh in range(2):
                mine = low if hh == 0 else jnp.logical_not(low)
                qh = jnp.where(mine, qt, 0.0).astype(BF16)
                s = lax.dot_general(qh, kt, nt, preferred_element_type=F32)
                ss.append(jnp.where(kvalid, s + bias_scr[pi, hh], NEG_BIG))
        return ss

    def accumulate(dil, g, ss, first):
        tiles_per_res = qb // dil // SWA_TQ
        loc, vts, old = [], [], []
        for u in range(SWA_TILES_PER_STEP):
            res, t = divmod(g * SWA_TILES_PER_STEP + u, tiles_per_res)
            loc0 = res + dil * (t * SWA_TQ)
            vt = rows(vp_ref, SWA_HALO + qi * qb + loc0 - dil * BAND_RADIUS, SWA_TK, dil)
            vts.append((jnp.where(low, vt, 1.0).astype(BF16), jnp.where(low, 1.0, vt).astype(BF16)))
            loc.append(loc0)
            if not first:
                old.append(tuple(rows(ref, loc0, SWA_TQ, dil) for ref in (m0_ref, m1_ref, l_ref, acc_ref)))
        m_new, alpha, ps = [], [], []
        for n, s in enumerate(ss):
            u, hh = divmod(n, 2)
            mt = jnp.max(s, axis=-1, keepdims=True)
            if first:
                mn = jnp.broadcast_to(mt, (SWA_TQ, LANES))
            else:
                mo = old[u][hh]
                mn = jnp.maximum(mo, mt)
                alpha.append(jnp.exp2(mo - mn))
            m_new.append(mn)
            ps.append(jnp.exp2(s - jnp.concatenate([mn, mn], axis=1)).astype(BF16))
        pv = [_dot(p, vts[n // 2][n % 2]) for n, p in enumerate(ps)]
        for u in range(SWA_TILES_PER_STEP):
            pv0, pv1 = pv[2 * u], pv[2 * u + 1]
            put(m0_ref, loc[u], SWA_TQ, dil, m_new[2 * u])
            put(m1_ref, loc[u], SWA_TQ, dil, m_new[2 * u + 1])
            lsum = jnp.where(low, pv1, pv0)
            pvv = jnp.where(low, pv0, pv1)
            if first:
                put(l_ref, loc[u], SWA_TQ, dil, lsum)
                put(acc_ref, loc[u], SWA_TQ, dil, pvv)
            else:
                a0, a1 = alpha[2 * u], alpha[2 * u + 1]
                put(l_ref, loc[u], SWA_TQ, dil, old[u][2] * jnp.where(low, a1, a0) + lsum)
                put(acc_ref, loc[u], SWA_TQ, dil, old[u][3] * jnp.where(low, a0, a1) + pvv)

    units = [(pi, dil, g) for pi, dil in enumerate(DILATIONS) for g in range(qb // SWA_TQ // SWA_TILES_PER_STEP)]
    pending = logits(*units[0])
    for k, (pi, dil, g) in enumerate(units):
        nxt = logits(*units[k + 1]) if k + 1 < len(units) else None
        accumulate(dil, g, pending, first=pi == 0)
        pending = nxt

    o_ref[0] = acc_ref[...] / pltpu.roll(l_ref[...], SWA_HEAD_DIM, 1)


def _t5_bucket(rel):
    nb = REL_BUCKETS // 2
    bucket = (rel > 0).astype(np.int32) * nb
    n = np.abs(rel)
    max_exact = nb // 2
    large = max_exact + (np.log(np.maximum(n, 1) / max_exact)
                         / math.log(REL_MAX_DISTANCE / max_exact) * (nb - max_exact)).astype(np.int32)
    large = np.minimum(large, nb - 1)
    return (bucket + np.where(n < max_exact, n, large)).astype(np.int32)


def _band_bias(rel_bias):
    rel = np.arange(2 * BAND_RADIUS + 1) - BAND_RADIUS
    rows = []
    for dil in DILATIONS:
        inside = jnp.transpose(rel_bias[_t5_bucket(rel * dil)]) * LOG2E
        rows.append(jnp.pad(inside, ((0, 0), (0, SWA_TK - inside.shape[1])), constant_values=NEG_BIG))
    bias = jnp.stack(rows, axis=0)
    bias = bias.reshape(len(DILATIONS), SWA_HEADS // 2, 2, 1, SWA_TK)
    return jnp.transpose(bias, (1, 0, 2, 3, 4)).astype(F32)


def _swa(qkvb, bias, *, qb):
    b, s, _ = qkvb.shape
    pairs = SWA_HEADS // 2
    assert s % qb == 0 and qb % (max(DILATIONS) * SWA_TQ) == 0
    col = lambda base: pl.BlockSpec((1, s, LANES), lambda i, p, j: (i, 0, base + p))
    padded = s + 2 * SWA_HALO
    return pl.pallas_call(
        functools.partial(_swa_kernel, seq=s, qb=qb),
        grid=(b, pairs, s // qb),
        in_specs=[col(0), col(pairs), col(2 * pairs),
                  pl.BlockSpec((1,) + bias.shape[1:], lambda i, p, j: (p, 0, 0, 0, 0))],
        out_specs=pl.BlockSpec((1, qb, LANES), lambda i, p, j: (i, j, p)),
        out_shape=jax.ShapeDtypeStruct((b, s, SWA_WIDTH), F32),
        scratch_shapes=[pltpu.VMEM((padded, LANES), F32), pltpu.VMEM((padded, LANES), F32)]
                       + [pltpu.VMEM((qb, LANES), F32)] * 4
                       + [pltpu.VMEM((len(DILATIONS), 2, SWA_TQ, SWA_TK), F32)],
        compiler_params=_cparams(("arbitrary", "arbitrary", "arbitrary")),
        name="swa",
    )(qkvb, qkvb, qkvb, bias)


FFN_TM = 512
FFN_FCHUNK = 256
SCAN_TS = 1024
SWA_QB = 2048


def _lane_row(vals, copies_at):
    row = jnp.zeros((LANES,), F32)
    for off in copies_at:
        row = row.at[off:off + vals.shape[0]].set(vals)
    return row[None, :]


def kernel(x, ffn1_norm, ffn1_w_gate, ffn1_w_up, ffn1_w_down, mix_norm, w_in, conv_w, a_log, dt_bias, gdn_norm_w, q_norm_w, k_norm_w, rel_bias, w_out, ffn2_norm, ffn2_w_gate, ffn2_w_up, ffn2_w_down, final_norm):
    b, s, d = x.shape
    n = b * s
    x2d = x.reshape(n, d)
    ng = 2 * GDN_HEADS
    c_z = 3 * GDN_WIDTH
    c_a = c_z + GDN_WIDTH
    c_b = c_a + ng
    c_qkvb = c_b + ng
    bias = _band_bias(rel_bias)
    for l in range(ffn1_norm.shape[0]):
        wl = w_in[l]
        w_a = wl[:, c_a:c_b]
        w_gates = jnp.concatenate([w_a, wl[:, c_b:c_qkvb], w_a, w_a, jnp.zeros((d, LANES - 4 * ng), F32)], axis=1)
        win = jnp.concatenate([wl[:, :c_a], w_gates, wl[:, c_qkvb:]], axis=1).astype(BF16)
        copies = (0, 2 * ng, 3 * ng)
        alog = _lane_row(a_log[l].reshape(ng), copies)
        dtb = _lane_row(dt_bias[l].reshape(ng), copies)
        x1, z, qkvb, q, k, v, gcol, grow = _ffn1_proj(
            x2d, ffn1_norm[l][None, :], ffn1_w_gate[l].astype(BF16), ffn1_w_up[l].astype(BF16),
            ffn1_w_down[l].astype(BF16), mix_norm[l][None, :], win,
            jnp.tile(q_norm_w[l], 2)[None, :], jnp.tile(k_norm_w[l], 2)[None, :],
            jnp.transpose(conv_w[l]), alog, dtb, batch=b, tm=FFN_TM, fchunk=FFN_FCHUNK)
        o_dirs = _gdn_scan(q, k, v, gcol.reshape(b, s, LANES), grow.reshape(b, s // CHUNK, 8, CHUNK), ts=SCAN_TS)
        attn = _swa(qkvb.reshape(b, s, -1), bias, qb=SWA_QB)
        x2d = _out_ffn2(
            x1, o_dirs.reshape(2, n, -1), z, attn.reshape(n, -1),
            gdn_norm_w[l][None, :], w_out[l].astype(BF16), ffn2_norm[l][None, :],
            ffn2_w_gate[l].astype(BF16), ffn2_w_up[l].astype(BF16), ffn2_w_down[l].astype(BF16),
            final_norm[l][None, :], tm=FFN_TM, fchunk=FFN_FCHUNK)
    return x2d.reshape(b, s, d)
```

```python
import functools
import math

import numpy as np
import jax
import jax.numpy as jnp
from jax import lax
from jax.experimental import pallas as pl
from jax.experimental.pallas import tpu as pltpu

F32 = jnp.float32
BF16 = jnp.bfloat16
EPS = 1e-6
NEG_BIG = -1e30
LOG2E = math.log2(math.e)

LANES = 128
GDN_HEADS = 4
GDN_HEAD_DIM = 128
GDN_WIDTH = GDN_HEADS * GDN_HEAD_DIM
CONV_WIDTH = 5
CHUNK = 64
SWA_HEADS = 8
SWA_HEAD_DIM = 64
SWA_WIDTH = SWA_HEADS * SWA_HEAD_DIM
DILATIONS = (1, 4, 16)
BAND_RADIUS = 64
REL_BUCKETS = 32
REL_MAX_DISTANCE = 1024
VMEM_LIMIT = 56 * 1024 * 1024


def _cparams(sem):
    return pltpu.CompilerParams(dimension_semantics=sem, vmem_limit_bytes=VMEM_LIMIT)


def _resident(shape):
    zeros = (0,) * len(shape)
    return pl.BlockSpec(shape, lambda *_: zeros, pipeline_mode=pl.Buffered(1))


def _rms(x, w):
    return x * lax.rsqrt(jnp.mean(x * x, axis=-1, keepdims=True) + EPS) * w


def _silu(x):
    return x * (1.0 / (1.0 + jnp.exp(-x)))


def _dot(a, b):
    return jnp.dot(a, b, preferred_element_type=F32)


def _zero_after(values):
    s = values[0:8]
    for r in range(8, values.shape[0], 8):
        s = s + values[r:r + 8]
    bits = lax.bitcast_convert_type(s, jnp.int32)
    return lax.shift_right_logical(lax.shift_right_logical(bits, 16), 16).astype(F32)


def _swiglu(h, wg_ref, wu_ref, wd_ref, fchunk, order_after=()):
    acc = None
    for n, c0 in enumerate(range(0, wg_ref.shape[1], fchunk)):
        g = _dot(h, wg_ref[:, c0:c0 + fchunk])
        if n < len(order_after):
            first = jnp.concatenate([g[0:8, 0:LANES] + order_after[n], g[0:8, LANES:]], axis=1)
            g = jnp.concatenate([first, g[8:]], axis=0)
        u = _dot(h, wu_ref[:, c0:c0 + fchunk])
        a = (_silu(g) * u).astype(BF16)
        d = _dot(a, wd_ref[c0:c0 + fchunk, :])
        acc = d if acc is None else acc + d
    return acc


def _ffn1_proj_kernel(x_ref, n1_ref, wg_ref, wu_ref, wd_ref, nm_ref, win_ref, qw_ref, kw_ref,
                      cw_ref, alog_ref, dtb_ref,
                      x1_ref, z_ref, qkvb_ref, q_ref, k_ref, v_ref, gcol_ref, grow_ref, pad_ref, tail_ref,
                      *, fchunk, tiles_per_seq, num_tiles):
    i = pl.program_id(0)
    tm = x_ref.shape[0]
    heads_out = (q_ref, k_ref, v_ref)

    @pl.when(i == 0)
    def _():
        pad_ref[...] = jnp.zeros(pad_ref.shape, F32)

    def conv_tail(cb, next_rows):
        tail_ref[cb, 0:8 + CONV_TAIL, :] = pad_ref[cb, tm - CONV_TAIL:8 + tm, :]
        tail_ref[cb, 8 + CONV_TAIL:16 + CONV_TAIL, :] = next_rows
        _gdn_conv_head(tail_ref.at[cb], cw_ref, cb, heads_out[cb // GDN_HEADS], tm - CONV_TAIL, CONV_TAIL)

    @pl.when(i == num_tiles)
    def _():
        for cb in range(3 * GDN_HEADS):
            _gdn_conv_head(pad_ref.at[cb], cw_ref, cb, heads_out[cb // GDN_HEADS], 0, tm - CONV_TAIL)
            conv_tail(cb, jnp.zeros((8, LANES), F32))

    @pl.when(i < num_tiles)
    def _():
        _ffn1_proj_tile(i, x_ref, n1_ref, wg_ref, wu_ref, wd_ref, nm_ref, win_ref, qw_ref, kw_ref,
                        cw_ref, alog_ref, dtb_ref, x1_ref, z_ref, qkvb_ref, heads_out, gcol_ref, grow_ref,
                        pad_ref, conv_tail, fchunk, tiles_per_seq)


def _ffn1_proj_tile(i, x_ref, n1_ref, wg_ref, wu_ref, wd_ref, nm_ref, win_ref, qw_ref, kw_ref,
                    cw_ref, alog_ref, dtb_ref, x1_ref, z_ref, qkvb_ref, heads_out, gcol_ref, grow_ref,
                    pad_ref, conv_tail, fchunk, tiles_per_seq):
    tm = x_ref.shape[0]
    conv_done = []
    for cb in range(3 * GDN_HEADS):
        even, odd = _gdn_conv_head(pad_ref.at[cb], cw_ref, cb, heads_out[cb // GDN_HEADS], 0, tm - CONV_TAIL)
        conv_done.append(_zero_after(even + odd))
    nchunks = wg_ref.shape[1] // fchunk
    order_after = conv_done[:nchunks - 1] + [functools.reduce(lambda a, b: a + b, conv_done[nchunks - 1:])]

    seq_start = lax.rem(i, tiles_per_seq) == 0
    x = x_ref[...]
    h = _rms(x, n1_ref[...]).astype(BF16)
    x1 = x + 0.5 * _swiglu(h, wg_ref, wu_ref, wd_ref, fchunk, order_after)
    x1_ref[...] = x1
    h2 = _rms(x1, nm_ref[...]).astype(BF16)
    c = 3 * GDN_WIDTH
    qkva = _dot(h2, win_ref[:, :c])
    z_ref[...] = _dot(h2, win_ref[:, c:c + GDN_WIDTH])
    c += GDN_WIDTH
    _gdn_gates(_dot(h2, win_ref[:, c:c + LANES]), alog_ref[...], dtb_ref[...], gcol_ref, grow_ref)
    c += LANES

    for cb in range(3 * GDN_HEADS):
        cs = slice(cb * LANES, (cb + 1) * LANES)
        conv_tail(cb, jnp.where(seq_start, 0.0, qkva[0:8, cs]))
        pad_ref[cb, 0:8, :] = jnp.where(seq_start, 0.0, pad_ref[cb, tm:tm + 8, :])
        pad_ref[cb, 8:8 + tm, :] = qkva[:, cs]

    low = lax.broadcasted_iota(jnp.int32, (1, LANES), 1) < SWA_HEAD_DIM

    def head_rms(y, w):
        y2 = y * y
        s0 = jnp.sum(jnp.where(low, y2, 0.0), axis=-1, keepdims=True)
        s1 = jnp.sum(jnp.where(low, 0.0, y2), axis=-1, keepdims=True)
        ms = jnp.where(low, s0, s1) * (1.0 / SWA_HEAD_DIM)
        return y * lax.rsqrt(ms + EPS) * w

    qscale = qw_ref[...] * (SWA_HEAD_DIM ** -0.5 * LOG2E)
    wide = 2 * LANES
    for j in range(3 * SWA_WIDTH // wide):
        y = _dot(h2, win_ref[:, c + j * wide:c + (j + 1) * wide])
        for half in range(2):
            p = 2 * j + half
            yp = y[:, half * LANES:(half + 1) * LANES]
            if p < SWA_WIDTH // LANES:
                yp = head_rms(yp, qscale)
            elif p < 2 * SWA_WIDTH // LANES:
                yp = head_rms(yp, kw_ref[...])
            qkvb_ref[:, p * LANES:(p + 1) * LANES] = yp


def _ffn1_proj(x2d, n1, wg, wu, wd, nm, win, qw, kw, cw, alog, dtb, *, batch, tm, fchunk):
    n, d = x2d.shape
    f = wg.shape[1]
    seq = n // batch
    assert win.shape[1] == 4 * GDN_WIDTH + LANES + 3 * SWA_WIDTH and seq % tm == 0
    nt = n // tm
    tps = seq // tm
    cur = lambda i: jnp.minimum(i, nt - 1)
    late = lambda i: jnp.maximum(i - 1, 0)
    row = lambda w: pl.BlockSpec((tm, w), lambda i: (cur(i), 0))
    heads = pl.BlockSpec((1, GDN_HEADS, tm, GDN_HEAD_DIM), lambda i: (late(i) // tps, 0, late(i) % tps, 0))
    f32 = lambda *shape: jax.ShapeDtypeStruct(shape, F32)
    return pl.pallas_call(
        functools.partial(_ffn1_proj_kernel, fchunk=fchunk, tiles_per_seq=tps, num_tiles=nt),
        grid=(nt + 1,),
        in_specs=[row(d), _resident((1, d)), _resident((d, f)), _resident((d, f)), _resident((f, d)),
                  _resident((1, d)), _resident(win.shape), _resident((1, LANES)), _resident((1, LANES)),
                  _resident(cw.shape), _resident((1, LANES)), _resident((1, LANES))],
        out_specs=[row(d), row(GDN_WIDTH), row(3 * SWA_WIDTH), heads, heads, heads, row(LANES),
                   pl.BlockSpec((tm // CHUNK, 8, CHUNK), lambda i: (cur(i), 0, 0))],
        out_shape=[f32(n, d), f32(n, GDN_WIDTH), f32(n, 3 * SWA_WIDTH)]
                  + [f32(batch, GDN_HEADS, seq, GDN_HEAD_DIM)] * 3 + [f32(n, LANES), f32(n // CHUNK, 8, CHUNK)],
        scratch_shapes=[pltpu.VMEM((3 * GDN_HEADS, tm + 8, LANES), F32),
                        pltpu.VMEM((3 * GDN_HEADS, CONV_TAIL + 16, LANES), F32)],
        compiler_params=_cparams(("arbitrary",)),
        name="ffn1_proj",
    )(x2d, n1, wg, wu, wd, nm, win, qw, kw, cw, alog, dtb)


def _out_ffn2_kernel(x1_ref, of_ref, ob_ref, z_ref, attn_ref, gnw_ref, wout_ref, n2_ref,
                     wg_ref, wu_ref, wd_ref, nf_ref, out_ref, *, fchunk):
    o = of_ref[0] + ob_ref[0]
    z = z_ref[...]
    gnw = gnw_ref[...]
    heads = []
    for h in range(GDN_HEADS):
        sl = slice(h * GDN_HEAD_DIM, (h + 1) * GDN_HEAD_DIM)
        heads.append(_rms(o[:, sl], gnw) * _silu(z[:, sl]))
    oa = jnp.concatenate(heads, axis=1).astype(BF16)
    mix = _dot(oa, wout_ref[:GDN_WIDTH, :]) + _dot(attn_ref[...].astype(BF16), wout_ref[GDN_WIDTH:, :])
    x2 = x1_ref[...] + mix
    h2 = _rms(x2, n2_ref[...]).astype(BF16)
    x3 = x2 + 0.5 * _swiglu(h2, wg_ref, wu_ref, wd_ref, fchunk)
    out_ref[...] = _rms(x3, nf_ref[...])


def _out_ffn2(x1, o_dirs, z, attn, gnw, wout, n2, wg, wu, wd, nf, *, tm, fchunk):
    n, d = x1.shape
    f = wg.shape[1]
    row = lambda w: pl.BlockSpec((tm, w), lambda i: (i, 0))
    direction = lambda k: pl.BlockSpec((1, tm, GDN_WIDTH), lambda i: (k, i, 0))
    return pl.pallas_call(
        functools.partial(_out_ffn2_kernel, fchunk=fchunk),
        grid=(n // tm,),
        in_specs=[row(d), direction(0), direction(1), row(GDN_WIDTH), row(SWA_WIDTH),
                  _resident((1, GDN_HEAD_DIM)), _resident(wout.shape), _resident((1, d)),
                  _resident((d, f)), _resident((d, f)), _resident((f, d)), _resident((1, d))],
        out_specs=row(d),
        out_shape=jax.ShapeDtypeStruct((n, d), F32),
        compiler_params=_cparams(("arbitrary",)),
        name="out_ffn2",
    )(x1, o_dirs, o_dirs, z, attn, gnw, wout, n2, wg, wu, wd, nf)


CONV_TAIL = 16


def _gdn_conv_head(src, cw_ref, cb, dst, out0, count):
    cs = slice(cb * LANES, (cb + 1) * LANES)
    half = count // 2
    done = []
    for par in range(2):
        acc = None
        for t in range(CONV_WIDTH):
            term = src[pl.ds(6 + t + par, half, stride=2), :] * cw_ref[t:t + 1, cs]
            acc = term if acc is None else acc + term
        y = _silu(acc)
        if cb < 2 * GDN_HEADS:
            y = y * lax.rsqrt(jnp.sum(y * y, axis=-1, keepdims=True) + EPS)
        if cb < GDN_HEADS:
            y = y * (GDN_HEAD_DIM ** -0.5)
        dst[0, cb % GDN_HEADS, pl.ds(out0 + par, half, stride=2), :] = y
        done.append(y)
    return done


def _gdn_gates(ab, alog, dtb, gcol_ref, grow_ref):
    ts = ab.shape[0]
    xs = ab + dtb
    softplus = jnp.maximum(xs, 0.0) + jnp.log(1.0 + jnp.exp(-jnp.abs(xs)))
    g = -jnp.exp(alog) * softplus
    beta = 1.0 / (1.0 + jnp.exp(-ab))
    g1 = g.astype(BF16)
    r1 = g - g1.astype(F32)
    g2 = r1.astype(BF16)
    g3 = (r1 - g2.astype(F32)).astype(BF16)
    ri = lax.broadcasted_iota(jnp.int32, (LANES, LANES), 0)
    ci = lax.broadcasted_iota(jnp.int32, (LANES, LANES), 1)
    same = (ri // CHUNK) == (ci // CHUNK)
    lower = jnp.where(same & (ri >= ci), 1.0, 0.0).astype(BF16)
    upper = jnp.where(same & (ri <= ci), 1.0, 0.0).astype(BF16)
    lane = lax.broadcasted_iota(jnp.int32, (1, LANES), 1)
    for s in range(ts // LANES):
        rs = slice(s * LANES, (s + 1) * LANES)
        pre = _dot(lower, g1[rs]) + _dot(lower, g2[rs]) + _dot(lower, g3[rs])
        suf = _dot(upper, g1[rs]) + _dot(upper, g2[rs]) + _dot(upper, g3[rs])
        gs = g[rs]
        col = jnp.where(lane < 4, pre,
              jnp.where(lane < 8, suf,
              jnp.where(lane < 16, beta[rs],
              jnp.where(lane < 20, suf - gs,
              jnp.where(lane < 24, pre - gs, pre + suf - gs)))))
        gcol_ref[rs, :] = col
        colt = col.T
        grow_ref[2 * s] = colt[0:8, 0:CHUNK]
        grow_ref[2 * s + 1] = colt[0:8, CHUNK:2 * CHUNK]


CHUNKS_PER_STEP = 4

def _gdn_scan_kernel(q_ref, k_ref, v_ref, gcol_ref, grow_ref, o_ref,
                     s_ref, wqg_ref, u_ref, ik_ref, eg_ref, *, nc, nblk):
    dirn = pl.program_id(1)
    j = pl.program_id(2)
    fwd = dirn == 0
    slot_build = lax.rem(j, 2)
    slot_scan = 1 - slot_build

    ri = lax.broadcasted_iota(jnp.int32, (CHUNK, CHUNK), 0)
    ci = lax.broadcasted_iota(jnp.int32, (CHUNK, CHUNK), 1)
    dd = (ri - ci) * (1 - 2 * dirn)
    incl = dd >= 0
    strict = dd > 0
    eye = jnp.where(ri == ci, 1.0, 0.0)
    left = lax.broadcasted_iota(jnp.int32, (CHUNK, 2 * CHUNK), 1) < CHUNK
    nt = (((1,), (1,)), ((), ()))

    nsq = int(math.log2(CHUNK))

    def chunk_matrices(ci):
        probs = [(ci * CHUNKS_PER_STEP + cc, h) for cc in range(CHUNKS_PER_STEP) for h in range(GDN_HEADS)]
        qs, ks, egs, rs, ps, intras = [], [], [], [], [], []
        for c, h in probs:
            r0 = pl.multiple_of(c * CHUNK, CHUNK)
            gates = gcol_ref[0, pl.ds(r0, CHUNK), :]
            rows = grow_ref[0, c]

            def pick(base):
                return jnp.where(fwd, gates[:, base + h:base + h + 1], gates[:, base + 4 + h:base + 5 + h])
            gc, beta, gdec, gl = pick(0), pick(8), pick(16), pick(24)
            grow = jnp.where(fwd, rows[h:h + 1, :], rows[4 + h:5 + h, :])
            qc = q_ref[0, h, pl.ds(r0, CHUNK), :]
            kc = k_ref[0, h, pl.ds(r0, CHUNK), :]
            vc = v_ref[0, h, pl.ds(r0, CHUNK), :]
            dec = jnp.where(incl, jnp.exp(jnp.where(incl, gc - grow, 0.0)), 0.0)
            kb = kc * beta
            qk = lax.dot_general(jnp.concatenate([kb, qc], axis=0).astype(BF16), kc.astype(BF16), nt,
                                 preferred_element_type=F32)
            ps.append(-jnp.where(strict, qk[:CHUNK] * dec, 0.0))
            intras.append(qk[CHUNK:] * dec)
            eg = jnp.exp(gc)
            rs.append(jnp.concatenate([vc * beta, kb * eg], axis=1))
            qs.append(qc * eg)
            ks.append(kc * jnp.exp(gdec))
            egs.append(jnp.exp(gl[0:1, :]))
        yield
        xs = [jnp.concatenate([p, eye], axis=1) for p in ps]
        for t in range(nsq):
            for n in range(len(probs)):
                xb = xs[n].astype(BF16)
                y = _dot(xb[:, :CHUNK], xb)
                xs[n] = jnp.where(left, y, xs[n] + y)
            yield
        for n in range(len(probs)):
            t_off = (xs[n][:, CHUNK:] - eye).astype(BF16)
            rs[n] = rs[n] + _dot(t_off, rs[n].astype(BF16))
        for n, (c, h) in enumerate(probs):
            u_ref[slot_build, c, h] = rs[n][:, :GDN_HEAD_DIM]
            wqg_ref[slot_build, c, h] = jnp.concatenate([rs[n][:, GDN_HEAD_DIM:], qs[n]], axis=0).astype(BF16)
            ik_ref[slot_build, c, h] = jnp.concatenate([intras[n], ks[n].T], axis=0).astype(BF16)
            eg_ref[slot_build, c, h] = jnp.broadcast_to(egs[n], (8, GDN_HEAD_DIM))
        yield

    def scan_chunks(ci):
        for cc in range(CHUNKS_PER_STEP):
            i = ci * CHUNKS_PER_STEP + cc
            c = i + dirn * (nc - 1 - 2 * i)
            r0 = pl.multiple_of(c * CHUNK, CHUNK)
            states = [s_ref[h] for h in range(GDN_HEADS)]
            m1 = [_dot(wqg_ref[slot_scan, c, h], states[h].astype(BF16)) for h in range(GDN_HEADS)]
            yield
            v_new = [(u_ref[slot_scan, c, h] - m1[h][:CHUNK]).astype(BF16) for h in range(GDN_HEADS)]
            m2 = [_dot(ik_ref[slot_scan, c, h], v_new[h]) for h in range(GDN_HEADS)]
            for h in range(GDN_HEADS):
                hs = slice(h * GDN_HEAD_DIM, (h + 1) * GDN_HEAD_DIM)
                o_ref[0, 0, pl.ds(r0, CHUNK), hs] = m1[h][CHUNK:] + m2[h][:CHUNK]
                s_ref[h] = states[h] * eg_ref[slot_scan, c, h][0:1, :] + m2[h][CHUNK:]
            yield

    def run(*makers):
        def body(ci, carry):
            live = [m(ci) for m in makers]
            while live:
                live = [g for g in live if next(g, StopIteration) is not StopIteration]
            return carry
        lax.fori_loop(0, nc // CHUNKS_PER_STEP, body, 0)

    @pl.when(j == 0)
    def _():
        s_ref[...] = jnp.zeros(s_ref.shape, F32)
        run(chunk_matrices)

    @pl.when((j > 0) & (j < nblk))
    def _():
        run(chunk_matrices, scan_chunks)

    @pl.when(j == nblk)
    def _():
        run(scan_chunks)


def _gdn_scan(q, k, v, gcol, grow, *, ts):
    b, nh, s, hd = q.shape
    w = nh * hd
    nblk = s // ts
    nc = ts // CHUNK
    order = lambda d, j: j + d * (nblk - 1 - 2 * j)
    src = lambda d, j: order(d, jnp.minimum(j, nblk - 1))
    dst = lambda d, j: order(d, jnp.maximum(j - 1, 0))
    return pl.pallas_call(
        functools.partial(_gdn_scan_kernel, nc=nc, nblk=nblk),
        grid=(b, 2, nblk + 1),
        in_specs=[pl.BlockSpec((1, nh, ts, hd), lambda i, d, j: (i, 0, src(d, j), 0))] * 3
                 + [pl.BlockSpec((1, ts, LANES), lambda i, d, j: (i, src(d, j), 0)),
                    pl.BlockSpec((1, nc, 8, CHUNK), lambda i, d, j: (i, src(d, j), 0, 0))],
        out_specs=pl.BlockSpec((1, 1, ts, w), lambda i, d, j: (d, i, dst(d, j), 0)),
        out_shape=jax.ShapeDtypeStruct((2, b, s, w), F32),
        scratch_shapes=[pltpu.VMEM((GDN_HEADS, GDN_HEAD_DIM, GDN_HEAD_DIM), F32),
                        pltpu.VMEM((2, nc, GDN_HEADS, 2 * CHUNK, GDN_HEAD_DIM), BF16),
                        pltpu.VMEM((2, nc, GDN_HEADS, CHUNK, GDN_HEAD_DIM), F32),
                        pltpu.VMEM((2, nc, GDN_HEADS, CHUNK + GDN_HEAD_DIM, CHUNK), BF16),
                        pltpu.VMEM((2, nc, GDN_HEADS, 8, GDN_HEAD_DIM), F32)],
        compiler_params=_cparams(("arbitrary", "arbitrary", "arbitrary")),
        name="gdn_scan",
    )(q, k, v, gcol, grow)


SWA_TQ = 128
SWA_TK = SWA_TQ + 2 * BAND_RADIUS
SWA_HALO = BAND_RADIUS * max(DILATIONS)
COPY_ROWS = 512
SWA_TILES_PER_STEP = 4


def _swa_kernel(q_ref, k_ref, v_ref, bias_ref, o_ref,
                kn_ref, vp_ref, m0_ref, m1_ref, l_ref, acc_ref, bias_scr, *, seq, qb):
    qi = pl.program_id(2)
    qn_ref = q_ref.at[0]
    lane = lax.broadcasted_iota(jnp.int32, (1, LANES), 1)
    low = lane < SWA_HEAD_DIM
    nt = (((1,), (1,)), ((), ()))

    @pl.when(qi == 0)
    def _():
        for pi in range(len(DILATIONS)):
            for hh in range(2):
                full = jnp.broadcast_to(bias_ref[0, pi, hh], (SWA_TQ, SWA_TK))
                bias_scr[pi, hh] = pltpu.roll(full, 0, 1, stride=1, stride_axis=0)

        zeros = jnp.zeros((SWA_HALO, LANES), F32)
        for ref in (kn_ref, vp_ref):
            ref[0:SWA_HALO, :] = zeros
            ref[SWA_HALO + seq:2 * SWA_HALO + seq, :] = zeros

        def copy_rows(i, carry):
            r0 = pl.multiple_of(i * COPY_ROWS, COPY_ROWS)
            kn_ref[pl.ds(SWA_HALO + r0, COPY_ROWS), :] = k_ref[0, pl.ds(r0, COPY_ROWS), :]
            vp_ref[pl.ds(SWA_HALO + r0, COPY_ROWS), :] = v_ref[0, pl.ds(r0, COPY_ROWS), :]
            return carry

        lax.fori_loop(0, seq // COPY_ROWS, copy_rows, 0)

    kcol = lax.broadcasted_iota(jnp.int32, (1, SWA_TK), 1)

    def rows(ref, start, size, stride):
        if stride == 1:
            return ref[pl.ds(start, size), :]
        return ref[pl.ds(start, size, stride=stride), :]

    def put(ref, start, size, stride, val):
        if stride == 1:
            ref[pl.ds(start, size), :] = val
        else:
            ref[pl.ds(start, size, stride=stride), :] = val

    def logits(pi, dil, g):
        sub_len = seq // dil
        tiles_per_res = qb // dil // SWA_TQ
        ss = []
        for u in range(SWA_TILES_PER_STEP):
            res, t = divmod(g * SWA_TILES_PER_STEP + u, tiles_per_res)
            tau0 = qi * (qb // dil) + t * SWA_TQ
            loc0 = res + dil * (t * SWA_TQ)
            krow = SWA_HALO + qi * qb + loc0 - dil * BAND_RADIUS
            qt = rows(qn_ref, qi * qb + loc0, SWA_TQ, dil)
            kt = rows(kn_ref, krow, SWA_TK, dil).astype(BF16)
            kidx = kcol + (tau0 - BAND_RADIUS)
            kvalid = (kidx >= 0) & (kidx < sub_len)
            for hh in range(2):
                mine = low if hh == 0 else jnp.logical_not(low)
                qh = jnp.where(mine, qt, 0.0).astype(BF16)
                s = lax.dot_general(qh, kt, nt, preferred_element_type=F32)
                ss.append(jnp.where(kvalid, s + bias_scr[pi, hh], NEG_BIG))
        return ss

    def accumulate(dil, g, ss, first):
        tiles_per_res = qb // dil // SWA_TQ
        loc, vts, old = [], [], []
        for u in range(SWA_TILES_PER_STEP):
            res, t = divmod(g * SWA_TILES_PER_STEP + u, tiles_per_res)
            loc0 = res + dil * (t * SWA_TQ)
            vt = rows(vp_ref, SWA_HALO + qi * qb + loc0 - dil * BAND_RADIUS, SWA_TK, dil)
            vts.append((jnp.where(low, vt, 1.0).astype(BF16), jnp.where(low, 1.0, vt).astype(BF16)))
            loc.append(loc0)
            if not first:
                old.append(tuple(rows(ref, loc0, SWA_TQ, dil) for ref in (m0_ref, m1_ref, l_ref, acc_ref)))
        m_new, alpha, ps = [], [], []
        for n, s in enumerate(ss):
            u, hh = divmod(n, 2)
            mt = jnp.max(s, axis=-1, keepdims=True)
            if first:
                mn = jnp.broadcast_to(mt, (SWA_TQ, LANES))
            else:
                mo = old[u][hh]
                mn = jnp.maximum(mo, mt)
                alpha.append(jnp.exp2(mo - mn))
            m_new.append(mn)
            ps.append(jnp.exp2(s - jnp.concatenate([mn, mn], axis=1)).astype(BF16))
        pv = [_dot(p, vts[n // 2][n % 2]) for n, p in enumerate(ps)]
        for u in range(SWA_TILES_PER_STEP):
            pv0, pv1 = pv[2 * u], pv[2 * u + 1]
            put(m0_ref, loc[u], SWA_TQ, dil, m_new[2 * u])
            put(m1_ref, loc[u], SWA_TQ, dil, m_new[2 * u + 1])
            lsum = jnp.where(low, pv1, pv0)
            pvv = jnp.where(low, pv0, pv1)
            if first:
                put(l_ref, loc[u], SWA_TQ, dil, lsum)
                put(acc_ref, loc[u], SWA_TQ, dil, pvv)
            else:
                a0, a1 = alpha[2 * u], alpha[2 * u + 1]
                put(l_ref, loc[u], SWA_TQ, dil, old[u][2] * jnp.where(low, a1, a0) + lsum)
                put(acc_ref, loc[u], SWA_TQ, dil, old[u][3] * jnp.where(low, a0, a1) + pvv)

    units = [(pi, dil, g) for pi, dil in enumerate(DILATIONS) for g in range(qb // SWA_TQ // SWA_TILES_PER_STEP)]
    pending = logits(*units[0])
    for k, (pi, dil, g) in enumerate(units):
        nxt = logits(*units[k + 1]) if k + 1 < len(units) else None
        accumulate(dil, g, pending, first=pi == 0)
        pending = nxt

    o_ref[0] = acc_ref[...] / pltpu.roll(l_ref[...], SWA_HEAD_DIM, 1)


def _t5_bucket(rel):
    nb = REL_BUCKETS // 2
    bucket = (rel > 0).astype(np.int32) * nb
    n = np.abs(rel)
    max_exact = nb // 2
    large = max_exact + (np.log(np.maximum(n, 1) / max_exact)
                         / math.log(REL_MAX_DISTANCE / max_exact) * (nb - max_exact)).astype(np.int32)
    large = np.minimum(large, nb - 1)
    return (bucket + np.where(n < max_exact, n, large)).astype(np.int32)


def _band_bias(rel_bias):
    rel = np.arange(2 * BAND_RADIUS + 1) - BAND_RADIUS
    rows = []
    for dil in DILATIONS:
        inside = jnp.transpose(rel_bias[_t5_bucket(rel * dil)]) * LOG2E
        rows.append(jnp.pad(inside, ((0, 0), (0, SWA_TK - inside.shape[1])), constant_values=NEG_BIG))
    bias = jnp.stack(rows, axis=0)
    bias = bias.reshape(len(DILATIONS), SWA_HEADS // 2, 2, 1, SWA_TK)
    return jnp.transpose(bias, (1, 0, 2, 3, 4)).astype(F32)


def _swa(qkvb, bias, *, qb):
    b, s, _ = qkvb.shape
    pairs = SWA_HEADS // 2
    assert s % qb == 0 and qb % (max(DILATIONS) * SWA_TQ) == 0
    col = lambda base: pl.BlockSpec((1, s, LANES), lambda i, p, j: (i, 0, base + p))
    padded = s + 2 * SWA_HALO
    return pl.pallas_call(
        functools.partial(_swa_kernel, seq=s, qb=qb),
        grid=(b, pairs, s // qb),
        in_specs=[col(0), col(pairs), col(2 * pairs),
                  pl.BlockSpec((1,) + bias.shape[1:], lambda i, p, j: (p, 0, 0, 0, 0))],
        out_specs=pl.BlockSpec((1, qb, LANES), lambda i, p, j: (i, j, p)),
        out_shape=jax.ShapeDtypeStruct((b, s, SWA_WIDTH), F32),
        scratch_shapes=[pltpu.VMEM((padded, LANES), F32), pltpu.VMEM((padded, LANES), F32)]
                       + [pltpu.VMEM((qb, LANES), F32)] * 4
                       + [pltpu.VMEM((len(DILATIONS), 2, SWA_TQ, SWA_TK), F32)],
        compiler_params=_cparams(("arbitrary", "arbitrary", "arbitrary")),
        name="swa",
    )(qkvb, qkvb, qkvb, bias)


FFN_TM = 512
FFN_FCHUNK = 256
SCAN_TS = 1024
SWA_QB = 2048


def _lane_row(vals, copies_at):
    row = jnp.zeros((LANES,), F32)
    for off in copies_at:
        row = row.at[off:off + vals.shape[0]].set(vals)
    return row[None, :]


def kernel(x, ffn1_norm, ffn1_w_gate, ffn1_w_up, ffn1_w_down, mix_norm, w_in, conv_w, a_log, dt_bias, gdn_norm_w, q_norm_w, k_norm_w, rel_bias, w_out, ffn2_norm, ffn2_w_gate, ffn2_w_up, ffn2_w_down, final_norm):
    b, s, d = x.shape
    n = b * s
    x2d = x.reshape(n, d)
    ng = 2 * GDN_HEADS
    c_z = 3 * GDN_WIDTH
    c_a = c_z + GDN_WIDTH
    c_b = c_a + ng
    c_qkvb = c_b + ng
    bias = _band_bias(rel_bias)
    for l in range(ffn1_norm.shape[0]):
        wl = w_in[l]
        w_a = wl[:, c_a:c_b]
        w_gates = jnp.concatenate([w_a, wl[:, c_b:c_qkvb], w_a, w_a, jnp.zeros((d, LANES - 4 * ng), F32)], axis=1)
        win = jnp.concatenate([wl[:, :c_a], w_gates, wl[:, c_qkvb:]], axis=1).astype(BF16)
        copies = (0, 2 * ng, 3 * ng)
        alog = _lane_row(a_log[l].reshape(ng), copies)
        dtb = _lane_row(dt_bias[l].reshape(ng), copies)
        x1, z, qkvb, q, k, v, gcol, grow = _ffn1_proj(
            x2d, ffn1_norm[l][None, :], ffn1_w_gate[l].astype(BF16), ffn1_w_up[l].astype(BF16),
            ffn1_w_down[l].astype(BF16), mix_norm[l][None, :], win,
            jnp.tile(q_norm_w[l], 2)[None, :], jnp.tile(k_norm_w[l], 2)[None, :],
            jnp.transpose(conv_w[l]), alog, dtb, batch=b, tm=FFN_TM, fchunk=FFN_FCHUNK)
        o_dirs = _gdn_scan(q, k, v, gcol.reshape(b, s, LANES), grow.reshape(b, s // CHUNK, 8, CHUNK), ts=SCAN_TS)
        attn = _swa(qkvb.reshape(b, s, -1), bias, qb=SWA_QB)
        x2d = _out_ffn2(
            x1, o_dirs.reshape(2, n, -1), z, attn.reshape(n, -1),
            gdn_norm_w[l][None, :], w_out[l].astype(BF16), ffn2_norm[l][None, :],
            ffn2_w_gate[l].astype(BF16), ffn2_w_up[l].astype(BF16), ffn2_w_down[l].astype(BF16),
            final_norm[l][None, :], tm=FFN_TM, fchunk=FFN_FCHUNK)
    return x2d.reshape(b, s, d)
```

```python
import functools
import math

import numpy as np
import jax
import jax.numpy as jnp
from jax import lax
from jax.experimental import pallas as pl
from jax.experimental.pallas import tpu as pltpu

F32 = jnp.float32
BF16 = jnp.bfloat16
EPS = 1e-6
NEG_BIG = -1e30
LOG2E = math.log2(math.e)

LANES = 128
GDN_HEADS = 4
GDN_HEAD_DIM = 128
GDN_WIDTH = GDN_HEADS * GDN_HEAD_DIM
CONV_WIDTH = 5
CHUNK = 64
SWA_HEADS = 8
SWA_HEAD_DIM = 64
SWA_WIDTH = SWA_HEADS * SWA_HEAD_DIM
DILATIONS = (1, 4, 16)
BAND_RADIUS = 64
REL_BUCKETS = 32
REL_MAX_DISTANCE = 1024
VMEM_LIMIT = 56 * 1024 * 1024


def _cparams(sem):
    return pltpu.CompilerParams(dimension_semantics=sem, vmem_limit_bytes=VMEM_LIMIT)


def _resident(shape):
    zeros = (0,) * len(shape)
    return pl.BlockSpec(shape, lambda *_: zeros, pipeline_mode=pl.Buffered(1))


def _rms(x, w):
    return x * lax.rsqrt(jnp.mean(x * x, axis=-1, keepdims=True) + EPS) * w


def _silu(x):
    return x * (1.0 / (1.0 + jnp.exp(-x)))


def _dot(a, b):
    return jnp.dot(a, b, preferred_element_type=F32)


def _zero_after(values):
    s = values[0:8]
    for r in range(8, values.shape[0], 8):
        s = s + values[r:r + 8]
    bits = lax.bitcast_convert_type(s, jnp.int32)
    return lax.shift_right_logical(lax.shift_right_logical(bits, 16), 16).astype(F32)


def _swiglu(h, wg_ref, wu_ref, wd_ref, fchunk, order_after=()):
    acc = None
    for n, c0 in enumerate(range(0, wg_ref.shape[1], fchunk)):
        g = _dot(h, wg_ref[:, c0:c0 + fchunk])
        if n < len(order_after):
            first = jnp.concatenate([g[0:8, 0:LANES] + order_after[n], g[0:8, LANES:]], axis=1)
            g = jnp.concatenate([first, g[8:]], axis=0)
        u = _dot(h, wu_ref[:, c0:c0 + fchunk])
        a = (_silu(g) * u).astype(BF16)
        d = _dot(a, wd_ref[c0:c0 + fchunk, :])
        acc = d if acc is None else acc + d
    return acc


def _ffn1_proj_kernel(x_ref, n1_ref, wg_ref, wu_ref, wd_ref, nm_ref, win_ref, qw_ref, kw_ref,
                      cw_ref, alog_ref, dtb_ref,
                      x1_ref, z_ref, qkvb_ref, q_ref, k_ref, v_ref, gcol_ref, grow_ref, pad_ref, tail_ref,
                      *, fchunk, tiles_per_seq, num_tiles):
    i = pl.program_id(0)
    tm = x_ref.shape[0]
    heads_out = (q_ref, k_ref, v_ref)

    @pl.when(i == 0)
    def _():
        pad_ref[...] = jnp.zeros(pad_ref.shape, F32)

    def conv_tail(cb, next_rows):
        tail_ref[cb, 0:8 + CONV_TAIL, :] = pad_ref[cb, tm - CONV_TAIL:8 + tm, :]
        tail_ref[cb, 8 + CONV_TAIL:16 + CONV_TAIL, :] = next_rows
        _gdn_conv_head(tail_ref.at[cb], cw_ref, cb, heads_out[cb // GDN_HEADS], tm - CONV_TAIL, CONV_TAIL)

    @pl.when(i == num_tiles)
    def _():
        for cb in range(3 * GDN_HEADS):
            _gdn_conv_head(pad_ref.at[cb], cw_ref, cb, heads_out[cb // GDN_HEADS], 0, tm - CONV_TAIL)
            conv_tail(cb, jnp.zeros((8, LANES), F32))

    @pl.when(i < num_tiles)
    def _():
        _ffn1_proj_tile(i, x_ref, n1_ref, wg_ref, wu_ref, wd_ref, nm_ref, win_ref, qw_ref, kw_ref,
                        cw_ref, alog_ref, dtb_ref, x1_ref, z_ref, qkvb_ref, heads_out, gcol_ref, grow_ref,
                        pad_ref, conv_tail, fchunk, tiles_per_seq)


def _ffn1_proj_tile(i, x_ref, n1_ref, wg_ref, wu_ref, wd_ref, nm_ref, win_ref, qw_ref, kw_ref,
                    cw_ref, alog_ref, dtb_ref, x1_ref, z_ref, qkvb_ref, heads_out, gcol_ref, grow_ref,
                    pad_ref, conv_tail, fchunk, tiles_per_seq):
    tm = x_ref.shape[0]
    conv_done = []
    for cb in range(3 * GDN_HEADS):
        even, odd = _gdn_conv_head(pad_ref.at[cb], cw_ref, cb, heads_out[cb // GDN_HEADS], 0, tm - CONV_TAIL)
        conv_done.append(_zero_after(even + odd))
    nchunks = wg_ref.shape[1] // fchunk
    order_after = conv_done[:nchunks - 1] + [functools.reduce(lambda a, b: a + b, conv_done[nchunks - 1:])]

    seq_start = lax.rem(i, tiles_per_seq) == 0
    x = x_ref[...]
    h = _rms(x, n1_ref[...]).astype(BF16)
    x1 = x + 0.5 * _swiglu(h, wg_ref, wu_ref, wd_ref, fchunk, order_after)
    x1_ref[...] = x1
    h2 = _rms(x1, nm_ref[...]).astype(BF16)
    c = 3 * GDN_WIDTH
    qkva = _dot(h2, win_ref[:, :c])
    z_ref[...] = _dot(h2, win_ref[:, c:c + GDN_WIDTH])
    c += GDN_WIDTH
    _gdn_gates(_dot(h2, win_ref[:, c:c + LANES]), alog_ref[...], dtb_ref[...], gcol_ref, grow_ref)
    c += LANES

    for cb in range(3 * GDN_HEADS):
        cs = slice(cb * LANES, (cb + 1) * LANES)
        conv_tail(cb, jnp.where(seq_start, 0.0, qkva[0:8, cs]))
        pad_ref[cb, 0:8, :] = jnp.where(seq_start, 0.0, pad_ref[cb, tm:tm + 8, :])
        pad_ref[cb, 8:8 + tm, :] = qkva[:, cs]

    low = lax.broadcasted_iota(jnp.int32, (1, LANES), 1) < SWA_HEAD_DIM

    def head_rms(y, w):
        y2 = y * y
        s0 = jnp.sum(jnp.where(low, y2, 0.0), axis=-1, keepdims=True)
        s1 = jnp.sum(jnp.where(low, 0.0, y2), axis=-1, keepdims=True)
        ms = jnp.where(low, s0, s1) * (1.0 / SWA_HEAD_DIM)
        return y * lax.rsqrt(ms + EPS) * w

    qscale = qw_ref[...] * (SWA_HEAD_DIM ** -0.5 * LOG2E)
    wide = 2 * LANES
    for j in range(3 * SWA_WIDTH // wide):
        y = _dot(h2, win_ref[:, c + j * wide:c + (j + 1) * wide])
        for half in range(2):
            p = 2 * j + half
            yp = y[:, half * LANES:(half + 1) * LANES]
            if p < SWA_WIDTH // LANES:
                yp = head_rms(yp, qscale)
            elif p < 2 * SWA_WIDTH // LANES:
                yp = head_rms(yp, kw_ref[...])
            qkvb_ref[:, p * LANES:(p + 1) * LANES] = yp


def _ffn1_proj(x2d, n1, wg, wu, wd, nm, win, qw, kw, cw, alog, dtb, *, batch, tm, fchunk):
    n, d = x2d.shape
    f = wg.shape[1]
    seq = n // batch
    assert win.shape[1] == 4 * GDN_WIDTH + LANES + 3 * SWA_WIDTH and seq % tm == 0
    nt = n // tm
    tps = seq // tm
    cur = lambda i: jnp.minimum(i, nt - 1)
    late = lambda i: jnp.maximum(i - 1, 0)
    row = lambda w: pl.BlockSpec((tm, w), lambda i: (cur(i), 0))
    heads = pl.BlockSpec((1, GDN_HEADS, tm, GDN_HEAD_DIM), lambda i: (late(i) // tps, 0, late(i) % tps, 0))
    f32 = lambda *shape: jax.ShapeDtypeStruct(shape, F32)
    return pl.pallas_call(
        functools.partial(_ffn1_proj_kernel, fchunk=fchunk, tiles_per_seq=tps, num_tiles=nt),
        grid=(nt + 1,),
        in_specs=[row(d), _resident((1, d)), _resident((d, f)), _resident((d, f)), _resident((f, d)),
                  _resident((1, d)), _resident(win.shape), _resident((1, LANES)), _resident((1, LANES)),
                  _resident(cw.shape), _resident((1, LANES)), _resident((1, LANES))],
        out_specs=[row(d), row(GDN_WIDTH), row(3 * SWA_WIDTH), heads, heads, heads, row(LANES),
                   pl.BlockSpec((tm // CHUNK, 8, 2 * CHUNK), lambda i: (cur(i), 0, 0))],
        out_shape=[f32(n, d), f32(n, GDN_WIDTH), f32(n, 3 * SWA_WIDTH)]
                  + [f32(batch, GDN_HEADS, seq, GDN_HEAD_DIM)] * 3 + [f32(n, LANES), f32(n // CHUNK, 8, 2 * CHUNK)],
        scratch_shapes=[pltpu.VMEM((3 * GDN_HEADS, tm + 8, LANES), F32),
                        pltpu.VMEM((3 * GDN_HEADS, CONV_TAIL + 16, LANES), F32)],
        compiler_params=_cparams(("arbitrary",)),
        name="ffn1_proj",
    )(x2d, n1, wg, wu, wd, nm, win, qw, kw, cw, alog, dtb)


def _out_ffn2_kernel(x1_ref, of_ref, ob_ref, z_ref, attn_ref, gnw_ref, wout_ref, n2_ref,
                     wg_ref, wu_ref, wd_ref, nf_ref, out_ref, *, fchunk):
    o = of_ref[0] + ob_ref[0]
    z = z_ref[...]
    gnw = gnw_ref[...]
    heads = []
    for h in range(GDN_HEADS):
        sl = slice(h * GDN_HEAD_DIM, (h + 1) * GDN_HEAD_DIM)
        heads.append(_rms(o[:, sl], gnw) * _silu(z[:, sl]))
    oa = jnp.concatenate(heads, axis=1).astype(BF16)
    mix = _dot(oa, wout_ref[:GDN_WIDTH, :]) + _dot(attn_ref[...].astype(BF16), wout_ref[GDN_WIDTH:, :])
    x2 = x1_ref[...] + mix
    h2 = _rms(x2, n2_ref[...]).astype(BF16)
    x3 = x2 + 0.5 * _swiglu(h2, wg_ref, wu_ref, wd_ref, fchunk)
    out_ref[...] = _rms(x3, nf_ref[...])


def _out_ffn2(x1, o_dirs, z, attn, gnw, wout, n2, wg, wu, wd, nf, *, tm, fchunk):
    n, d = x1.shape
    f = wg.shape[1]
    row = lambda w: pl.BlockSpec((tm, w), lambda i: (i, 0))
    direction = lambda k: pl.BlockSpec((1, tm, GDN_WIDTH), lambda i: (k, i, 0))
    return pl.pallas_call(
        functools.partial(_out_ffn2_kernel, fchunk=fchunk),
        grid=(n // tm,),
        in_specs=[row(d), direction(0), direction(1), row(GDN_WIDTH), row(SWA_WIDTH),
                  _resident((1, GDN_HEAD_DIM)), _resident(wout.shape), _resident((1, d)),
                  _resident((d, f)), _resident((d, f)), _resident((f, d)), _resident((1, d))],
        out_specs=row(d),
        out_shape=jax.ShapeDtypeStruct((n, d), F32),
        compiler_params=_cparams(("arbitrary",)),
        name="out_ffn2",
    )(x1, o_dirs, o_dirs, z, attn, gnw, wout, n2, wg, wu, wd, nf)


CONV_TAIL = 16


def _gdn_conv_head(src, cw_ref, cb, dst, out0, count):
    cs = slice(cb * LANES, (cb + 1) * LANES)
    half = count // 2
    done = []
    for par in range(2):
        acc = None
        for t in range(CONV_WIDTH):
            term = src[pl.ds(6 + t + par, half, stride=2), :] * cw_ref[t:t + 1, cs]
            acc = term if acc is None else acc + term
        y = _silu(acc)
        if cb < 2 * GDN_HEADS:
            y = y * lax.rsqrt(jnp.sum(y * y, axis=-1, keepdims=True) + EPS)
        if cb < GDN_HEADS:
            y = y * (GDN_HEAD_DIM ** -0.5)
        dst[0, cb % GDN_HEADS, pl.ds(out0 + par, half, stride=2), :] = y
        done.append(y)
    return done


def _gdn_gates(ab, alog, dtb, gcol_ref, grow_ref):
    ts = ab.shape[0]
    xs = ab + dtb
    softplus = jnp.maximum(xs, 0.0) + jnp.log(1.0 + jnp.exp(-jnp.abs(xs)))
    g = -jnp.exp(alog) * softplus
    beta = 1.0 / (1.0 + jnp.exp(-ab))
    g1 = g.astype(BF16)
    r1 = g - g1.astype(F32)
    g2 = r1.astype(BF16)
    g3 = (r1 - g2.astype(F32)).astype(BF16)
    ri = lax.broadcasted_iota(jnp.int32, (LANES, LANES), 0)
    ci = lax.broadcasted_iota(jnp.int32, (LANES, LANES), 1)
    same = (ri // CHUNK) == (ci // CHUNK)
    lower = jnp.where(same & (ri >= ci), 1.0, 0.0).astype(BF16)
    upper = jnp.where(same & (ri <= ci), 1.0, 0.0).astype(BF16)
    lane = lax.broadcasted_iota(jnp.int32, (1, LANES), 1)
    for s in range(ts // LANES):
        rs = slice(s * LANES, (s + 1) * LANES)
        pre = _dot(lower, g1[rs]) + _dot(lower, g2[rs]) + _dot(lower, g3[rs])
        suf = _dot(upper, g1[rs]) + _dot(upper, g2[rs]) + _dot(upper, g3[rs])
        gs = g[rs]
        col = jnp.where(lane < 4, pre,
              jnp.where(lane < 8, suf,
              jnp.where(lane < 16, beta[rs],
              jnp.where(lane < 20, suf - gs,
              jnp.where(lane < 24, pre - gs, pre + suf - gs)))))
        gcol_ref[rs, :] = col
        rows = col.T[0:8]
        swapped = pltpu.roll(rows, CHUNK, 1)
        grow_ref[2 * s] = jnp.where(lane < CHUNK, rows, swapped)
        grow_ref[2 * s + 1] = jnp.where(lane < CHUNK, swapped, rows)


CHUNKS_PER_STEP = 4

def _gdn_scan_kernel(q_ref, k_ref, v_ref, gcol_ref, grow_ref, o_ref,
                     s_ref, wqg_ref, u_ref, ik_ref, eg_ref, *, nc, nblk):
    dirn = pl.program_id(1)
    j = pl.program_id(2)
    fwd = dirn == 0
    slot_build = lax.rem(j, 2)
    slot_scan = 1 - slot_build

    ri = lax.broadcasted_iota(jnp.int32, (CHUNK, 2 * CHUNK), 0)
    li = lax.broadcasted_iota(jnp.int32, (CHUNK, 2 * CHUNK), 1)
    left = li < CHUNK
    dd = (ri - jnp.where(left, li, li - CHUNK)) * (1 - 2 * dirn)
    incl = dd >= 0
    strict = dd > 0
    eye_lo = jnp.where(li == ri, 1.0, 0.0)
    eye_hi = jnp.where(li == ri + CHUNK, 1.0, 0.0)
    nt = (((1,), (1,)), ((), ()))
    zc = jnp.zeros((CHUNK, GDN_HEAD_DIM), BF16)
    zs = jnp.zeros((GDN_HEAD_DIM, GDN_HEAD_DIM), BF16)
    npairs = GDN_HEADS // 2

    def block_diag(a, b, zero):
        return jnp.concatenate([jnp.concatenate([a, zero], axis=1), jnp.concatenate([zero, b], axis=1)], axis=0)

    nsq = int(math.log2(CHUNK))

    def chunk_matrices(ci):
        probs = [(ci * CHUNKS_PER_STEP + cc, p) for cc in range(CHUNKS_PER_STEP) for p in range(npairs)]
        qgs, kds, egs, rs, xa, xb, intras = [], [], [], [], [], [], []
        for c, p in probs:
            r0 = pl.multiple_of(c * CHUNK, CHUNK)
            gates = gcol_ref[0, pl.ds(r0, CHUNK), :]
            rows = grow_ref[0, c]
            per_head = []
            for h in (2 * p, 2 * p + 1):
                def pick(base):
                    return jnp.where(fwd, gates[:, base + h:base + h + 1], gates[:, base + 4 + h:base + 5 + h])
                gc, beta, gdec, gl = pick(0), pick(8), pick(16), pick(24)
                grow = jnp.where(fwd, rows[h:h + 1, :], rows[4 + h:5 + h, :])
                qc = q_ref[0, h, pl.ds(r0, CHUNK), :]
                kc = k_ref[0, h, pl.ds(r0, CHUNK), :]
                vc = v_ref[0, h, pl.ds(r0, CHUNK), :]
                kb = kc * beta
                eg = jnp.exp(gc)
                per_head.append(dict(gc=gc, grow=grow, q=qc, k=kc, kb=kb, r=jnp.concatenate([vc * beta, kb * eg], axis=1),
                                     qg=qc * eg, kd=kc * jnp.exp(gdec), eg=jnp.exp(gl[0:1, :])))
            a, b = per_head
            dec = jnp.where(incl, jnp.exp(jnp.where(incl, jnp.where(left, a["gc"], b["gc"])
                                                    - jnp.where(left[0:1], a["grow"], b["grow"]), 0.0)), 0.0)
            lhs = jnp.concatenate([jnp.concatenate([a["kb"], b["kb"]], axis=1),
                                   jnp.concatenate([a["q"], b["q"]], axis=1)], axis=0).astype(BF16)
            qk = lax.dot_general(lhs, block_diag(a["k"].astype(BF16), b["k"].astype(BF16), zc), nt,
                                 preferred_element_type=F32)
            neg_a = -jnp.where(strict, qk[:CHUNK] * dec, 0.0)
            xa.append(jnp.where(left, neg_a, eye_hi))
            xb.append(jnp.where(left, eye_lo, neg_a))
            intras.append(qk[CHUNK:] * dec)
            rs.append((a["r"], b["r"]))
            qgs.append((a["qg"], b["qg"]))
            kds.append(jnp.concatenate([a["kd"], b["kd"]], axis=0))
            egs.append(jnp.concatenate([jnp.broadcast_to(a["eg"], (8, GDN_HEAD_DIM)),
                                        jnp.broadcast_to(b["eg"], (8, GDN_HEAD_DIM))], axis=1))
        yield
        for t in range(nsq):
            for n in range(len(probs)):
                ya = xa[n].astype(BF16)
                yb = xb[n].astype(BF16)
                zz = jnp.zeros((CHUNK, 2 * CHUNK), BF16)
                y = _dot(jnp.where(left, ya, yb), block_diag(ya, yb, zz))
                xa[n] = jnp.where(left, y[:, :2 * CHUNK], xa[n] + y[:, :2 * CHUNK])
                xb[n] = jnp.where(left, xb[n] + y[:, 2 * CHUNK:], y[:, 2 * CHUNK:])
            yield
        for n, (c, p) in enumerate(probs):
            t_off = jnp.where(left, xb[n] - eye_lo, xa[n] - eye_hi).astype(BF16)
            ra, rb = rs[n]
            zr = jnp.zeros((CHUNK, 2 * GDN_HEAD_DIM), BF16)
            out = _dot(t_off, block_diag(rb.astype(BF16), ra.astype(BF16), zr))
            rb = rb + out[:, :2 * GDN_HEAD_DIM]
            ra = ra + out[:, 2 * GDN_HEAD_DIM:]
            u_ref[slot_build, c, p] = jnp.concatenate([ra[:, :GDN_HEAD_DIM], rb[:, :GDN_HEAD_DIM]], axis=1)
            wqg_ref[slot_build, c, p] = jnp.concatenate(
                [jnp.concatenate([ra[:, GDN_HEAD_DIM:], rb[:, GDN_HEAD_DIM:]], axis=1),
                 jnp.concatenate(qgs[n], axis=1)], axis=0).astype(BF16)
            ik_ref[slot_build, c, p] = jnp.concatenate([intras[n], kds[n].T], axis=0).astype(BF16)
            eg_ref[slot_build, c, p] = egs[n]
        yield

    def scan_chunks(ci):
        for cc in range(CHUNKS_PER_STEP):
            i = ci * CHUNKS_PER_STEP + cc
            c = i + dirn * (nc - 1 - 2 * i)
            r0 = pl.multiple_of(c * CHUNK, CHUNK)
            states = [s_ref[p] for p in range(npairs)]
            m1 = [_dot(wqg_ref[slot_scan, c, p],
                       block_diag(states[p][:, :GDN_HEAD_DIM].astype(BF16), states[p][:, GDN_HEAD_DIM:].astype(BF16), zs))
                  for p in range(npairs)]
            yield
            v_new = [(u_ref[slot_scan, c, p] - m1[p][:CHUNK]).astype(BF16) for p in range(npairs)]
            m2 = [_dot(ik_ref[slot_scan, c, p], block_diag(v_new[p][:, :GDN_HEAD_DIM], v_new[p][:, GDN_HEAD_DIM:], zc))
                  for p in range(npairs)]
            for p in range(npairs):
                ps = slice(2 * p * GDN_HEAD_DIM, 2 * (p + 1) * GDN_HEAD_DIM)
                o_ref[0, 0, pl.ds(r0, CHUNK), ps] = m1[p][CHUNK:] + m2[p][:CHUNK]
                s_ref[p] = states[p] * eg_ref[slot_scan, c, p][0:1, :] + m2[p][CHUNK:]
            yield

    def run(*makers):
        def body(ci, carry):
            live = [m(ci) for m in makers]
            while live:
                live = [g for g in live if next(g, StopIteration) is not StopIteration]
            return carry
        lax.fori_loop(0, nc // CHUNKS_PER_STEP, body, 0)

    @pl.when(j == 0)
    def _():
        s_ref[...] = jnp.zeros(s_ref.shape, F32)
        run(chunk_matrices)

    @pl.when((j > 0) & (j < nblk))
    def _():
        run(chunk_matrices, scan_chunks)

    @pl.when(j == nblk)
    def _():
        run(scan_chunks)


def _gdn_scan(q, k, v, gcol, grow, *, ts):
    b, nh, s, hd = q.shape
    w = nh * hd
    nblk = s // ts
    nc = ts // CHUNK
    order = lambda d, j: j + d * (nblk - 1 - 2 * j)
    src = lambda d, j: order(d, jnp.minimum(j, nblk - 1))
    dst = lambda d, j: order(d, jnp.maximum(j - 1, 0))
    return pl.pallas_call(
        functools.partial(_gdn_scan_kernel, nc=nc, nblk=nblk),
        grid=(b, 2, nblk + 1),
        in_specs=[pl.BlockSpec((1, nh, ts, hd), lambda i, d, j: (i, 0, src(d, j), 0))] * 3
                 + [pl.BlockSpec((1, ts, LANES), lambda i, d, j: (i, src(d, j), 0)),
                    pl.BlockSpec((1, nc, 8, 2 * CHUNK), lambda i, d, j: (i, src(d, j), 0, 0))],
        out_specs=pl.BlockSpec((1, 1, ts, w), lambda i, d, j: (d, i, dst(d, j), 0)),
        out_shape=jax.ShapeDtypeStruct((2, b, s, w), F32),
        scratch_shapes=[pltpu.VMEM((nh // 2, hd, 2 * hd), F32),
                        pltpu.VMEM((2, nc, nh // 2, 2 * CHUNK, 2 * hd), BF16),
                        pltpu.VMEM((2, nc, nh // 2, CHUNK, 2 * hd), F32),
                        pltpu.VMEM((2, nc, nh // 2, CHUNK + hd, 2 * CHUNK), BF16),
                        pltpu.VMEM((2, nc, nh // 2, 8, 2 * hd), F32)],
        compiler_params=_cparams(("arbitrary", "arbitrary", "arbitrary")),
        name="gdn_scan",
    )(q, k, v, gcol, grow)


SWA_TQ = 128
SWA_TK = SWA_TQ + 2 * BAND_RADIUS
SWA_HALO = BAND_RADIUS * max(DILATIONS)
COPY_ROWS = 512
SWA_TILES_PER_STEP = 4


def _swa_kernel(q_ref, k_ref, v_ref, bias_ref, o_ref,
                kn_ref, vp_ref, m0_ref, m1_ref, l_ref, acc_ref, bias_scr, *, seq, qb):
    qi = pl.program_id(2)
    qn_ref = q_ref.at[0]
    lane = lax.broadcasted_iota(jnp.int32, (1, LANES), 1)
    low = lane < SWA_HEAD_DIM
    nt = (((1,), (1,)), ((), ()))

    @pl.when(qi == 0)
    def _():
        for pi in range(len(DILATIONS)):
            for hh in range(2):
                full = jnp.broadcast_to(bias_ref[0, pi, hh], (SWA_TQ, SWA_TK))
                bias_scr[pi, hh] = pltpu.roll(full, 0, 1, stride=1, stride_axis=0)

        zeros = jnp.zeros((SWA_HALO, LANES), F32)
        for ref in (kn_ref, vp_ref):
            ref[0:SWA_HALO, :] = zeros
            ref[SWA_HALO + seq:2 * SWA_HALO + seq, :] = zeros

        def copy_rows(i, carry):
            r0 = pl.multiple_of(i * COPY_ROWS, COPY_ROWS)
            kn_ref[pl.ds(SWA_HALO + r0, COPY_ROWS), :] = k_ref[0, pl.ds(r0, COPY_ROWS), :]
            vp_ref[pl.ds(SWA_HALO + r0, COPY_ROWS), :] = v_ref[0, pl.ds(r0, COPY_ROWS), :]
            return carry

        lax.fori_loop(0, seq // COPY_ROWS, copy_rows, 0)

    kcol = lax.broadcasted_iota(jnp.int32, (1, SWA_TK), 1)

    def rows(ref, start, size, stride):
        if stride == 1:
            return ref[pl.ds(start, size), :]
        return ref[pl.ds(start, size, stride=stride), :]

    def put(ref, start, size, stride, val):
        if stride == 1:
            ref[pl.ds(start, size), :] = val
        else:
            ref[pl.ds(start, size, stride=stride), :] = val

    def logits(pi, dil, g):
        sub_len = seq // dil
        tiles_per_res = qb // dil // SWA_TQ
        ss = []
        for u in range(SWA_TILES_PER_STEP):
            res, t = divmod(g * SWA_TILES_PER_STEP + u, tiles_per_res)
            tau0 = qi * (qb // dil) + t * SWA_TQ
            loc0 = res + dil * (t * SWA_TQ)
            krow = SWA_HALO + qi * qb + loc0 - dil * BAND_RADIUS
            qt = rows(qn_ref, qi * qb + loc0, SWA_TQ, dil)
            kt = rows(kn_ref, krow, SWA_TK, dil).astype(BF16)
            kidx = kcol + (tau0 - BAND_RADIUS)
            kvalid = (kidx >= 0) & (kidx < sub_len)
            for hh in range(2):
                mine = low if hh == 0 else jnp.logical_not(low)
                qh = jnp.where(mine, qt, 0.0).astype(BF16)
                s = lax.dot_general(qh, kt, nt, preferred_element_type=F32)
                ss.append(jnp.where(kvalid, s + bias_scr[pi, hh], NEG_BIG))
        return ss

    def accumulate(dil, g, ss, first):
        tiles_per_res = qb // dil // SWA_TQ
        loc, vts, old = [], [], []
        for u in range(SWA_TILES_PER_STEP):
            res, t = divmod(g * SWA_TILES_PER_STEP + u, tiles_per_res)
            loc0 = res + dil * (t * SWA_TQ)
            vt = rows(vp_ref, SWA_HALO + qi * qb + loc0 - dil * BAND_RADIUS, SWA_TK, dil)
            vts.append((jnp.where(low, vt, 1.0).astype(BF16), jnp.where(low, 1.0, vt).astype(BF16)))
            loc.append(loc0)
            if not first:
                old.append(tuple(rows(ref, loc0, SWA_TQ, dil) for ref in (m0_ref, m1_ref, l_ref, acc_ref)))
        m_new, alpha, ps = [], [], []
        for n, s in enumerate(ss):
            u, hh = divmod(n, 2)
            mt = jnp.max(s, axis=-1, keepdims=True)
            if first:
                mn = jnp.broadcast_to(mt, (SWA_TQ, LANES))
            else:
                mo = old[u][hh]
                mn = jnp.maximum(mo, mt)
                alpha.append(jnp.exp2(mo - mn))
            m_new.append(mn)
            ps.append(jnp.exp2(s - jnp.concatenate([mn, mn], axis=1)).astype(BF16))
        pv = [_dot(p, vts[n // 2][n % 2]) for n, p in enumerate(ps)]
        for u in range(SWA_TILES_PER_STEP):
            pv0, pv1 = pv[2 * u], pv[2 * u + 1]
            put(m0_ref, loc[u], SWA_TQ, dil, m_new[2 * u])
            put(m1_ref, loc[u], SWA_TQ, dil, m_new[2 * u + 1])
            lsum = jnp.where(low, pv1, pv0)
            pvv = jnp.where(low, pv0, pv1)
            if first:
                put(l_ref, loc[u], SWA_TQ, dil, lsum)
                put(acc_ref, loc[u], SWA_TQ, dil, pvv)
            else:
                a0, a1 = alpha[2 * u], alpha[2 * u + 1]
                put(l_ref, loc[u], SWA_TQ, dil, old[u][2] * jnp.where(low, a1, a0) + lsum)
                put(acc_ref, loc[u], SWA_TQ, dil, old[u][3] * jnp.where(low, a0, a1) + pvv)

    units = [(pi, dil, g) for pi, dil in enumerate(DILATIONS) for g in range(qb // SWA_TQ // SWA_TILES_PER_STEP)]
    pending = logits(*units[0])
    for k, (pi, dil, g) in enumerate(units):
        nxt = logits(*units[k + 1]) if k + 1 < len(units) else None
        accumulate(dil, g, pending, first=pi == 0)
        pending = nxt

    o_ref[0] = acc_ref[...] / pltpu.roll(l_ref[...], SWA_HEAD_DIM, 1)


def _t5_bucket(rel):
    nb = REL_BUCKETS // 2
    bucket = (rel > 0).astype(np.int32) * nb
    n = np.abs(rel)
    max_exact = nb // 2
    large = max_exact + (np.log(np.maximum(n, 1) / max_exact)
                         / math.log(REL_MAX_DISTANCE / max_exact) * (nb - max_exact)).astype(np.int32)
    large = np.minimum(large, nb - 1)
    return (bucket + np.where(n < max_exact, n, large)).astype(np.int32)


def _band_bias(rel_bias):
    rel = np.arange(2 * BAND_RADIUS + 1) - BAND_RADIUS
    rows = []
    for dil in DILATIONS:
        inside = jnp.transpose(rel_bias[_t5_bucket(rel * dil)]) * LOG2E
        rows.append(jnp.pad(inside, ((0, 0), (0, SWA_TK - inside.shape[1])), constant_values=NEG_BIG))
    bias = jnp.stack(rows, axis=0)
    bias = bias.reshape(len(DILATIONS), SWA_HEADS // 2, 2, 1, SWA_TK)
    return jnp.transpose(bias, (1, 0, 2, 3, 4)).astype(F32)


def _swa(qkvb, bias, *, qb):
    b, s, _ = qkvb.shape
    pairs = SWA_HEADS // 2
    assert s % qb == 0 and qb % (max(DILATIONS) * SWA_TQ) == 0
    col = lambda base: pl.BlockSpec((1, s, LANES), lambda i, p, j: (i, 0, base + p))
    padded = s + 2 * SWA_HALO
    return pl.pallas_call(
        functools.partial(_swa_kernel, seq=s, qb=qb),
        grid=(b, pairs, s // qb),
        in_specs=[col(0), col(pairs), col(2 * pairs),
                  pl.BlockSpec((1,) + bias.shape[1:], lambda i, p, j: (p, 0, 0, 0, 0))],
        out_specs=pl.BlockSpec((1, qb, LANES), lambda i, p, j: (i, j, p)),
        out_shape=jax.ShapeDtypeStruct((b, s, SWA_WIDTH), F32),
        scratch_shapes=[pltpu.VMEM((padded, LANES), F32), pltpu.VMEM((padded, LANES), F32)]
                       + [pltpu.VMEM((qb, LANES), F32)] * 4
                       + [pltpu.VMEM((len(DILATIONS), 2, SWA_TQ, SWA_TK), F32)],
        compiler_params=_cparams(("arbitrary", "arbitrary", "arbitrary")),
        name="swa",
    )(qkvb, qkvb, qkvb, bias)


FFN_TM = 512
FFN_FCHUNK = 256
SCAN_TS = 1024
SWA_QB = 2048


def _lane_row(vals, copies_at):
    row = jnp.zeros((LANES,), F32)
    for off in copies_at:
        row = row.at[off:off + vals.shape[0]].set(vals)
    return row[None, :]


def kernel(x, ffn1_norm, ffn1_w_gate, ffn1_w_up, ffn1_w_down, mix_norm, w_in, conv_w, a_log, dt_bias, gdn_norm_w, q_norm_w, k_norm_w, rel_bias, w_out, ffn2_norm, ffn2_w_gate, ffn2_w_up, ffn2_w_down, final_norm):
    b, s, d = x.shape
    n = b * s
    x2d = x.reshape(n, d)
    ng = 2 * GDN_HEADS
    c_z = 3 * GDN_WIDTH
    c_a = c_z + GDN_WIDTH
    c_b = c_a + ng
    c_qkvb = c_b + ng
    bias = _band_bias(rel_bias)
    for l in range(ffn1_norm.shape[0]):
        wl = w_in[l]
        w_a = wl[:, c_a:c_b]
        w_gates = jnp.concatenate([w_a, wl[:, c_b:c_qkvb], w_a, w_a, jnp.zeros((d, LANES - 4 * ng), F32)], axis=1)
        win = jnp.concatenate([wl[:, :c_a], w_gates, wl[:, c_qkvb:]], axis=1).astype(BF16)
        copies = (0, 2 * ng, 3 * ng)
        alog = _lane_row(a_log[l].reshape(ng), copies)
        dtb = _lane_row(dt_bias[l].reshape(ng), copies)
        x1, z, qkvb, q, k, v, gcol, grow = _ffn1_proj(
            x2d, ffn1_norm[l][None, :], ffn1_w_gate[l].astype(BF16), ffn1_w_up[l].astype(BF16),
            ffn1_w_down[l].astype(BF16), mix_norm[l][None, :], win,
            jnp.tile(q_norm_w[l], 2)[None, :], jnp.tile(k_norm_w[l], 2)[None, :],
            jnp.transpose(conv_w[l]), alog, dtb, batch=b, tm=FFN_TM, fchunk=FFN_FCHUNK)
        o_dirs = _gdn_scan(q, k, v, gcol.reshape(b, s, LANES), grow.reshape(b, s // CHUNK, 8, 2 * CHUNK), ts=SCAN_TS)
        attn = _swa(qkvb.reshape(b, s, -1), bias, qb=SWA_QB)
        x2d = _out_ffn2(
            x1, o_dirs.reshape(2, n, -1), z, attn.reshape(n, -1),
            gdn_norm_w[l][None, :], w_out[l].astype(BF16), ffn2_norm[l][None, :],
            ffn2_w_gate[l].astype(BF16), ffn2_w_up[l].astype(BF16), ffn2_w_down[l].astype(BF16),
            final_norm[l][None, :], tm=FFN_TM, fchunk=FFN_FCHUNK)
    return x2d.reshape(b, s, d)
```

```python
import functools
import math

import numpy as np
import jax
import jax.numpy as jnp
from jax import lax
from jax.experimental import pallas as pl
from jax.experimental.pallas import tpu as pltpu

F32 = jnp.float32
BF16 = jnp.bfloat16
EPS = 1e-6
NEG_BIG = -1e30
LOG2E = math.log2(math.e)

LANES = 128
GDN_HEADS = 4
GDN_HEAD_DIM = 128
GDN_WIDTH = GDN_HEADS * GDN_HEAD_DIM
CONV_WIDTH = 5
CHUNK = 64
SWA_HEADS = 8
SWA_HEAD_DIM = 64
SWA_WIDTH = SWA_HEADS * SWA_HEAD_DIM
DILATIONS = (1, 4, 16)
BAND_RADIUS = 64
REL_BUCKETS = 32
REL_MAX_DISTANCE = 1024
VMEM_LIMIT = 56 * 1024 * 1024


def _cparams(sem):
    return pltpu.CompilerParams(dimension_semantics=sem, vmem_limit_bytes=VMEM_LIMIT)


def _resident(shape):
    zeros = (0,) * len(shape)
    return pl.BlockSpec(shape, lambda *_: zeros, pipeline_mode=pl.Buffered(1))


def _rms(x, w):
    return x * lax.rsqrt(jnp.mean(x * x, axis=-1, keepdims=True) + EPS) * w


def _silu(x):
    return x * (1.0 / (1.0 + jnp.exp(-x)))


def _dot(a, b):
    return jnp.dot(a, b, preferred_element_type=F32)


def _zero_after(values):
    s = values[0:8]
    for r in range(8, values.shape[0], 8):
        s = s + values[r:r + 8]
    bits = lax.bitcast_convert_type(s, jnp.int32)
    return lax.shift_right_logical(lax.shift_right_logical(bits, 16), 16).astype(F32)


def _swiglu(h, wg_ref, wu_ref, wd_ref, fchunk, order_after=()):
    acc = None
    for n, c0 in enumerate(range(0, wg_ref.shape[1], fchunk)):
        g = _dot(h, wg_ref[:, c0:c0 + fchunk])
        if n < len(order_after):
            first = jnp.concatenate([g[0:8, 0:LANES] + order_after[n], g[0:8, LANES:]], axis=1)
            g = jnp.concatenate([first, g[8:]], axis=0)
        u = _dot(h, wu_ref[:, c0:c0 + fchunk])
        a = (_silu(g) * u).astype(BF16)
        d = _dot(a, wd_ref[c0:c0 + fchunk, :])
        acc = d if acc is None else acc + d
    return acc


def _ffn1_proj_kernel(x_ref, n1_ref, wg_ref, wu_ref, wd_ref, nm_ref, win_ref, qw_ref, kw_ref,
                      cw_ref, alog_ref, dtb_ref,
                      x1_ref, z_ref, qkvb_ref, q_ref, k_ref, v_ref, gcol_ref, grow_ref, pad_ref, tail_ref,
                      *, fchunk, tiles_per_seq, num_tiles):
    i = pl.program_id(0)
    tm = x_ref.shape[0]
    heads_out = (q_ref, k_ref, v_ref)

    @pl.when(i == 0)
    def _():
        pad_ref[...] = jnp.zeros(pad_ref.shape, F32)

    def conv_tail(cb, next_rows):
        tail_ref[cb, 0:8 + CONV_TAIL, :] = pad_ref[cb, tm - CONV_TAIL:8 + tm, :]
        tail_ref[cb, 8 + CONV_TAIL:16 + CONV_TAIL, :] = next_rows
        _gdn_conv_head(tail_ref.at[cb], cw_ref, cb, heads_out[cb // GDN_HEADS], tm - CONV_TAIL, CONV_TAIL)

    @pl.when(i == num_tiles)
    def _():
        for cb in range(3 * GDN_HEADS):
            _gdn_conv_head(pad_ref.at[cb], cw_ref, cb, heads_out[cb // GDN_HEADS], 0, tm - CONV_TAIL)
            conv_tail(cb, jnp.zeros((8, LANES), F32))

    @pl.when(i < num_tiles)
    def _():
        _ffn1_proj_tile(i, x_ref, n1_ref, wg_ref, wu_ref, wd_ref, nm_ref, win_ref, qw_ref, kw_ref,
                        cw_ref, alog_ref, dtb_ref, x1_ref, z_ref, qkvb_ref, heads_out, gcol_ref, grow_ref,
                        pad_ref, conv_tail, fchunk, tiles_per_seq)


def _ffn1_proj_tile(i, x_ref, n1_ref, wg_ref, wu_ref, wd_ref, nm_ref, win_ref, qw_ref, kw_ref,
                    cw_ref, alog_ref, dtb_ref, x1_ref, z_ref, qkvb_ref, heads_out, gcol_ref, grow_ref,
                    pad_ref, conv_tail, fchunk, tiles_per_seq):
    tm = x_ref.shape[0]
    conv_done = []
    for cb in range(3 * GDN_HEADS):
        even, odd = _gdn_conv_head(pad_ref.at[cb], cw_ref, cb, heads_out[cb // GDN_HEADS], 0, tm - CONV_TAIL)
        conv_done.append(_zero_after(even + odd))
    nchunks = wg_ref.shape[1] // fchunk
    order_after = conv_done[:nchunks - 1] + [functools.reduce(lambda a, b: a + b, conv_done[nchunks - 1:])]

    seq_start = lax.rem(i, tiles_per_seq) == 0
    x = x_ref[...]
    h = _rms(x, n1_ref[...]).astype(BF16)
    x1 = x + 0.5 * _swiglu(h, wg_ref, wu_ref, wd_ref, fchunk, order_after)
    x1_ref[...] = x1
    h2 = _rms(x1, nm_ref[...]).astype(BF16)
    c = 3 * GDN_WIDTH
    qkva = _dot(h2, win_ref[:, :c])
    z_ref[...] = _dot(h2, win_ref[:, c:c + GDN_WIDTH])
    c += GDN_WIDTH
    _gdn_gates(_dot(h2, win_ref[:, c:c + LANES]), alog_ref[...], dtb_ref[...], gcol_ref, grow_ref)
    c += LANES

    for cb in range(3 * GDN_HEADS):
        cs = slice(cb * LANES, (cb + 1) * LANES)
        conv_tail(cb, jnp.where(seq_start, 0.0, qkva[0:8, cs]))
        pad_ref[cb, 0:8, :] = jnp.where(seq_start, 0.0, pad_ref[cb, tm:tm + 8, :])
        pad_ref[cb, 8:8 + tm, :] = qkva[:, cs]

    low = lax.broadcasted_iota(jnp.int32, (1, LANES), 1) < SWA_HEAD_DIM

    def head_rms(y, w):
        y2 = y * y
        s0 = jnp.sum(jnp.where(low, y2, 0.0), axis=-1, keepdims=True)
        s1 = jnp.sum(jnp.where(low, 0.0, y2), axis=-1, keepdims=True)
        ms = jnp.where(low, s0, s1) * (1.0 / SWA_HEAD_DIM)
        return y * lax.rsqrt(ms + EPS) * w

    qscale = qw_ref[...] * (SWA_HEAD_DIM ** -0.5 * LOG2E)
    wide = 2 * LANES
    for j in range(3 * SWA_WIDTH // wide):
        y = _dot(h2, win_ref[:, c + j * wide:c + (j + 1) * wide])
        for half in range(2):
            p = 2 * j + half
            yp = y[:, half * LANES:(half + 1) * LANES]
            if p < SWA_WIDTH // LANES:
                yp = head_rms(yp, qscale)
            elif p < 2 * SWA_WIDTH // LANES:
                yp = head_rms(yp, kw_ref[...])
            qkvb_ref[:, p * LANES:(p + 1) * LANES] = yp


def _ffn1_proj(x2d, n1, wg, wu, wd, nm, win, qw, kw, cw, alog, dtb, *, batch, tm, fchunk):
    n, d = x2d.shape
    f = wg.shape[1]
    seq = n // batch
    assert win.shape[1] == 4 * GDN_WIDTH + LANES + 3 * SWA_WIDTH and seq % tm == 0
    nt = n // tm
    tps = seq // tm
    cur = lambda i: jnp.minimum(i, nt - 1)
    late = lambda i: jnp.maximum(i - 1, 0)
    row = lambda w: pl.BlockSpec((tm, w), lambda i: (cur(i), 0))
    heads = pl.BlockSpec((1, GDN_HEADS, tm, GDN_HEAD_DIM), lambda i: (late(i) // tps, 0, late(i) % tps, 0))
    f32 = lambda *shape: jax.ShapeDtypeStruct(shape, F32)
    return pl.pallas_call(
        functools.partial(_ffn1_proj_kernel, fchunk=fchunk, tiles_per_seq=tps, num_tiles=nt),
        grid=(nt + 1,),
        in_specs=[row(d), _resident((1, d)), _resident((d, f)), _resident((d, f)), _resident((f, d)),
                  _resident((1, d)), _resident(win.shape), _resident((1, LANES)), _resident((1, LANES)),
                  _resident(cw.shape), _resident((1, LANES)), _resident((1, LANES))],
        out_specs=[row(d), row(GDN_WIDTH), row(3 * SWA_WIDTH), heads, heads, heads, row(LANES),
                   pl.BlockSpec((tm // CHUNK, 8, 2 * CHUNK), lambda i: (cur(i), 0, 0))],
        out_shape=[f32(n, d), f32(n, GDN_WIDTH), f32(n, 3 * SWA_WIDTH)]
                  + [f32(batch, GDN_HEADS, seq, GDN_HEAD_DIM)] * 3 + [f32(n, LANES), f32(n // CHUNK, 8, 2 * CHUNK)],
        scratch_shapes=[pltpu.VMEM((3 * GDN_HEADS, tm + 8, LANES), F32),
                        pltpu.VMEM((3 * GDN_HEADS, CONV_TAIL + 16, LANES), F32)],
        compiler_params=_cparams(("arbitrary",)),
        name="ffn1_proj",
    )(x2d, n1, wg, wu, wd, nm, win, qw, kw, cw, alog, dtb)


def _out_ffn2_kernel(x1_ref, of_ref, ob_ref, z_ref, attn_ref, gnw_ref, wout_ref, n2_ref,
                     wg_ref, wu_ref, wd_ref, nf_ref, out_ref, *, fchunk):
    o = of_ref[0] + ob_ref[0]
    z = z_ref[...]
    gnw = gnw_ref[...]
    heads = []
    for h in range(GDN_HEADS):
        sl = slice(h * GDN_HEAD_DIM, (h + 1) * GDN_HEAD_DIM)
        heads.append(_rms(o[:, sl], gnw) * _silu(z[:, sl]))
    oa = jnp.concatenate(heads, axis=1).astype(BF16)
    mix = _dot(oa, wout_ref[:GDN_WIDTH, :]) + _dot(attn_ref[...].astype(BF16), wout_ref[GDN_WIDTH:, :])
    x2 = x1_ref[...] + mix
    h2 = _rms(x2, n2_ref[...]).astype(BF16)
    x3 = x2 + 0.5 * _swiglu(h2, wg_ref, wu_ref, wd_ref, fchunk)
    out_ref[...] = _rms(x3, nf_ref[...])


def _out_ffn2(x1, o_dirs, z, attn, gnw, wout, n2, wg, wu, wd, nf, *, tm, fchunk):
    n, d = x1.shape
    f = wg.shape[1]
    row = lambda w: pl.BlockSpec((tm, w), lambda i: (i, 0))
    direction = lambda k: pl.BlockSpec((1, tm, GDN_WIDTH), lambda i: (k, i, 0))
    return pl.pallas_call(
        functools.partial(_out_ffn2_kernel, fchunk=fchunk),
        grid=(n // tm,),
        in_specs=[row(d), direction(0), direction(1), row(GDN_WIDTH), row(SWA_WIDTH),
                  _resident((1, GDN_HEAD_DIM)), _resident(wout.shape), _resident((1, d)),
                  _resident((d, f)), _resident((d, f)), _resident((f, d)), _resident((1, d))],
        out_specs=row(d),
        out_shape=jax.ShapeDtypeStruct((n, d), F32),
        compiler_params=_cparams(("arbitrary",)),
        name="out_ffn2",
    )(x1, o_dirs, o_dirs, z, attn, gnw, wout, n2, wg, wu, wd, nf)


CONV_TAIL = 16


def _gdn_conv_head(src, cw_ref, cb, dst, out0, count):
    cs = slice(cb * LANES, (cb + 1) * LANES)
    half = count // 2
    done = []
    for par in range(2):
        acc = None
        for t in range(CONV_WIDTH):
            term = src[pl.ds(6 + t + par, half, stride=2), :] * cw_ref[t:t + 1, cs]
            acc = term if acc is None else acc + term
        y = _silu(acc)
        if cb < 2 * GDN_HEADS:
            y = y * lax.rsqrt(jnp.sum(y * y, axis=-1, keepdims=True) + EPS)
        if cb < GDN_HEADS:
            y = y * (GDN_HEAD_DIM ** -0.5)
        dst[0, cb % GDN_HEADS, pl.ds(out0 + par, half, stride=2), :] = y
        done.append(y)
    return done


def _gdn_gates(ab, alog, dtb, gcol_ref, grow_ref):
    ts = ab.shape[0]
    xs = ab + dtb
    softplus = jnp.maximum(xs, 0.0) + jnp.log(1.0 + jnp.exp(-jnp.abs(xs)))
    g = -jnp.exp(alog) * softplus
    beta = 1.0 / (1.0 + jnp.exp(-ab))
    g1 = g.astype(BF16)
    r1 = g - g1.astype(F32)
    g2 = r1.astype(BF16)
    g3 = (r1 - g2.astype(F32)).astype(BF16)
    ri = lax.broadcasted_iota(jnp.int32, (LANES, LANES), 0)
    ci = lax.broadcasted_iota(jnp.int32, (LANES, LANES), 1)
    same = (ri // CHUNK) == (ci // CHUNK)
    lower = jnp.where(same & (ri >= ci), 1.0, 0.0).astype(BF16)
    upper = jnp.where(same & (ri <= ci), 1.0, 0.0).astype(BF16)
    lane = lax.broadcasted_iota(jnp.int32, (1, LANES), 1)
    for s in range(ts // LANES):
        rs = slice(s * LANES, (s + 1) * LANES)
        pre = _dot(lower, g1[rs]) + _dot(lower, g2[rs]) + _dot(lower, g3[rs])
        suf = _dot(upper, g1[rs]) + _dot(upper, g2[rs]) + _dot(upper, g3[rs])
        gs = g[rs]
        col = jnp.where(lane < 4, pre,
              jnp.where(lane < 8, suf,
              jnp.where(lane < 16, beta[rs],
              jnp.where(lane < 20, suf - gs,
              jnp.where(lane < 24, pre - gs, pre + suf - gs)))))
        gcol_ref[rs, :] = col
        rows = col.T[0:8]
        swapped = pltpu.roll(rows, CHUNK, 1)
        grow_ref[2 * s] = jnp.where(lane < CHUNK, rows, swapped)
        grow_ref[2 * s + 1] = jnp.where(lane < CHUNK, swapped, rows)


CHUNKS_PER_STEP = 4

def _gdn_scan_kernel(q_ref, k_ref, v_ref, gcol_ref, grow_ref, o_ref,
                     s_ref, wqg_ref, u_ref, ik_ref, eg_ref, *, nc, nblk):
    dirn = pl.program_id(1)
    j = pl.program_id(2)
    fwd = dirn == 0
    slot_build = lax.rem(j, 2)
    slot_scan = 1 - slot_build

    ri = lax.broadcasted_iota(jnp.int32, (CHUNK, 2 * CHUNK), 0)
    li = lax.broadcasted_iota(jnp.int32, (CHUNK, 2 * CHUNK), 1)
    left = li < CHUNK
    dd = (ri - jnp.where(left, li, li - CHUNK)) * (1 - 2 * dirn)
    incl = dd >= 0
    strict = dd > 0
    eye_lo = jnp.where(li == ri, 1.0, 0.0)
    eye_hi = jnp.where(li == ri + CHUNK, 1.0, 0.0)
    nt = (((1,), (1,)), ((), ()))
    zc = jnp.zeros((CHUNK, GDN_HEAD_DIM), BF16)
    zs = jnp.zeros((GDN_HEAD_DIM, GDN_HEAD_DIM), BF16)
    npairs = GDN_HEADS // 2

    def block_diag(a, b, zero):
        return jnp.concatenate([jnp.concatenate([a, zero], axis=1), jnp.concatenate([zero, b], axis=1)], axis=0)

    nsq = int(math.log2(CHUNK))

    def chunk_matrices(ci):
        probs = [(ci * CHUNKS_PER_STEP + cc, p) for cc in range(CHUNKS_PER_STEP) for p in range(npairs)]
        qgs, kds, egs, rs, xa, xb, intras = [], [], [], [], [], [], []
        for c, p in probs:
            r0 = pl.multiple_of(c * CHUNK, CHUNK)
            gates = gcol_ref[0, pl.ds(r0, CHUNK), :]
            rows = grow_ref[0, c]
            per_head = []
            for h in (2 * p, 2 * p + 1):
                def pick(base):
                    return jnp.where(fwd, gates[:, base + h:base + h + 1], gates[:, base + 4 + h:base + 5 + h])
                gc, beta, gdec, gl = pick(0), pick(8), pick(16), pick(24)
                grow = jnp.where(fwd, rows[h:h + 1, :], rows[4 + h:5 + h, :])
                qc = q_ref[0, h, pl.ds(r0, CHUNK), :]
                kc = k_ref[0, h, pl.ds(r0, CHUNK), :]
                vc = v_ref[0, h, pl.ds(r0, CHUNK), :]
                kb = kc * beta
                eg = jnp.exp(gc)
                per_head.append(dict(gc=gc, grow=grow, q=qc, k=kc, kb=kb, r=jnp.concatenate([vc * beta, kb * eg], axis=1),
                                     qg=qc * eg, kd=kc * jnp.exp(gdec), eg=jnp.exp(gl[0:1, :])))
            a, b = per_head
            dec = jnp.where(incl, jnp.exp(jnp.where(incl, jnp.where(left, a["gc"], b["gc"])
                                                    - jnp.where(left[0:1], a["grow"], b["grow"]), 0.0)), 0.0)
            lhs = jnp.concatenate([jnp.concatenate([a["kb"], b["kb"]], axis=1),
                                   jnp.concatenate([a["q"], b["q"]], axis=1)], axis=0).astype(BF16)
            qk = lax.dot_general(lhs, block_diag(a["k"].astype(BF16), b["k"].astype(BF16), zc), nt,
                                 preferred_element_type=F32)
            neg_a = -jnp.where(strict, qk[:CHUNK] * dec, 0.0)
            xa.append(jnp.where(left, neg_a, eye_hi))
            xb.append(jnp.where(left, eye_lo, neg_a))
            intras.append(qk[CHUNK:] * dec)
            rs.append((a["r"], b["r"]))
            qgs.append((a["qg"], b["qg"]))
            kds.append(jnp.concatenate([a["kd"], b["kd"]], axis=0))
            egs.append(jnp.concatenate([jnp.broadcast_to(a["eg"], (8, GDN_HEAD_DIM)),
                                        jnp.broadcast_to(b["eg"], (8, GDN_HEAD_DIM))], axis=1))
        yield
        for t in range(nsq):
            for n in range(len(probs)):
                ya = xa[n].astype(BF16)
                yb = xb[n].astype(BF16)
                zz = jnp.zeros((CHUNK, 2 * CHUNK), BF16)
                y = _dot(jnp.where(left, ya, yb), block_diag(ya, yb, zz))
                xa[n] = jnp.where(left, y[:, :2 * CHUNK], xa[n] + y[:, :2 * CHUNK])
                xb[n] = jnp.where(left, xb[n] + y[:, 2 * CHUNK:], y[:, 2 * CHUNK:])
            yield
        for n, (c, p) in enumerate(probs):
            t_off = jnp.where(left, xb[n] - eye_lo, xa[n] - eye_hi).astype(BF16)
            ra, rb = rs[n]
            zr = jnp.zeros((CHUNK, 2 * GDN_HEAD_DIM), BF16)
            out = _dot(t_off, block_diag(rb.astype(BF16), ra.astype(BF16), zr))
            rb = rb + out[:, :2 * GDN_HEAD_DIM]
            ra = ra + out[:, 2 * GDN_HEAD_DIM:]
            u_ref[slot_build, c, p] = jnp.concatenate([ra[:, :GDN_HEAD_DIM], rb[:, :GDN_HEAD_DIM]], axis=1)
            wqg_ref[slot_build, c, p] = jnp.concatenate(
                [jnp.concatenate([ra[:, GDN_HEAD_DIM:], rb[:, GDN_HEAD_DIM:]], axis=1),
                 jnp.concatenate(qgs[n], axis=1)], axis=0).astype(BF16)
            ik_ref[slot_build, c, p] = jnp.concatenate([intras[n], kds[n].T], axis=0).astype(BF16)
            eg_ref[slot_build, c, p] = egs[n]
        yield

    def scan_chunks(ci):
        for cc in range(CHUNKS_PER_STEP):
            i = ci * CHUNKS_PER_STEP + cc
            c = i + dirn * (nc - 1 - 2 * i)
            r0 = pl.multiple_of(c * CHUNK, CHUNK)
            states = [s_ref[p] for p in range(npairs)]
            m1 = [_dot(wqg_ref[slot_scan, c, p],
                       block_diag(states[p][:, :GDN_HEAD_DIM].astype(BF16), states[p][:, GDN_HEAD_DIM:].astype(BF16), zs))
                  for p in range(npairs)]
            yield
            v_new = [(u_ref[slot_scan, c, p] - m1[p][:CHUNK]).astype(BF16) for p in range(npairs)]
            m2 = [_dot(ik_ref[slot_scan, c, p], block_diag(v_new[p][:, :GDN_HEAD_DIM], v_new[p][:, GDN_HEAD_DIM:], zc))
                  for p in range(npairs)]
            for p in range(npairs):
                ps = slice(2 * p * GDN_HEAD_DIM, 2 * (p + 1) * GDN_HEAD_DIM)
                o_ref[0, 0, pl.ds(r0, CHUNK), ps] = m1[p][CHUNK:] + m2[p][:CHUNK]
                s_ref[p] = states[p] * eg_ref[slot_scan, c, p][0:1, :] + m2[p][CHUNK:]
            yield

    def run(*makers):
        def body(ci, carry):
            live = [m(ci) for m in makers]
            while live:
                live = [g for g in live if next(g, StopIteration) is not StopIteration]
            return carry
        lax.fori_loop(0, nc // CHUNKS_PER_STEP, body, 0)

    @pl.when(j == 0)
    def _():
        s_ref[...] = jnp.zeros(s_ref.shape, F32)
        run(chunk_matrices)

    @pl.when((j > 0) & (j < nblk))
    def _():
        run(chunk_matrices, scan_chunks)

    @pl.when(j == nblk)
    def _():
        run(scan_chunks)


def _gdn_scan(q, k, v, gcol, grow, *, ts):
    b, nh, s, hd = q.shape
    w = nh * hd
    nblk = s // ts
    nc = ts // CHUNK
    order = lambda d, j: j + d * (nblk - 1 - 2 * j)
    src = lambda d, j: order(d, jnp.minimum(j, nblk - 1))
    dst = lambda d, j: order(d, jnp.maximum(j - 1, 0))
    return pl.pallas_call(
        functools.partial(_gdn_scan_kernel, nc=nc, nblk=nblk),
        grid=(b, 2, nblk + 1),
        in_specs=[pl.BlockSpec((1, nh, ts, hd), lambda i, d, j: (i, 0, src(d, j), 0))] * 3
                 + [pl.BlockSpec((1, ts, LANES), lambda i, d, j: (i, src(d, j), 0)),
                    pl.BlockSpec((1, nc, 8, 2 * CHUNK), lambda i, d, j: (i, src(d, j), 0, 0))],
        out_specs=pl.BlockSpec((1, 1, ts, w), lambda i, d, j: (d, i, dst(d, j), 0)),
        out_shape=jax.ShapeDtypeStruct((2, b, s, w), F32),
        scratch_shapes=[pltpu.VMEM((nh // 2, hd, 2 * hd), F32),
                        pltpu.VMEM((2, nc, nh // 2, 2 * CHUNK, 2 * hd), BF16),
                        pltpu.VMEM((2, nc, nh // 2, CHUNK, 2 * hd), F32),
                        pltpu.VMEM((2, nc, nh // 2, CHUNK + hd, 2 * CHUNK), BF16),
                        pltpu.VMEM((2, nc, nh // 2, 8, 2 * hd), F32)],
        compiler_params=_cparams(("arbitrary", "arbitrary", "arbitrary")),
        name="gdn_scan",
    )(q, k, v, gcol, grow)


SWA_TQ = 128
SWA_TK = SWA_TQ + 2 * BAND_RADIUS
SWA_HALO = BAND_RADIUS * max(DILATIONS)
COPY_ROWS = 512
SWA_TILES_PER_STEP = 4


def _swa_kernel(q_ref, k_ref, v_ref, bias_ref, o_ref,
                kn_ref, vp_ref, m0_ref, m1_ref, l_ref, acc_ref, bias_scr, *, seq, qb):
    qi = pl.program_id(2)
    qn_ref = q_ref.at[0]
    lane = lax.broadcasted_iota(jnp.int32, (1, LANES), 1)
    low = lane < SWA_HEAD_DIM
    nt = (((1,), (1,)), ((), ()))

    @pl.when(qi == 0)
    def _():
        for pi in range(len(DILATIONS)):
            for hh in range(2):
                full = jnp.broadcast_to(bias_ref[0, pi, hh], (SWA_TQ, SWA_TK))
                bias_scr[pi, hh] = pltpu.roll(full, 0, 1, stride=1, stride_axis=0)

        zeros = jnp.zeros((SWA_HALO, LANES), F32)
        for ref in (kn_ref, vp_ref):
            ref[0:SWA_HALO, :] = zeros
            ref[SWA_HALO + seq:2 * SWA_HALO + seq, :] = zeros

        def copy_rows(i, carry):
            r0 = pl.multiple_of(i * COPY_ROWS, COPY_ROWS)
            kn_ref[pl.ds(SWA_HALO + r0, COPY_ROWS), :] = k_ref[0, pl.ds(r0, COPY_ROWS), :]
            vp_ref[pl.ds(SWA_HALO + r0, COPY_ROWS), :] = v_ref[0, pl.ds(r0, COPY_ROWS), :]
            return carry

        lax.fori_loop(0, seq // COPY_ROWS, copy_rows, 0)

    kcol = lax.broadcasted_iota(jnp.int32, (1, SWA_TK), 1)

    def rows(ref, start, size, stride):
        if stride == 1:
            return ref[pl.ds(start, size), :]
        return ref[pl.ds(start, size, stride=stride), :]

    def put(ref, start, size, stride, val):
        if stride == 1:
            ref[pl.ds(start, size), :] = val
        else:
            ref[pl.ds(start, size, stride=stride), :] = val

    def logits(pi, dil, g):
        sub_len = seq // dil
        tiles_per_res = qb // dil // SWA_TQ
        ss = []
        for u in range(SWA_TILES_PER_STEP):
            res, t = divmod(g * SWA_TILES_PER_STEP + u, tiles_per_res)
            tau0 = qi * (qb // dil) + t * SWA_TQ
            loc0 = res + dil * (t * SWA_TQ)
            krow = SWA_HALO + qi * qb + loc0 - dil * BAND_RADIUS
            qt = rows(qn_ref, qi * qb + loc0, SWA_TQ, dil)
            kt = rows(kn_ref, krow, SWA_TK, dil).astype(BF16)
            at_end = t in (0, tiles_per_res - 1)
            kidx = kcol + (tau0 - BAND_RADIUS)
            kvalid = (kidx >= 0) & (kidx < sub_len)
            for hh in range(2):
                mine = low if hh == 0 else jnp.logical_not(low)
                qh = jnp.where(mine, qt, 0.0).astype(BF16)
                s = lax.dot_general(qh, kt, nt, preferred_element_type=F32) + bias_scr[pi, hh]
                ss.append(jnp.where(kvalid, s, NEG_BIG) if at_end else s)
        return ss

    def accumulate(dil, g, ss, first):
        tiles_per_res = qb // dil // SWA_TQ
        loc, vts, old = [], [], []
        for u in range(SWA_TILES_PER_STEP):
            res, t = divmod(g * SWA_TILES_PER_STEP + u, tiles_per_res)
            loc0 = res + dil * (t * SWA_TQ)
            vt = rows(vp_ref, SWA_HALO + qi * qb + loc0 - dil * BAND_RADIUS, SWA_TK, dil)
            vts.append((jnp.where(low, vt, 1.0).astype(BF16), jnp.where(low, 1.0, vt).astype(BF16)))
            loc.append(loc0)
            if not first:
                old.append(tuple(rows(ref, loc0, SWA_TQ, dil) for ref in (m0_ref, m1_ref, l_ref, acc_ref)))
        m_new, alpha, ps = [], [], []
        for n, s in enumerate(ss):
            u, hh = divmod(n, 2)
            mt = jnp.max(s, axis=-1, keepdims=True)
            if first:
                mn = jnp.broadcast_to(mt, (SWA_TQ, LANES))
            else:
                mo = old[u][hh]
                mn = jnp.maximum(mo, mt)
                alpha.append(jnp.exp2(mo - mn))
            m_new.append(mn)
            ps.append(jnp.exp2(s - jnp.concatenate([mn, mn], axis=1)).astype(BF16))
        pv = [_dot(p, vts[n // 2][n % 2]) for n, p in enumerate(ps)]
        for u in range(SWA_TILES_PER_STEP):
            pv0, pv1 = pv[2 * u], pv[2 * u + 1]
            put(m0_ref, loc[u], SWA_TQ, dil, m_new[2 * u])
            put(m1_ref, loc[u], SWA_TQ, dil, m_new[2 * u + 1])
            lsum = jnp.where(low, pv1, pv0)
            pvv = jnp.where(low, pv0, pv1)
            if first:
                put(l_ref, loc[u], SWA_TQ, dil, lsum)
                put(acc_ref, loc[u], SWA_TQ, dil, pvv)
            else:
                a0, a1 = alpha[2 * u], alpha[2 * u + 1]
                put(l_ref, loc[u], SWA_TQ, dil, old[u][2] * jnp.where(low, a1, a0) + lsum)
                put(acc_ref, loc[u], SWA_TQ, dil, old[u][3] * jnp.where(low, a0, a1) + pvv)

    order = sorted(range(len(DILATIONS)), key=lambda pi: -DILATIONS[pi])
    units = [(pi, DILATIONS[pi], g) for pi in order for g in range(qb // SWA_TQ // SWA_TILES_PER_STEP)]
    pending = logits(*units[0])
    for k, (pi, dil, g) in enumerate(units):
        nxt = logits(*units[k + 1]) if k + 1 < len(units) else None
        accumulate(dil, g, pending, first=pi == order[0])
        pending = nxt

    o_ref[0] = acc_ref[...] / pltpu.roll(l_ref[...], SWA_HEAD_DIM, 1)


def _t5_bucket(rel):
    nb = REL_BUCKETS // 2
    bucket = (rel > 0).astype(np.int32) * nb
    n = np.abs(rel)
    max_exact = nb // 2
    large = max_exact + (np.log(np.maximum(n, 1) / max_exact)
                         / math.log(REL_MAX_DISTANCE / max_exact) * (nb - max_exact)).astype(np.int32)
    large = np.minimum(large, nb - 1)
    return (bucket + np.where(n < max_exact, n, large)).astype(np.int32)


def _band_bias(rel_bias):
    rel = np.arange(2 * BAND_RADIUS + 1) - BAND_RADIUS
    rows = []
    for dil in DILATIONS:
        inside = jnp.transpose(rel_bias[_t5_bucket(rel * dil)]) * LOG2E
        rows.append(jnp.pad(inside, ((0, 0), (0, SWA_TK - inside.shape[1])), constant_values=NEG_BIG))
    bias = jnp.stack(rows, axis=0)
    bias = bias.reshape(len(DILATIONS), SWA_HEADS // 2, 2, 1, SWA_TK)
    return jnp.transpose(bias, (1, 0, 2, 3, 4)).astype(F32)


def _swa(qkvb, bias, *, qb):
    b, s, _ = qkvb.shape
    pairs = SWA_HEADS // 2
    assert s % qb == 0 and qb % (max(DILATIONS) * SWA_TQ) == 0
    col = lambda base: pl.BlockSpec((1, s, LANES), lambda i, p, j: (i, 0, base + p))
    padded = s + 2 * SWA_HALO
    return pl.pallas_call(
        functools.partial(_swa_kernel, seq=s, qb=qb),
        grid=(b, pairs, s // qb),
        in_specs=[col(0), col(pairs), col(2 * pairs),
                  pl.BlockSpec((1,) + bias.shape[1:], lambda i, p, j: (p, 0, 0, 0, 0))],
        out_specs=pl.BlockSpec((1, qb, LANES), lambda i, p, j: (i, j, p)),
        out_shape=jax.ShapeDtypeStruct((b, s, SWA_WIDTH), F32),
        scratch_shapes=[pltpu.VMEM((padded, LANES), F32), pltpu.VMEM((padded, LANES), F32)]
                       + [pltpu.VMEM((qb, LANES), F32)] * 4
                       + [pltpu.VMEM((len(DILATIONS), 2, SWA_TQ, SWA_TK), F32)],
        compiler_params=_cparams(("arbitrary", "arbitrary", "arbitrary")),
        name="swa",
    )(qkvb, qkvb, qkvb, bias)


FFN_TM = 512
FFN_FCHUNK = 256
SCAN_TS = 1024
SWA_QB = 2048


def _lane_row(vals, copies_at):
    row = jnp.zeros((LANES,), F32)
    for off in copies_at:
        row = row.at[off:off + vals.shape[0]].set(vals)
    return row[None, :]


def kernel(x, ffn1_norm, ffn1_w_gate, ffn1_w_up, ffn1_w_down, mix_norm, w_in, conv_w, a_log, dt_bias, gdn_norm_w, q_norm_w, k_norm_w, rel_bias, w_out, ffn2_norm, ffn2_w_gate, ffn2_w_up, ffn2_w_down, final_norm):
    b, s, d = x.shape
    n = b * s
    x2d = x.reshape(n, d)
    ng = 2 * GDN_HEADS
    c_z = 3 * GDN_WIDTH
    c_a = c_z + GDN_WIDTH
    c_b = c_a + ng
    c_qkvb = c_b + ng
    bias = _band_bias(rel_bias)
    for l in range(ffn1_norm.shape[0]):
        wl = w_in[l]
        w_a = wl[:, c_a:c_b]
        w_gates = jnp.concatenate([w_a, wl[:, c_b:c_qkvb], w_a, w_a, jnp.zeros((d, LANES - 4 * ng), F32)], axis=1)
        win = jnp.concatenate([wl[:, :c_a], w_gates, wl[:, c_qkvb:]], axis=1).astype(BF16)
        copies = (0, 2 * ng, 3 * ng)
        alog = _lane_row(a_log[l].reshape(ng), copies)
        dtb = _lane_row(dt_bias[l].reshape(ng), copies)
        x1, z, qkvb, q, k, v, gcol, grow = _ffn1_proj(
            x2d, ffn1_norm[l][None, :], ffn1_w_gate[l].astype(BF16), ffn1_w_up[l].astype(BF16),
            ffn1_w_down[l].astype(BF16), mix_norm[l][None, :], win,
            jnp.tile(q_norm_w[l], 2)[None, :], jnp.tile(k_norm_w[l], 2)[None, :],
            jnp.transpose(conv_w[l]), alog, dtb, batch=b, tm=FFN_TM, fchunk=FFN_FCHUNK)
        o_dirs = _gdn_scan(q, k, v, gcol.reshape(b, s, LANES), grow.reshape(b, s // CHUNK, 8, 2 * CHUNK), ts=SCAN_TS)
        attn = _swa(qkvb.reshape(b, s, -1), bias, qb=SWA_QB)
        x2d = _out_ffn2(
            x1, o_dirs.reshape(2, n, -1), z, attn.reshape(n, -1),
            gdn_norm_w[l][None, :], w_out[l].astype(BF16), ffn2_norm[l][None, :],
            ffn2_w_gate[l].astype(BF16), ffn2_w_up[l].astype(BF16), ffn2_w_down[l].astype(BF16),
            final_norm[l][None, :], tm=FFN_TM, fchunk=FFN_FCHUNK)
    return x2d.reshape(b, s, d)
```

```python
import functools
import math

import numpy as np
import jax
import jax.numpy as jnp
from jax import lax
from jax.experimental import pallas as pl
from jax.experimental.pallas import tpu as pltpu

F32 = jnp.float32
BF16 = jnp.bfloat16
EPS = 1e-6
NEG_BIG = -1e30
LOG2E = math.log2(math.e)

LANES = 128
GDN_HEADS = 4
GDN_HEAD_DIM = 128
GDN_WIDTH = GDN_HEADS * GDN_HEAD_DIM
CONV_WIDTH = 5
CHUNK = 64
SWA_HEADS = 8
SWA_HEAD_DIM = 64
SWA_WIDTH = SWA_HEADS * SWA_HEAD_DIM
DILATIONS = (1, 4, 16)
BAND_RADIUS = 64
REL_BUCKETS = 32
REL_MAX_DISTANCE = 1024
VMEM_LIMIT = 56 * 1024 * 1024


def _cparams(sem):
    return pltpu.CompilerParams(dimension_semantics=sem, vmem_limit_bytes=VMEM_LIMIT)


def _resident(shape):
    zeros = (0,) * len(shape)
    return pl.BlockSpec(shape, lambda *_: zeros, pipeline_mode=pl.Buffered(1))


def _rms(x, w):
    return x * lax.rsqrt(jnp.mean(x * x, axis=-1, keepdims=True) + EPS) * w


def _silu(x):
    return x * (1.0 / (1.0 + jnp.exp(-x)))


def _dot(a, b):
    return jnp.dot(a, b, preferred_element_type=F32)


def _zero_after(values):
    s = values[0:8]
    for r in range(8, values.shape[0], 8):
        s = s + values[r:r + 8]
    bits = lax.bitcast_convert_type(s, jnp.int32)
    return lax.shift_right_logical(lax.shift_right_logical(bits, 16), 16).astype(F32)


def _swiglu(h, wg_ref, wu_ref, wd_ref, fchunk, order_after=()):
    acc = None
    for n, c0 in enumerate(range(0, wg_ref.shape[1], fchunk)):
        g = _dot(h, wg_ref[:, c0:c0 + fchunk])
        if n < len(order_after):
            first = jnp.concatenate([g[0:8, 0:LANES] + order_after[n], g[0:8, LANES:]], axis=1)
            g = jnp.concatenate([first, g[8:]], axis=0)
        u = _dot(h, wu_ref[:, c0:c0 + fchunk])
        a = (_silu(g) * u).astype(BF16)
        d = _dot(a, wd_ref[c0:c0 + fchunk, :])
        acc = d if acc is None else acc + d
    return acc


def _ffn1_proj_kernel(x_ref, n1_ref, wg_ref, wu_ref, wd_ref, nm_ref, win_ref, qw_ref, kw_ref,
                      cw_ref, alog_ref, dtb_ref,
                      x1_ref, z_ref, qkvb_ref, q_ref, k_ref, v_ref, gcol_ref, grow_ref, pad_ref, tail_ref,
                      *, fchunk, tiles_per_seq, num_tiles):
    i = pl.program_id(0)
    tm = x_ref.shape[0]
    heads_out = (q_ref, k_ref, v_ref)

    @pl.when(i == 0)
    def _():
        pad_ref[...] = jnp.zeros(pad_ref.shape, F32)

    def conv_tail(cb, next_rows):
        tail_ref[cb, 0:8 + CONV_TAIL, :] = pad_ref[cb, tm - CONV_TAIL:8 + tm, :]
        tail_ref[cb, 8 + CONV_TAIL:16 + CONV_TAIL, :] = next_rows
        _gdn_conv_head(tail_ref.at[cb], cw_ref, cb, heads_out[cb // GDN_HEADS], tm - CONV_TAIL, CONV_TAIL)

    @pl.when(i == num_tiles)
    def _():
        for cb in range(3 * GDN_HEADS):
            _gdn_conv_head(pad_ref.at[cb], cw_ref, cb, heads_out[cb // GDN_HEADS], 0, tm - CONV_TAIL)
            conv_tail(cb, jnp.zeros((8, LANES), F32))

    @pl.when(i < num_tiles)
    def _():
        _ffn1_proj_tile(i, x_ref, n1_ref, wg_ref, wu_ref, wd_ref, nm_ref, win_ref, qw_ref, kw_ref,
                        cw_ref, alog_ref, dtb_ref, x1_ref, z_ref, qkvb_ref, heads_out, gcol_ref, grow_ref,
                        pad_ref, conv_tail, fchunk, tiles_per_seq)


def _ffn1_proj_tile(i, x_ref, n1_ref, wg_ref, wu_ref, wd_ref, nm_ref, win_ref, qw_ref, kw_ref,
                    cw_ref, alog_ref, dtb_ref, x1_ref, z_ref, qkvb_ref, heads_out, gcol_ref, grow_ref,
                    pad_ref, conv_tail, fchunk, tiles_per_seq):
    tm = x_ref.shape[0]
    conv_done = []
    for cb in range(3 * GDN_HEADS):
        even, odd = _gdn_conv_head(pad_ref.at[cb], cw_ref, cb, heads_out[cb // GDN_HEADS], 0, tm - CONV_TAIL)
        conv_done.append(_zero_after(even + odd))
    nchunks = wg_ref.shape[1] // fchunk
    order_after = conv_done[:nchunks - 1] + [functools.reduce(lambda a, b: a + b, conv_done[nchunks - 1:])]

    seq_start = lax.rem(i, tiles_per_seq) == 0
    x = x_ref[...]
    h = _rms(x, n1_ref[...]).astype(BF16)
    x1 = x + 0.5 * _swiglu(h, wg_ref, wu_ref, wd_ref, fchunk, order_after)
    x1_ref[...] = x1
    h2 = _rms(x1, nm_ref[...]).astype(BF16)
    c = 3 * GDN_WIDTH
    qkva = _dot(h2, win_ref[:, :c])
    z_ref[...] = _dot(h2, win_ref[:, c:c + GDN_WIDTH])
    c += GDN_WIDTH
    _gdn_gates(_dot(h2, win_ref[:, c:c + LANES]), alog_ref[...], dtb_ref[...], gcol_ref, grow_ref)
    c += LANES

    for cb in range(3 * GDN_HEADS):
        cs = slice(cb * LANES, (cb + 1) * LANES)
        conv_tail(cb, jnp.where(seq_start, 0.0, qkva[0:8, cs]))
        pad_ref[cb, 0:8, :] = jnp.where(seq_start, 0.0, pad_ref[cb, tm:tm + 8, :])
        pad_ref[cb, 8:8 + tm, :] = qkva[:, cs]

    low = lax.broadcasted_iota(jnp.int32, (1, LANES), 1) < SWA_HEAD_DIM

    def head_rms(y, w):
        y2 = y * y
        s0 = jnp.sum(jnp.where(low, y2, 0.0), axis=-1, keepdims=True)
        s1 = jnp.sum(jnp.where(low, 0.0, y2), axis=-1, keepdims=True)
        ms = jnp.where(low, s0, s1) * (1.0 / SWA_HEAD_DIM)
        return y * lax.rsqrt(ms + EPS) * w

    qscale = qw_ref[...] * (SWA_HEAD_DIM ** -0.5 * LOG2E)
    wide = 2 * LANES
    for j in range(3 * SWA_WIDTH // wide):
        y = _dot(h2, win_ref[:, c + j * wide:c + (j + 1) * wide])
        for half in range(2):
            p = 2 * j + half
            yp = y[:, half * LANES:(half + 1) * LANES]
            if p < SWA_WIDTH // LANES:
                yp = head_rms(yp, qscale)
            elif p < 2 * SWA_WIDTH // LANES:
                yp = head_rms(yp, kw_ref[...])
            qkvb_ref[:, p * LANES:(p + 1) * LANES] = yp


def _ffn1_proj(x2d, n1, wg, wu, wd, nm, win, qw, kw, cw, alog, dtb, *, batch, tm, fchunk):
    n, d = x2d.shape
    f = wg.shape[1]
    seq = n // batch
    assert win.shape[1] == 4 * GDN_WIDTH + LANES + 3 * SWA_WIDTH and seq % tm == 0
    nt = n // tm
    tps = seq // tm
    cur = lambda i: jnp.minimum(i, nt - 1)
    late = lambda i: jnp.maximum(i - 1, 0)
    row = lambda w: pl.BlockSpec((tm, w), lambda i: (cur(i), 0))
    heads = pl.BlockSpec((1, GDN_HEADS, tm, GDN_HEAD_DIM), lambda i: (late(i) // tps, 0, late(i) % tps, 0))
    f32 = lambda *shape: jax.ShapeDtypeStruct(shape, F32)
    return pl.pallas_call(
        functools.partial(_ffn1_proj_kernel, fchunk=fchunk, tiles_per_seq=tps, num_tiles=nt),
        grid=(nt + 1,),
        in_specs=[row(d), _resident((1, d)), _resident((d, f)), _resident((d, f)), _resident((f, d)),
                  _resident((1, d)), _resident(win.shape), _resident((1, LANES)), _resident((1, LANES)),
                  _resident(cw.shape), _resident((1, LANES)), _resident((1, LANES))],
        out_specs=[row(d), row(GDN_WIDTH), row(3 * SWA_WIDTH), heads, heads, heads, row(LANES),
                   pl.BlockSpec((tm // CHUNK, 8, 2 * CHUNK), lambda i: (cur(i), 0, 0))],
        out_shape=[f32(n, d), f32(n, GDN_WIDTH), f32(n, 3 * SWA_WIDTH)]
                  + [f32(batch, GDN_HEADS, seq, GDN_HEAD_DIM)] * 3 + [f32(n, LANES), f32(n // CHUNK, 8, 2 * CHUNK)],
        scratch_shapes=[pltpu.VMEM((3 * GDN_HEADS, tm + 8, LANES), F32),
                        pltpu.VMEM((3 * GDN_HEADS, CONV_TAIL + 16, LANES), F32)],
        compiler_params=_cparams(("arbitrary",)),
        name="ffn1_proj",
    )(x2d, n1, wg, wu, wd, nm, win, qw, kw, cw, alog, dtb)


def _out_ffn2_kernel(x1_ref, of_ref, ob_ref, z_ref, attn_ref, gnw_ref, wout_ref, n2_ref,
                     wg_ref, wu_ref, wd_ref, nf_ref, out_ref, *, fchunk):
    o = of_ref[0] + ob_ref[0]
    z = z_ref[...]
    gnw = gnw_ref[...]
    heads = []
    for h in range(GDN_HEADS):
        sl = slice(h * GDN_HEAD_DIM, (h + 1) * GDN_HEAD_DIM)
        heads.append(_rms(o[:, sl], gnw) * _silu(z[:, sl]))
    oa = jnp.concatenate(heads, axis=1).astype(BF16)
    mix = _dot(oa, wout_ref[:GDN_WIDTH, :]) + _dot(attn_ref[...].astype(BF16), wout_ref[GDN_WIDTH:, :])
    x2 = x1_ref[...] + mix
    h2 = _rms(x2, n2_ref[...]).astype(BF16)
    x3 = x2 + 0.5 * _swiglu(h2, wg_ref, wu_ref, wd_ref, fchunk)
    out_ref[...] = _rms(x3, nf_ref[...])


def _out_ffn2(x1, o_dirs, z, attn, gnw, wout, n2, wg, wu, wd, nf, *, tm, fchunk):
    n, d = x1.shape
    f = wg.shape[1]
    row = lambda w: pl.BlockSpec((tm, w), lambda i: (i, 0))
    direction = lambda k: pl.BlockSpec((1, tm, GDN_WIDTH), lambda i: (k, i, 0))
    return pl.pallas_call(
        functools.partial(_out_ffn2_kernel, fchunk=fchunk),
        grid=(n // tm,),
        in_specs=[row(d), direction(0), direction(1), row(GDN_WIDTH), row(SWA_WIDTH),
                  _resident((1, GDN_HEAD_DIM)), _resident(wout.shape), _resident((1, d)),
                  _resident((d, f)), _resident((d, f)), _resident((f, d)), _resident((1, d))],
        out_specs=row(d),
        out_shape=jax.ShapeDtypeStruct((n, d), F32),
        compiler_params=_cparams(("arbitrary",)),
        name="out_ffn2",
    )(x1, o_dirs, o_dirs, z, attn, gnw, wout, n2, wg, wu, wd, nf)


CONV_TAIL = 16


def _gdn_conv_head(src, cw_ref, cb, dst, out0, count):
    cs = slice(cb * LANES, (cb + 1) * LANES)
    half = count // 2
    done = []
    for par in range(2):
        acc = None
        for t in range(CONV_WIDTH):
            term = src[pl.ds(6 + t + par, half, stride=2), :] * cw_ref[t:t + 1, cs]
            acc = term if acc is None else acc + term
        y = _silu(acc)
        if cb < 2 * GDN_HEADS:
            y = y * lax.rsqrt(jnp.sum(y * y, axis=-1, keepdims=True) + EPS)
        if cb < GDN_HEADS:
            y = y * (GDN_HEAD_DIM ** -0.5)
        dst[0, cb % GDN_HEADS, pl.ds(out0 + par, half, stride=2), :] = y
        done.append(y)
    return done


def _gdn_gates(ab, alog, dtb, gcol_ref, grow_ref):
    ts = ab.shape[0]
    xs = ab + dtb
    softplus = jnp.maximum(xs, 0.0) + jnp.log(1.0 + jnp.exp(-jnp.abs(xs)))
    g = -jnp.exp(alog) * softplus
    beta = 1.0 / (1.0 + jnp.exp(-ab))
    g1 = g.astype(BF16)
    r1 = g - g1.astype(F32)
    g2 = r1.astype(BF16)
    g3 = (r1 - g2.astype(F32)).astype(BF16)
    ri = lax.broadcasted_iota(jnp.int32, (LANES, LANES), 0)
    ci = lax.broadcasted_iota(jnp.int32, (LANES, LANES), 1)
    same = (ri // CHUNK) == (ci // CHUNK)
    lower = jnp.where(same & (ri >= ci), 1.0, 0.0).astype(BF16)
    upper = jnp.where(same & (ri <= ci), 1.0, 0.0).astype(BF16)
    lane = lax.broadcasted_iota(jnp.int32, (1, LANES), 1)
    for s in range(ts // LANES):
        rs = slice(s * LANES, (s + 1) * LANES)
        pre = _dot(lower, g1[rs]) + _dot(lower, g2[rs]) + _dot(lower, g3[rs])
        suf = _dot(upper, g1[rs]) + _dot(upper, g2[rs]) + _dot(upper, g3[rs])
        gs = g[rs]
        col = jnp.where(lane < 4, pre,
              jnp.where(lane < 8, suf,
              jnp.where(lane < 16, beta[rs],
              jnp.where(lane < 20, suf - gs,
              jnp.where(lane < 24, pre - gs, pre + suf - gs)))))
        gcol_ref[rs, :] = col
        rows = col.T[0:8]
        swapped = pltpu.roll(rows, CHUNK, 1)
        grow_ref[2 * s] = jnp.where(lane < CHUNK, rows, swapped)
        grow_ref[2 * s + 1] = jnp.where(lane < CHUNK, swapped, rows)


CHUNKS_PER_STEP = 4

def _gdn_scan_kernel(q_ref, k_ref, v_ref, gcol_ref, grow_ref, o_ref,
                     s_ref, rw_ref, ut_ref, kd_ref, in_ref, eg_ref, *, nc, nblk):
    dirn = pl.program_id(1)
    j = pl.program_id(2)
    fwd = dirn == 0
    slot_build = lax.rem(j, 2)
    slot_scan = 1 - slot_build

    ri = lax.broadcasted_iota(jnp.int32, (CHUNK, 2 * CHUNK), 0)
    li = lax.broadcasted_iota(jnp.int32, (CHUNK, 2 * CHUNK), 1)
    left = li < CHUNK
    dd = (ri - jnp.where(left, li, li - CHUNK)) * (1 - 2 * dirn)
    incl = dd >= 0
    strict = dd > 0
    eye_lo = jnp.where(li == ri, 1.0, 0.0)
    eye_hi = jnp.where(li == ri + CHUNK, 1.0, 0.0)
    nt = (((1,), (1,)), ((), ()))
    zc = jnp.zeros((CHUNK, GDN_HEAD_DIM), BF16)
    zs = jnp.zeros((GDN_HEAD_DIM, GDN_HEAD_DIM), BF16)
    left128 = lax.broadcasted_iota(jnp.int32, (GDN_HEAD_DIM, GDN_HEAD_DIM), 1) < CHUNK
    npairs = GDN_HEADS // 2

    def block_diag(a, b, zero):
        return jnp.concatenate([jnp.concatenate([a, zero], axis=1), jnp.concatenate([zero, b], axis=1)], axis=0)

    nsq = int(math.log2(CHUNK))

    def chunk_matrices(ci):
        probs = [(ci * CHUNKS_PER_STEP + cc, p) for cc in range(CHUNKS_PER_STEP) for p in range(npairs)]
        qgs, kds, egs, rs, xa, xb, intras = [], [], [], [], [], [], []
        for c, p in probs:
            r0 = pl.multiple_of(c * CHUNK, CHUNK)
            gates = gcol_ref[0, pl.ds(r0, CHUNK), :]
            rows = grow_ref[0, c]
            per_head = []
            for h in (2 * p, 2 * p + 1):
                def pick(base):
                    return jnp.where(fwd, gates[:, base + h:base + h + 1], gates[:, base + 4 + h:base + 5 + h])
                gc, beta, gdec, gl = pick(0), pick(8), pick(16), pick(24)
                grow = jnp.where(fwd, rows[h:h + 1, :], rows[4 + h:5 + h, :])
                qc = q_ref[0, h, pl.ds(r0, CHUNK), :]
                kc = k_ref[0, h, pl.ds(r0, CHUNK), :]
                vc = v_ref[0, h, pl.ds(r0, CHUNK), :]
                kb = kc * beta
                eg = jnp.exp(gc)
                per_head.append(dict(gc=gc, grow=grow, q=qc, k=kc, kb=kb, r=jnp.concatenate([vc * beta, kb * eg], axis=1),
                                     qg=qc * eg, kd=kc * jnp.exp(gdec), eg=jnp.exp(gl[0:1, :])))
            a, b = per_head
            dec = jnp.where(incl, jnp.exp(jnp.where(incl, jnp.where(left, a["gc"], b["gc"])
                                                    - jnp.where(left[0:1], a["grow"], b["grow"]), 0.0)), 0.0)
            lhs = jnp.concatenate([jnp.concatenate([a["kb"], b["kb"]], axis=1),
                                   jnp.concatenate([a["q"], b["q"]], axis=1)], axis=0).astype(BF16)
            qk = lax.dot_general(lhs, block_diag(a["k"].astype(BF16), b["k"].astype(BF16), zc), nt,
                                 preferred_element_type=F32)
            neg_a = -jnp.where(strict, qk[:CHUNK] * dec, 0.0)
            xa.append(jnp.where(left, neg_a, eye_hi))
            xb.append(jnp.where(left, eye_lo, neg_a))
            intras.append(qk[CHUNK:] * dec)
            rs.append((a["r"], b["r"]))
            qgs.append((a["qg"], b["qg"]))
            kds.append((a["kd"], b["kd"]))
            egs.append(jnp.concatenate([jnp.broadcast_to(a["eg"], (8, GDN_HEAD_DIM)),
                                        jnp.broadcast_to(b["eg"], (8, GDN_HEAD_DIM))], axis=1))
        yield
        for t in range(nsq):
            for n in range(len(probs)):
                ya = xa[n].astype(BF16)
                yb = xb[n].astype(BF16)
                zz = jnp.zeros((CHUNK, 2 * CHUNK), BF16)
                y = _dot(jnp.where(left, ya, yb), block_diag(ya, yb, zz))
                xa[n] = jnp.where(left, y[:, :2 * CHUNK], xa[n] + y[:, :2 * CHUNK])
                xb[n] = jnp.where(left, xb[n] + y[:, 2 * CHUNK:], y[:, 2 * CHUNK:])
            yield
        for n, (c, p) in enumerate(probs):
            t_off = jnp.where(left, xb[n] - eye_lo, xa[n] - eye_hi).astype(BF16)
            ra, rb = rs[n]
            zr = jnp.zeros((CHUNK, 2 * GDN_HEAD_DIM), BF16)
            out = _dot(t_off, block_diag(rb.astype(BF16), ra.astype(BF16), zr))
            rb = rb + out[:, :2 * GDN_HEAD_DIM]
            ra = ra + out[:, 2 * GDN_HEAD_DIM:]
            ua, wa = ra[:, :GDN_HEAD_DIM], ra[:, GDN_HEAD_DIM:]
            ub, wb = rb[:, :GDN_HEAD_DIM], rb[:, GDN_HEAD_DIM:]
            qga, qgb = qgs[n]
            kda, kdb = kds[n]
            rw_ref[slot_build, c, p] = jnp.concatenate(
                [jnp.concatenate([wa, qga], axis=0), jnp.concatenate([qgb, wb], axis=0)], axis=1).astype(BF16)
            ut = jnp.concatenate([ua, ub], axis=0).T
            ut_ref[slot_build, c, p] = jnp.concatenate([jnp.where(left128, ut, 0.0), jnp.where(left128, 0.0, ut)], axis=1)
            kd_ref[slot_build, c, p] = jnp.concatenate([kda, kdb], axis=1).astype(BF16)
            in_ref[slot_build, c, p] = intras[n].astype(BF16)
            eg_ref[slot_build, c, p] = egs[n]
        yield

    def scan_chunks(ci):
        z2 = jnp.zeros((CHUNK, 2 * GDN_HEAD_DIM), BF16)
        for cc in range(CHUNKS_PER_STEP):
            i = ci * CHUNKS_PER_STEP + cc
            c = i + dirn * (nc - 1 - 2 * i)
            r0 = pl.multiple_of(c * CHUNK, CHUNK)
            states = [s_ref[p] for p in range(npairs)]
            m1 = []
            for p in range(npairs):
                rw = rw_ref[slot_scan, c, p]
                m1.append(lax.dot_general(states[p].astype(BF16),
                                          block_diag(rw[:, :GDN_HEAD_DIM], rw[:, GDN_HEAD_DIM:], zs), nt,
                                          preferred_element_type=F32))
            yield
            m2s, m2o = [], []
            for p in range(npairs):
                vt = (ut_ref[slot_scan, c, p] - m1[p]).astype(BF16)
                kd = kd_ref[slot_scan, c, p]
                m2s.append(_dot(vt, jnp.concatenate([jnp.concatenate([kd[:, :GDN_HEAD_DIM], zc], axis=1), z2, z2,
                                                     jnp.concatenate([zc, kd[:, GDN_HEAD_DIM:]], axis=1)], axis=0)))
                intra = in_ref[slot_scan, c, p]
                zi = jnp.zeros_like(intra)
                rows_b = jnp.concatenate([zi, jnp.where(left, zi, intra)], axis=1)
                rows_a = jnp.concatenate([jnp.where(left, intra, zi), zi], axis=1)
                m2o.append(lax.dot_general(vt, jnp.concatenate([rows_b, rows_a], axis=0), nt,
                                           preferred_element_type=F32))
            for p in range(npairs):
                ps = slice(2 * p * GDN_HEAD_DIM, 2 * (p + 1) * GDN_HEAD_DIM)
                ot = (jnp.where(left128, m1[p][:, GDN_HEAD_DIM:], m1[p][:, :GDN_HEAD_DIM]) + m2o[p]).T
                o_ref[0, 0, pl.ds(r0, CHUNK), ps] = jnp.concatenate([ot[CHUNK:], ot[:CHUNK]], axis=1)
                s_ref[p] = states[p] * eg_ref[slot_scan, c, p][0:1, :] + m2s[p]
            yield

    def run(*makers):
        def body(ci, carry):
            live = [m(ci) for m in makers]
            while live:
                live = [g for g in live if next(g, StopIteration) is not StopIteration]
            return carry
        lax.fori_loop(0, nc // CHUNKS_PER_STEP, body, 0)

    @pl.when(j == 0)
    def _():
        s_ref[...] = jnp.zeros(s_ref.shape, F32)
        run(chunk_matrices)

    @pl.when((j > 0) & (j < nblk))
    def _():
        run(chunk_matrices, scan_chunks)

    @pl.when(j == nblk)
    def _():
        run(scan_chunks)


def _gdn_scan(q, k, v, gcol, grow, *, ts):
    b, nh, s, hd = q.shape
    w = nh * hd
    nblk = s // ts
    nc = ts // CHUNK
    order = lambda d, j: j + d * (nblk - 1 - 2 * j)
    src = lambda d, j: order(d, jnp.minimum(j, nblk - 1))
    dst = lambda d, j: order(d, jnp.maximum(j - 1, 0))
    return pl.pallas_call(
        functools.partial(_gdn_scan_kernel, nc=nc, nblk=nblk),
        grid=(b, 2, nblk + 1),
        in_specs=[pl.BlockSpec((1, nh, ts, hd), lambda i, d, j: (i, 0, src(d, j), 0))] * 3
                 + [pl.BlockSpec((1, ts, LANES), lambda i, d, j: (i, src(d, j), 0)),
                    pl.BlockSpec((1, nc, 8, 2 * CHUNK), lambda i, d, j: (i, src(d, j), 0, 0))],
        out_specs=pl.BlockSpec((1, 1, ts, w), lambda i, d, j: (d, i, dst(d, j), 0)),
        out_shape=jax.ShapeDtypeStruct((2, b, s, w), F32),
        scratch_shapes=[pltpu.VMEM((nh // 2, hd, 2 * hd), F32),
                        pltpu.VMEM((2, nc, nh // 2, 2 * CHUNK, 2 * hd), BF16),
                        pltpu.VMEM((2, nc, nh // 2, hd, 2 * hd), F32),
                        pltpu.VMEM((2, nc, nh // 2, CHUNK, 2 * hd), BF16),
                        pltpu.VMEM((2, nc, nh // 2, CHUNK, 2 * CHUNK), BF16),
                        pltpu.VMEM((2, nc, nh // 2, 8, 2 * hd), F32)],
        compiler_params=_cparams(("arbitrary", "arbitrary", "arbitrary")),
        name="gdn_scan",
    )(q, k, v, gcol, grow)


SWA_TQ = 128
SWA_TK = SWA_TQ + 2 * BAND_RADIUS
SWA_HALO = BAND_RADIUS * max(DILATIONS)
COPY_ROWS = 512
SWA_TILES_PER_STEP = 4


def _swa_kernel(q_ref, k_ref, v_ref, bias_ref, o_ref,
                kn_ref, vp_ref, m0_ref, m1_ref, l_ref, acc_ref, bias_scr, *, seq, qb):
    qi = pl.program_id(2)
    qn_ref = q_ref.at[0]
    lane = lax.broadcasted_iota(jnp.int32, (1, LANES), 1)
    low = lane < SWA_HEAD_DIM
    nt = (((1,), (1,)), ((), ()))

    @pl.when(qi == 0)
    def _():
        for pi in range(len(DILATIONS)):
            for hh in range(2):
                full = jnp.broadcast_to(bias_ref[0, pi, hh], (SWA_TQ, SWA_TK))
                bias_scr[pi, hh] = pltpu.roll(full, 0, 1, stride=1, stride_axis=0)

        zeros = jnp.zeros((SWA_HALO, LANES), F32)
        for ref in (kn_ref, vp_ref):
            ref[0:SWA_HALO, :] = zeros
            ref[SWA_HALO + seq:2 * SWA_HALO + seq, :] = zeros

        def copy_rows(i, carry):
            r0 = pl.multiple_of(i * COPY_ROWS, COPY_ROWS)
            kn_ref[pl.ds(SWA_HALO + r0, COPY_ROWS), :] = k_ref[0, pl.ds(r0, COPY_ROWS), :]
            vp_ref[pl.ds(SWA_HALO + r0, COPY_ROWS), :] = v_ref[0, pl.ds(r0, COPY_ROWS), :]
            return carry

        lax.fori_loop(0, seq // COPY_ROWS, copy_rows, 0)

    kcol = lax.broadcasted_iota(jnp.int32, (1, SWA_TK), 1)

    def rows(ref, start, size, stride):
        if stride == 1:
            return ref[pl.ds(start, size), :]
        return ref[pl.ds(start, size, stride=stride), :]

    def put(ref, start, size, stride, val):
        if stride == 1:
            ref[pl.ds(start, size), :] = val
        else:
            ref[pl.ds(start, size, stride=stride), :] = val

    def logits(pi, dil, g):
        sub_len = seq // dil
        tiles_per_res = qb // dil // SWA_TQ
        ss = []
        for u in range(SWA_TILES_PER_STEP):
            res, t = divmod(g * SWA_TILES_PER_STEP + u, tiles_per_res)
            tau0 = qi * (qb // dil) + t * SWA_TQ
            loc0 = res + dil * (t * SWA_TQ)
            krow = SWA_HALO + qi * qb + loc0 - dil * BAND_RADIUS
            qt = rows(qn_ref, qi * qb + loc0, SWA_TQ, dil)
            kt = rows(kn_ref, krow, SWA_TK, dil).astype(BF16)
            at_end = t in (0, tiles_per_res - 1)
            kidx = kcol + (tau0 - BAND_RADIUS)
            kvalid = (kidx >= 0) & (kidx < sub_len)
            for hh in range(2):
                mine = low if hh == 0 else jnp.logical_not(low)
                qh = jnp.where(mine, qt, 0.0).astype(BF16)
                s = lax.dot_general(qh, kt, nt, preferred_element_type=F32) + bias_scr[pi, hh]
                ss.append(jnp.where(kvalid, s, NEG_BIG) if at_end else s)
        return ss

    def accumulate(dil, g, ss, first):
        tiles_per_res = qb // dil // SWA_TQ
        loc, vts, old = [], [], []
        for u in range(SWA_TILES_PER_STEP):
            res, t = divmod(g * SWA_TILES_PER_STEP + u, tiles_per_res)
            loc0 = res + dil * (t * SWA_TQ)
            vt = rows(vp_ref, SWA_HALO + qi * qb + loc0 - dil * BAND_RADIUS, SWA_TK, dil)
            vts.append((jnp.where(low, vt, 1.0).astype(BF16), jnp.where(low, 1.0, vt).astype(BF16)))
            loc.append(loc0)
            if not first:
                old.append(tuple(rows(ref, loc0, SWA_TQ, dil) for ref in (m0_ref, m1_ref, l_ref, acc_ref)))
        m_new, alpha, ps = [], [], []
        for n, s in enumerate(ss):
            u, hh = divmod(n, 2)
            mt = jnp.max(s, axis=-1, keepdims=True)
            if first:
                mn = jnp.broadcast_to(mt, (SWA_TQ, LANES))
            else:
                mo = old[u][hh]
                mn = jnp.maximum(mo, mt)
                alpha.append(jnp.exp2(mo - mn))
            m_new.append(mn)
            ps.append(jnp.exp2(s - jnp.concatenate([mn, mn], axis=1)).astype(BF16))
        pv = [_dot(p, vts[n // 2][n % 2]) for n, p in enumerate(ps)]
        for u in range(SWA_TILES_PER_STEP):
            pv0, pv1 = pv[2 * u], pv[2 * u + 1]
            put(m0_ref, loc[u], SWA_TQ, dil, m_new[2 * u])
            put(m1_ref, loc[u], SWA_TQ, dil, m_new[2 * u + 1])
            lsum = jnp.where(low, pv1, pv0)
            pvv = jnp.where(low, pv0, pv1)
            if first:
                put(l_ref, loc[u], SWA_TQ, dil, lsum)
                put(acc_ref, loc[u], SWA_TQ, dil, pvv)
            else:
                a0, a1 = alpha[2 * u], alpha[2 * u + 1]
                put(l_ref, loc[u], SWA_TQ, dil, old[u][2] * jnp.where(low, a1, a0) + lsum)
                put(acc_ref, loc[u], SWA_TQ, dil, old[u][3] * jnp.where(low, a0, a1) + pvv)

    order = sorted(range(len(DILATIONS)), key=lambda pi: -DILATIONS[pi])
    units = [(pi, DILATIONS[pi], g) for pi in order for g in range(qb // SWA_TQ // SWA_TILES_PER_STEP)]
    pending = logits(*units[0])
    for k, (pi, dil, g) in enumerate(units):
        nxt = logits(*units[k + 1]) if k + 1 < len(units) else None
        accumulate(dil, g, pending, first=pi == order[0])
        pending = nxt

    o_ref[0] = acc_ref[...] / pltpu.roll(l_ref[...], SWA_HEAD_DIM, 1)


def _t5_bucket(rel):
    nb = REL_BUCKETS // 2
    bucket = (rel > 0).astype(np.int32) * nb
    n = np.abs(rel)
    max_exact = nb // 2
    large = max_exact + (np.log(np.maximum(n, 1) / max_exact)
                         / math.log(REL_MAX_DISTANCE / max_exact) * (nb - max_exact)).astype(np.int32)
    large = np.minimum(large, nb - 1)
    return (bucket + np.where(n < max_exact, n, large)).astype(np.int32)


def _band_bias(rel_bias):
    rel = np.arange(2 * BAND_RADIUS + 1) - BAND_RADIUS
    rows = []
    for dil in DILATIONS:
        inside = jnp.transpose(rel_bias[_t5_bucket(rel * dil)]) * LOG2E
        rows.append(jnp.pad(inside, ((0, 0), (0, SWA_TK - inside.shape[1])), constant_values=NEG_BIG))
    bias = jnp.stack(rows, axis=0)
    bias = bias.reshape(len(DILATIONS), SWA_HEADS // 2, 2, 1, SWA_TK)
    return jnp.transpose(bias, (1, 0, 2, 3, 4)).astype(F32)


def _swa(qkvb, bias, *, qb):
    b, s, _ = qkvb.shape
    pairs = SWA_HEADS // 2
    assert s % qb == 0 and qb % (max(DILATIONS) * SWA_TQ) == 0
    col = lambda base: pl.BlockSpec((1, s, LANES), lambda i, p, j: (i, 0, base + p))
    padded = s + 2 * SWA_HALO
    return pl.pallas_call(
        functools.partial(_swa_kernel, seq=s, qb=qb),
        grid=(b, pairs, s // qb),
        in_specs=[col(0), col(pairs), col(2 * pairs),
                  pl.BlockSpec((1,) + bias.shape[1:], lambda i, p, j: (p, 0, 0, 0, 0))],
        out_specs=pl.BlockSpec((1, qb, LANES), lambda i, p, j: (i, j, p)),
        out_shape=jax.ShapeDtypeStruct((b, s, SWA_WIDTH), F32),
        scratch_shapes=[pltpu.VMEM((padded, LANES), F32), pltpu.VMEM((padded, LANES), F32)]
                       + [pltpu.VMEM((qb, LANES), F32)] * 4
                       + [pltpu.VMEM((len(DILATIONS), 2, SWA_TQ, SWA_TK), F32)],
        compiler_params=_cparams(("arbitrary", "arbitrary", "arbitrary")),
        name="swa",
    )(qkvb, qkvb, qkvb, bias)


FFN_TM = 512
FFN_FCHUNK = 256
SCAN_TS = 1024
SWA_QB = 2048


def _lane_row(vals, copies_at):
    row = jnp.zeros((LANES,), F32)
    for off in copies_at:
        row = row.at[off:off + vals.shape[0]].set(vals)
    return row[None, :]


def kernel(x, ffn1_norm, ffn1_w_gate, ffn1_w_up, ffn1_w_down, mix_norm, w_in, conv_w, a_log, dt_bias, gdn_norm_w, q_norm_w, k_norm_w, rel_bias, w_out, ffn2_norm, ffn2_w_gate, ffn2_w_up, ffn2_w_down, final_norm):
    b, s, d = x.shape
    n = b * s
    x2d = x.reshape(n, d)
    ng = 2 * GDN_HEADS
    c_z = 3 * GDN_WIDTH
    c_a = c_z + GDN_WIDTH
    c_b = c_a + ng
    c_qkvb = c_b + ng
    bias = _band_bias(rel_bias)
    for l in range(ffn1_norm.shape[0]):
        wl = w_in[l]
        w_a = wl[:, c_a:c_b]
        w_gates = jnp.concatenate([w_a, wl[:, c_b:c_qkvb], w_a, w_a, jnp.zeros((d, LANES - 4 * ng), F32)], axis=1)
        win = jnp.concatenate([wl[:, :c_a], w_gates, wl[:, c_qkvb:]], axis=1).astype(BF16)
        copies = (0, 2 * ng, 3 * ng)
        alog = _lane_row(a_log[l].reshape(ng), copies)
        dtb = _lane_row(dt_bias[l].reshape(ng), copies)
        x1, z, qkvb, q, k, v, gcol, grow = _ffn1_proj(
            x2d, ffn1_norm[l][None, :], ffn1_w_gate[l].astype(BF16), ffn1_w_up[l].astype(BF16),
            ffn1_w_down[l].astype(BF16), mix_norm[l][None, :], win,
            jnp.tile(q_norm_w[l], 2)[None, :], jnp.tile(k_norm_w[l], 2)[None, :],
            jnp.transpose(conv_w[l]), alog, dtb, batch=b, tm=FFN_TM, fchunk=FFN_FCHUNK)
        o_dirs = _gdn_scan(q, k, v, gcol.reshape(b, s, LANES), grow.reshape(b, s // CHUNK, 8, 2 * CHUNK), ts=SCAN_TS)
        attn = _swa(qkvb.reshape(b, s, -1), bias, qb=SWA_QB)
        x2d = _out_ffn2(
            x1, o_dirs.reshape(2, n, -1), z, attn.reshape(n, -1),
            gdn_norm_w[l][None, :], w_out[l].astype(BF16), ffn2_norm[l][None, :],
            ffn2_w_gate[l].astype(BF16), ffn2_w_up[l].astype(BF16), ffn2_w_down[l].astype(BF16),
            final_norm[l][None, :], tm=FFN_TM, fchunk=FFN_FCHUNK)
    return x2d.reshape(b, s, d)
```

```python
import functools
import math

import numpy as np
import jax
import jax.numpy as jnp
from jax import lax
from jax.experimental import pallas as pl
from jax.experimental.pallas import tpu as pltpu

F32 = jnp.float32
BF16 = jnp.bfloat16
EPS = 1e-6
NEG_BIG = -1e30
LOG2E = math.log2(math.e)

LANES = 128
GDN_HEADS = 4
GDN_HEAD_DIM = 128
GDN_WIDTH = GDN_HEADS * GDN_HEAD_DIM
CONV_WIDTH = 5
CHUNK = 64
SWA_HEADS = 8
SWA_HEAD_DIM = 64
SWA_WIDTH = SWA_HEADS * SWA_HEAD_DIM
DILATIONS = (1, 4, 16)
BAND_RADIUS = 64
REL_BUCKETS = 32
REL_MAX_DISTANCE = 1024
VMEM_LIMIT = 56 * 1024 * 1024


def _cparams(sem):
    return pltpu.CompilerParams(dimension_semantics=sem, vmem_limit_bytes=VMEM_LIMIT)


def _resident(shape):
    zeros = (0,) * len(shape)
    return pl.BlockSpec(shape, lambda *_: zeros, pipeline_mode=pl.Buffered(1))


def _rms(x, w):
    return x * lax.rsqrt(jnp.mean(x * x, axis=-1, keepdims=True) + EPS) * w


def _silu(x):
    return x * (1.0 / (1.0 + jnp.exp(-x)))


def _dot(a, b):
    return jnp.dot(a, b, preferred_element_type=F32)


def _zero_after(values):
    s = values[0:8]
    for r in range(8, values.shape[0], 8):
        s = s + values[r:r + 8]
    bits = lax.bitcast_convert_type(s, jnp.int32)
    return lax.shift_right_logical(lax.shift_right_logical(bits, 16), 16).astype(F32)


def _swiglu(h, wg_ref, wu_ref, wd_ref, fchunk, order_after=()):
    acc = None
    for n, c0 in enumerate(range(0, wg_ref.shape[1], fchunk)):
        g = _dot(h, wg_ref[:, c0:c0 + fchunk])
        if n < len(order_after):
            first = jnp.concatenate([g[0:8, 0:LANES] + order_after[n], g[0:8, LANES:]], axis=1)
            g = jnp.concatenate([first, g[8:]], axis=0)
        u = _dot(h, wu_ref[:, c0:c0 + fchunk])
        a = (_silu(g) * u).astype(BF16)
        d = _dot(a, wd_ref[c0:c0 + fchunk, :])
        acc = d if acc is None else acc + d
    return acc


def _ffn1_proj_kernel(x_ref, n1_ref, wg_ref, wu_ref, wd_ref, nm_ref, win_ref, qw_ref, kw_ref,
                      cw_ref, alog_ref, dtb_ref,
                      x1_ref, z_ref, qkvb_ref, q_ref, k_ref, v_ref, gcol_ref, grow_ref, pad_ref, tail_ref,
                      *, fchunk, tiles_per_seq, num_tiles):
    i = pl.program_id(0)
    tm = x_ref.shape[0]
    heads_out = (q_ref, k_ref, v_ref)

    @pl.when(i == 0)
    def _():
        pad_ref[...] = jnp.zeros(pad_ref.shape, F32)

    def conv_tail(cb, next_rows):
        tail_ref[cb, 0:8 + CONV_TAIL, :] = pad_ref[cb, tm - CONV_TAIL:8 + tm, :]
        tail_ref[cb, 8 + CONV_TAIL:16 + CONV_TAIL, :] = next_rows
        _gdn_conv_head(tail_ref.at[cb], cw_ref, cb, heads_out[cb // GDN_HEADS], tm - CONV_TAIL, CONV_TAIL)

    @pl.when(i == num_tiles)
    def _():
        for cb in range(3 * GDN_HEADS):
            _gdn_conv_head(pad_ref.at[cb], cw_ref, cb, heads_out[cb // GDN_HEADS], 0, tm - CONV_TAIL)
            conv_tail(cb, jnp.zeros((8, LANES), F32))

    @pl.when(i < num_tiles)
    def _():
        _ffn1_proj_tile(i, x_ref, n1_ref, wg_ref, wu_ref, wd_ref, nm_ref, win_ref, qw_ref, kw_ref,
                        cw_ref, alog_ref, dtb_ref, x1_ref, z_ref, qkvb_ref, heads_out, gcol_ref, grow_ref,
                        pad_ref, conv_tail, fchunk, tiles_per_seq)


def _ffn1_proj_tile(i, x_ref, n1_ref, wg_ref, wu_ref, wd_ref, nm_ref, win_ref, qw_ref, kw_ref,
                    cw_ref, alog_ref, dtb_ref, x1_ref, z_ref, qkvb_ref, heads_out, gcol_ref, grow_ref,
                    pad_ref, conv_tail, fchunk, tiles_per_seq):
    tm = x_ref.shape[0]
    conv_done = []
    for cb in range(3 * GDN_HEADS):
        even, odd = _gdn_conv_head(pad_ref.at[cb], cw_ref, cb, heads_out[cb // GDN_HEADS], 0, tm - CONV_TAIL)
        conv_done.append(_zero_after(even + odd))
    nchunks = wg_ref.shape[1] // fchunk
    order_after = conv_done[:nchunks - 1] + [functools.reduce(lambda a, b: a + b, conv_done[nchunks - 1:])]

    seq_start = lax.rem(i, tiles_per_seq) == 0
    x = x_ref[...]
    h = _rms(x, n1_ref[...]).astype(BF16)
    x1 = x + 0.5 * _swiglu(h, wg_ref, wu_ref, wd_ref, fchunk, order_after)
    x1_ref[...] = x1
    h2 = _rms(x1, nm_ref[...]).astype(BF16)
    c = 3 * GDN_WIDTH
    qkva = _dot(h2, win_ref[:, :c])
    z_ref[...] = _dot(h2, win_ref[:, c:c + GDN_WIDTH])
    c += GDN_WIDTH
    _gdn_gates(_dot(h2, win_ref[:, c:c + LANES]), alog_ref[...], dtb_ref[...], gcol_ref, grow_ref)
    c += LANES

    for cb in range(3 * GDN_HEADS):
        cs = slice(cb * LANES, (cb + 1) * LANES)
        conv_tail(cb, jnp.where(seq_start, 0.0, qkva[0:8, cs]))
        pad_ref[cb, 0:8, :] = jnp.where(seq_start, 0.0, pad_ref[cb, tm:tm + 8, :])
        pad_ref[cb, 8:8 + tm, :] = qkva[:, cs]

    low = lax.broadcasted_iota(jnp.int32, (1, LANES), 1) < SWA_HEAD_DIM

    def head_rms(y, w):
        y2 = y * y
        s0 = jnp.sum(jnp.where(low, y2, 0.0), axis=-1, keepdims=True)
        s1 = jnp.sum(jnp.where(low, 0.0, y2), axis=-1, keepdims=True)
        ms = jnp.where(low, s0, s1) * (1.0 / SWA_HEAD_DIM)
        return y * lax.rsqrt(ms + EPS) * w

    qscale = qw_ref[...] * (SWA_HEAD_DIM ** -0.5 * LOG2E)
    wide = 2 * LANES
    for j in range(3 * SWA_WIDTH // wide):
        y = _dot(h2, win_ref[:, c + j * wide:c + (j + 1) * wide])
        for half in range(2):
            p = 2 * j + half
            yp = y[:, half * LANES:(half + 1) * LANES]
            if p < SWA_WIDTH // LANES:
                yp = head_rms(yp, qscale)
            elif p < 2 * SWA_WIDTH // LANES:
                yp = head_rms(yp, kw_ref[...])
            qkvb_ref[:, p * LANES:(p + 1) * LANES] = yp


def _ffn1_proj(x2d, n1, wg, wu, wd, nm, win, qw, kw, cw, alog, dtb, *, batch, tm, fchunk):
    n, d = x2d.shape
    f = wg.shape[1]
    seq = n // batch
    assert win.shape[1] == 4 * GDN_WIDTH + LANES + 3 * SWA_WIDTH and seq % tm == 0
    nt = n // tm
    tps = seq // tm
    cur = lambda i: jnp.minimum(i, nt - 1)
    late = lambda i: jnp.maximum(i - 1, 0)
    row = lambda w: pl.BlockSpec((tm, w), lambda i: (cur(i), 0))
    heads = pl.BlockSpec((1, GDN_HEADS, tm, GDN_HEAD_DIM), lambda i: (late(i) // tps, 0, late(i) % tps, 0))
    f32 = lambda *shape: jax.ShapeDtypeStruct(shape, F32)
    return pl.pallas_call(
        functools.partial(_ffn1_proj_kernel, fchunk=fchunk, tiles_per_seq=tps, num_tiles=nt),
        grid=(nt + 1,),
        in_specs=[row(d), _resident((1, d)), _resident((d, f)), _resident((d, f)), _resident((f, d)),
                  _resident((1, d)), _resident(win.shape), _resident((1, LANES)), _resident((1, LANES)),
                  _resident(cw.shape), _resident((1, LANES)), _resident((1, LANES))],
        out_specs=[row(d), row(GDN_WIDTH), row(3 * SWA_WIDTH), heads, heads, heads, row(LANES),
                   pl.BlockSpec((tm // CHUNK, 8, 2 * CHUNK), lambda i: (cur(i), 0, 0))],
        out_shape=[f32(n, d), f32(n, GDN_WIDTH), f32(n, 3 * SWA_WIDTH)]
                  + [f32(batch, GDN_HEADS, seq, GDN_HEAD_DIM)] * 3 + [f32(n, LANES), f32(n // CHUNK, 8, 2 * CHUNK)],
        scratch_shapes=[pltpu.VMEM((3 * GDN_HEADS, tm + 8, LANES), F32),
                        pltpu.VMEM((3 * GDN_HEADS, CONV_TAIL + 16, LANES), F32)],
        compiler_params=_cparams(("arbitrary",)),
        name="ffn1_proj",
    )(x2d, n1, wg, wu, wd, nm, win, qw, kw, cw, alog, dtb)


def _out_ffn2_kernel(x1_ref, of_ref, ob_ref, z_ref, attn_ref, gnw_ref, wout_ref, n2_ref,
                     wg_ref, wu_ref, wd_ref, nf_ref, out_ref, *, fchunk):
    o = of_ref[...] + ob_ref[...]
    z = z_ref[...]
    gnw = gnw_ref[...]
    heads = []
    for h in range(GDN_HEADS):
        sl = slice(h * GDN_HEAD_DIM, (h + 1) * GDN_HEAD_DIM)
        heads.append(_rms(o[:, sl], gnw) * _silu(z[:, sl]))
    oa = jnp.concatenate(heads, axis=1).astype(BF16)
    mix = _dot(oa, wout_ref[:GDN_WIDTH, :]) + _dot(attn_ref[...].astype(BF16), wout_ref[GDN_WIDTH:, :])
    x2 = x1_ref[...] + mix
    h2 = _rms(x2, n2_ref[...]).astype(BF16)
    x3 = x2 + 0.5 * _swiglu(h2, wg_ref, wu_ref, wd_ref, fchunk)
    out_ref[...] = _rms(x3, nf_ref[...])


def _out_ffn2(x1, o_fwd, o_bwd, z, attn, gnw, wout, n2, wg, wu, wd, nf, *, tm, fchunk):
    n, d = x1.shape
    f = wg.shape[1]
    row = lambda w: pl.BlockSpec((tm, w), lambda i: (i, 0))
    return pl.pallas_call(
        functools.partial(_out_ffn2_kernel, fchunk=fchunk),
        grid=(n // tm,),
        in_specs=[row(d), row(GDN_WIDTH), row(GDN_WIDTH), row(GDN_WIDTH), row(SWA_WIDTH),
                  _resident((1, GDN_HEAD_DIM)), _resident(wout.shape), _resident((1, d)),
                  _resident((d, f)), _resident((d, f)), _resident((f, d)), _resident((1, d))],
        out_specs=row(d),
        out_shape=jax.ShapeDtypeStruct((n, d), F32),
        compiler_params=_cparams(("arbitrary",)),
        name="out_ffn2",
    )(x1, o_fwd, o_bwd, z, attn, gnw, wout, n2, wg, wu, wd, nf)


CONV_TAIL = 16


def _gdn_conv_head(src, cw_ref, cb, dst, out0, count):
    cs = slice(cb * LANES, (cb + 1) * LANES)
    half = count // 2
    done = []
    for par in range(2):
        acc = None
        for t in range(CONV_WIDTH):
            term = src[pl.ds(6 + t + par, half, stride=2), :] * cw_ref[t:t + 1, cs]
            acc = term if acc is None else acc + term
        y = _silu(acc)
        if cb < 2 * GDN_HEADS:
            y = y * lax.rsqrt(jnp.sum(y * y, axis=-1, keepdims=True) + EPS)
        if cb < GDN_HEADS:
            y = y * (GDN_HEAD_DIM ** -0.5)
        dst[0, cb % GDN_HEADS, pl.ds(out0 + par, half, stride=2), :] = y
        done.append(y)
    return done


def _gdn_gates(ab, alog, dtb, gcol_ref, grow_ref):
    ts = ab.shape[0]
    xs = ab + dtb
    softplus = jnp.maximum(xs, 0.0) + jnp.log(1.0 + jnp.exp(-jnp.abs(xs)))
    g = -jnp.exp(alog) * softplus
    beta = 1.0 / (1.0 + jnp.exp(-ab))
    g1 = g.astype(BF16)
    r1 = g - g1.astype(F32)
    g2 = r1.astype(BF16)
    g3 = (r1 - g2.astype(F32)).astype(BF16)
    ri = lax.broadcasted_iota(jnp.int32, (LANES, LANES), 0)
    ci = lax.broadcasted_iota(jnp.int32, (LANES, LANES), 1)
    same = (ri // CHUNK) == (ci // CHUNK)
    lower = jnp.where(same & (ri >= ci), 1.0, 0.0).astype(BF16)
    upper = jnp.where(same & (ri <= ci), 1.0, 0.0).astype(BF16)
    lane = lax.broadcasted_iota(jnp.int32, (1, LANES), 1)
    for s in range(ts // LANES):
        rs = slice(s * LANES, (s + 1) * LANES)
        pre = _dot(lower, g1[rs]) + _dot(lower, g2[rs]) + _dot(lower, g3[rs])
        suf = _dot(upper, g1[rs]) + _dot(upper, g2[rs]) + _dot(upper, g3[rs])
        gs = g[rs]
        col = jnp.where(lane < 4, pre,
              jnp.where(lane < 8, suf,
              jnp.where(lane < 16, beta[rs],
              jnp.where(lane < 20, suf - gs,
              jnp.where(lane < 24, pre - gs, pre + suf - gs)))))
        gcol_ref[rs, :] = col
        rows = col.T[0:8]
        swapped = pltpu.roll(rows, CHUNK, 1)
        grow_ref[2 * s] = jnp.where(lane < CHUNK, rows, swapped)
        grow_ref[2 * s + 1] = jnp.where(lane < CHUNK, swapped, rows)


CHUNKS_PER_STEP = 2

def _gdn_scan_kernel(*refs, nc, nblk):
    ins, (of_ref, ob_ref, s_ref), scr = refs[:10], refs[10:13], refs[13:]
    j = pl.program_id(1)
    slot_build = lax.rem(j, 2)
    slot_scan = 1 - slot_build

    ri = lax.broadcasted_iota(jnp.int32, (CHUNK, 2 * CHUNK), 0)
    li = lax.broadcasted_iota(jnp.int32, (CHUNK, 2 * CHUNK), 1)
    left = li < CHUNK
    eye_lo = jnp.where(li == ri, 1.0, 0.0)
    eye_hi = jnp.where(li == ri + CHUNK, 1.0, 0.0)
    nt = (((1,), (1,)), ((), ()))
    zc = jnp.zeros((CHUNK, GDN_HEAD_DIM), BF16)
    zs = jnp.zeros((GDN_HEAD_DIM, GDN_HEAD_DIM), BF16)
    npairs = GDN_HEADS // 2

    def block_diag(a, b, zero):
        return jnp.concatenate([jnp.concatenate([a, zero], axis=1), jnp.concatenate([zero, b], axis=1)], axis=0)

    nsq = int(math.log2(CHUNK))

    def chunk_matrices(ci, d):
        q_ref, k_ref, v_ref, gcol_ref, grow_ref = ins[5 * d:5 * d + 5]
        wqg_ref, u_ref, ik_ref, eg_ref = scr[4 * d:4 * d + 4]
        dd = (ri - jnp.where(left, li, li - CHUNK)) * (1 - 2 * d)
        incl = dd >= 0
        strict = dd > 0
        probs = [(ci * CHUNKS_PER_STEP + cc, p) for cc in range(CHUNKS_PER_STEP) for p in range(npairs)]
        qgs, kds, egs, rs, xa, xb, intras = [], [], [], [], [], [], []
        for c, p in probs:
            r0 = pl.multiple_of(c * CHUNK, CHUNK)
            gates = gcol_ref[0, pl.ds(r0, CHUNK), :]
            rows = grow_ref[0, c]
            per_head = []
            for h in (2 * p, 2 * p + 1):
                col = GDN_HEADS * d + h
                gc, beta, gdec, gl = (gates[:, base + col:base + col + 1] for base in (0, 8, 16, 24))
                grow = rows[col:col + 1, :]
                qc = q_ref[0, h, pl.ds(r0, CHUNK), :]
                kc = k_ref[0, h, pl.ds(r0, CHUNK), :]
                vc = v_ref[0, h, pl.ds(r0, CHUNK), :]
                kb = kc * beta
                eg = jnp.exp(gc)
                per_head.append(dict(gc=gc, grow=grow, q=qc, k=kc, kb=kb, r=jnp.concatenate([vc * beta, kb * eg], axis=1),
                                     qg=qc * eg, kd=kc * jnp.exp(gdec), eg=jnp.exp(gl[0:1, :])))
            a, b = per_head
            dec = jnp.where(incl, jnp.exp(jnp.where(incl, jnp.where(left, a["gc"], b["gc"])
                                                    - jnp.where(left[0:1], a["grow"], b["grow"]), 0.0)), 0.0)
            lhs = jnp.concatenate([jnp.concatenate([a["kb"], b["kb"]], axis=1),
                                   jnp.concatenate([a["q"], b["q"]], axis=1)], axis=0).astype(BF16)
            qk = lax.dot_general(lhs, block_diag(a["k"].astype(BF16), b["k"].astype(BF16), zc), nt,
                                 preferred_element_type=F32)
            neg_a = -jnp.where(strict, qk[:CHUNK] * dec, 0.0)
            xa.append(jnp.where(left, neg_a, eye_hi))
            xb.append(jnp.where(left, eye_lo, neg_a))
            intras.append(qk[CHUNK:] * dec)
            rs.append((a["r"], b["r"]))
            qgs.append((a["qg"], b["qg"]))
            kds.append(jnp.concatenate([a["kd"], b["kd"]], axis=0))
            egs.append(jnp.concatenate([jnp.broadcast_to(a["eg"], (8, GDN_HEAD_DIM)),
                                        jnp.broadcast_to(b["eg"], (8, GDN_HEAD_DIM))], axis=1))
        yield
        for t in range(nsq):
            for n in range(len(probs)):
                ya = xa[n].astype(BF16)
                yb = xb[n].astype(BF16)
                zz = jnp.zeros((CHUNK, 2 * CHUNK), BF16)
                y = _dot(jnp.where(left, ya, yb), block_diag(ya, yb, zz))
                xa[n] = jnp.where(left, y[:, :2 * CHUNK], xa[n] + y[:, :2 * CHUNK])
                xb[n] = jnp.where(left, xb[n] + y[:, 2 * CHUNK:], y[:, 2 * CHUNK:])
            yield
        for n, (c, p) in enumerate(probs):
            t_off = jnp.where(left, xb[n] - eye_lo, xa[n] - eye_hi).astype(BF16)
            ra, rb = rs[n]
            zr = jnp.zeros((CHUNK, 2 * GDN_HEAD_DIM), BF16)
            out = _dot(t_off, block_diag(rb.astype(BF16), ra.astype(BF16), zr))
            rb = rb + out[:, :2 * GDN_HEAD_DIM]
            ra = ra + out[:, 2 * GDN_HEAD_DIM:]
            ua, wa = ra[:, :GDN_HEAD_DIM], ra[:, GDN_HEAD_DIM:]
            ub, wb = rb[:, :GDN_HEAD_DIM], rb[:, GDN_HEAD_DIM:]
            u_ref[slot_build, c, p] = jnp.concatenate([ua, ub], axis=1)
            wqg_ref[slot_build, c, p] = jnp.concatenate(
                [jnp.concatenate([wa, wb], axis=1), jnp.concatenate(qgs[n], axis=1)], axis=0).astype(BF16)
            ik_ref[slot_build, c, p] = jnp.concatenate([intras[n], kds[n].T], axis=0).astype(BF16)
            eg_ref[slot_build, c, p] = egs[n]
        yield

    def scan_chunks(ci, d):
        wqg_ref, u_ref, ik_ref, eg_ref = scr[4 * d:4 * d + 4]
        o_ref = (of_ref, ob_ref)[d]
        for cc in range(CHUNKS_PER_STEP):
            i = ci * CHUNKS_PER_STEP + cc
            c = nc - 1 - i if d else i
            r0 = pl.multiple_of(c * CHUNK, CHUNK)
            states = [s_ref[d, p] for p in range(npairs)]
            m1 = [_dot(wqg_ref[slot_scan, c, p],
                       block_diag(states[p][:, :GDN_HEAD_DIM].astype(BF16), states[p][:, GDN_HEAD_DIM:].astype(BF16), zs))
                  for p in range(npairs)]
            yield
            v_new = [(u_ref[slot_scan, c, p] - m1[p][:CHUNK]).astype(BF16) for p in range(npairs)]
            m2 = [_dot(ik_ref[slot_scan, c, p], block_diag(v_new[p][:, :GDN_HEAD_DIM], v_new[p][:, GDN_HEAD_DIM:], zc))
                  for p in range(npairs)]
            for p in range(npairs):
                ps = slice(2 * p * GDN_HEAD_DIM, 2 * (p + 1) * GDN_HEAD_DIM)
                o_ref[0, pl.ds(r0, CHUNK), ps] = m1[p][CHUNK:] + m2[p][:CHUNK]
                s_ref[d, p] = states[p] * eg_ref[slot_scan, c, p][0:1, :] + m2[p][CHUNK:]
            yield

    def run(*makers):
        def body(ci, carry):
            live = [m(ci, d) for m in makers for d in range(2)]
            while live:
                live = [g for g in live if next(g, StopIteration) is not StopIteration]
            return carry
        lax.fori_loop(0, nc // CHUNKS_PER_STEP, body, 0)

    @pl.when(j == 0)
    def _():
        s_ref[...] = jnp.zeros(s_ref.shape, F32)
        run(chunk_matrices)

    @pl.when((j > 0) & (j < nblk))
    def _():
        run(chunk_matrices, scan_chunks)

    @pl.when(j == nblk)
    def _():
        run(scan_chunks)


def _gdn_scan(q, k, v, gcol, grow, *, ts):
    b, nh, s, hd = q.shape
    w = nh * hd
    nblk = s // ts
    nc = ts // CHUNK
    order = (lambda t: t, lambda t: nblk - 1 - t)
    in_specs, out_specs, scratch = [], [], [pltpu.VMEM((2, nh // 2, hd, 2 * hd), F32)]
    for d in range(2):
        src = lambda j, d=d: order[d](jnp.minimum(j, nblk - 1))
        dst = lambda j, d=d: order[d](jnp.maximum(j - 1, 0))
        in_specs += [pl.BlockSpec((1, nh, ts, hd), lambda i, j, src=src: (i, 0, src(j), 0))] * 3
        in_specs += [pl.BlockSpec((1, ts, LANES), lambda i, j, src=src: (i, src(j), 0)),
                     pl.BlockSpec((1, nc, 8, 2 * CHUNK), lambda i, j, src=src: (i, src(j), 0, 0))]
        out_specs.append(pl.BlockSpec((1, ts, w), lambda i, j, dst=dst: (i, dst(j), 0)))
        scratch += [pltpu.VMEM((2, nc, nh // 2, 2 * CHUNK, 2 * hd), BF16),
                    pltpu.VMEM((2, nc, nh // 2, CHUNK, 2 * hd), F32),
                    pltpu.VMEM((2, nc, nh // 2, CHUNK + hd, 2 * CHUNK), BF16),
                    pltpu.VMEM((2, nc, nh // 2, 8, 2 * hd), F32)]
    return pl.pallas_call(
        functools.partial(_gdn_scan_kernel, nc=nc, nblk=nblk),
        grid=(b, nblk + 1),
        in_specs=in_specs,
        out_specs=out_specs,
        out_shape=[jax.ShapeDtypeStruct((b, s, w), F32)] * 2,
        scratch_shapes=scratch,
        compiler_params=_cparams(("arbitrary", "arbitrary")),
        name="gdn_scan",
    )(*([q, k, v, gcol, grow] * 2))


SWA_TQ = 128
SWA_TK = SWA_TQ + 2 * BAND_RADIUS
SWA_HALO = BAND_RADIUS * max(DILATIONS)
COPY_ROWS = 512
SWA_TILES_PER_STEP = 4


def _swa_kernel(q_ref, k_ref, v_ref, bias_ref, o_ref,
                kn_ref, vp_ref, m0_ref, m1_ref, l_ref, acc_ref, bias_scr, *, seq, qb):
    qi = pl.program_id(2)
    qn_ref = q_ref.at[0]
    lane = lax.broadcasted_iota(jnp.int32, (1, LANES), 1)
    low = lane < SWA_HEAD_DIM
    nt = (((1,), (1,)), ((), ()))

    @pl.when(qi == 0)
    def _():
        for pi in range(len(DILATIONS)):
            for hh in range(2):
                full = jnp.broadcast_to(bias_ref[0, pi, hh], (SWA_TQ, SWA_TK))
                bias_scr[pi, hh] = pltpu.roll(full, 0, 1, stride=1, stride_axis=0)

        zeros = jnp.zeros((SWA_HALO, LANES), F32)
        for ref in (kn_ref, vp_ref):
            ref[0:SWA_HALO, :] = zeros
            ref[SWA_HALO + seq:2 * SWA_HALO + seq, :] = zeros

        def copy_rows(i, carry):
            r0 = pl.multiple_of(i * COPY_ROWS, COPY_ROWS)
            kn_ref[pl.ds(SWA_HALO + r0, COPY_ROWS), :] = k_ref[0, pl.ds(r0, COPY_ROWS), :]
            vp_ref[pl.ds(SWA_HALO + r0, COPY_ROWS), :] = v_ref[0, pl.ds(r0, COPY_ROWS), :]
            return carry

        lax.fori_loop(0, seq // COPY_ROWS, copy_rows, 0)

    kcol = lax.broadcasted_iota(jnp.int32, (1, SWA_TK), 1)

    def rows(ref, start, size, stride):
        if stride == 1:
            return ref[pl.ds(start, size), :]
        return ref[pl.ds(start, size, stride=stride), :]

    def put(ref, start, size, stride, val):
        if stride == 1:
            ref[pl.ds(start, size), :] = val
        else:
            ref[pl.ds(start, size, stride=stride), :] = val

    def logits(pi, dil, g):
        sub_len = seq // dil
        tiles_per_res = qb // dil // SWA_TQ
        ss = []
        for u in range(SWA_TILES_PER_STEP):
            res, t = divmod(g * SWA_TILES_PER_STEP + u, tiles_per_res)
            tau0 = qi * (qb // dil) + t * SWA_TQ
            loc0 = res + dil * (t * SWA_TQ)
            krow = SWA_HALO + qi * qb + loc0 - dil * BAND_RADIUS
            qt = rows(qn_ref, qi * qb + loc0, SWA_TQ, dil)
            kt = rows(kn_ref, krow, SWA_TK, dil).astype(BF16)
            at_end = t in (0, tiles_per_res - 1)
            kidx = kcol + (tau0 - BAND_RADIUS)
            kvalid = (kidx >= 0) & (kidx < sub_len)
            for hh in range(2):
                mine = low if hh == 0 else jnp.logical_not(low)
                qh = jnp.where(mine, qt, 0.0).astype(BF16)
                s = lax.dot_general(qh, kt, nt, preferred_element_type=F32) + bias_scr[pi, hh]
                ss.append(jnp.where(kvalid, s, NEG_BIG) if at_end else s)
        return ss

    def accumulate(dil, g, ss, first):
        tiles_per_res = qb // dil // SWA_TQ
        loc, vts, old = [], [], []
        for u in range(SWA_TILES_PER_STEP):
            res, t = divmod(g * SWA_TILES_PER_STEP + u, tiles_per_res)
            loc0 = res + dil * (t * SWA_TQ)
            vt = rows(vp_ref, SWA_HALO + qi * qb + loc0 - dil * BAND_RADIUS, SWA_TK, dil)
            vts.append((jnp.where(low, vt, 1.0).astype(BF16), jnp.where(low, 1.0, vt).astype(BF16)))
            loc.append(loc0)
            if not first:
                old.append(tuple(rows(ref, loc0, SWA_TQ, dil) for ref in (m0_ref, m1_ref, l_ref, acc_ref)))
        m_new, alpha, ps = [], [], []
        for n, s in enumerate(ss):
            u, hh = divmod(n, 2)
            mt = jnp.max(s, axis=-1, keepdims=True)
            if first:
                mn = jnp.broadcast_to(mt, (SWA_TQ, LANES))
            else:
                mo = old[u][hh]
                mn = jnp.maximum(mo, mt)
                alpha.append(jnp.exp2(mo - mn))
            m_new.append(mn)
            ps.append(jnp.exp2(s - jnp.concatenate([mn, mn], axis=1)).astype(BF16))
        pv = [_dot(p, vts[n // 2][n % 2]) for n, p in enumerate(ps)]
        for u in range(SWA_TILES_PER_STEP):
            pv0, pv1 = pv[2 * u], pv[2 * u + 1]
            put(m0_ref, loc[u], SWA_TQ, dil, m_new[2 * u])
            put(m1_ref, loc[u], SWA_TQ, dil, m_new[2 * u + 1])
            lsum = jnp.where(low, pv1, pv0)
            pvv = jnp.where(low, pv0, pv1)
            if first:
                put(l_ref, loc[u], SWA_TQ, dil, lsum)
                put(acc_ref, loc[u], SWA_TQ, dil, pvv)
            else:
                a0, a1 = alpha[2 * u], alpha[2 * u + 1]
                put(l_ref, loc[u], SWA_TQ, dil, old[u][2] * jnp.where(low, a1, a0) + lsum)
                put(acc_ref, loc[u], SWA_TQ, dil, old[u][3] * jnp.where(low, a0, a1) + pvv)

    order = sorted(range(len(DILATIONS)), key=lambda pi: -DILATIONS[pi])
    units = [(pi, DILATIONS[pi], g) for pi in order for g in range(qb // SWA_TQ // SWA_TILES_PER_STEP)]
    pending = logits(*units[0])
    for k, (pi, dil, g) in enumerate(units):
        nxt = logits(*units[k + 1]) if k + 1 < len(units) else None
        accumulate(dil, g, pending, first=pi == order[0])
        pending = nxt

    o_ref[0] = acc_ref[...] / pltpu.roll(l_ref[...], SWA_HEAD_DIM, 1)


def _t5_bucket(rel):
    nb = REL_BUCKETS // 2
    bucket = (rel > 0).astype(np.int32) * nb
    n = np.abs(rel)
    max_exact = nb // 2
    large = max_exact + (np.log(np.maximum(n, 1) / max_exact)
                         / math.log(REL_MAX_DISTANCE / max_exact) * (nb - max_exact)).astype(np.int32)
    large = np.minimum(large, nb - 1)
    return (bucket + np.where(n < max_exact, n, large)).astype(np.int32)


def _band_bias(rel_bias):
    rel = np.arange(2 * BAND_RADIUS + 1) - BAND_RADIUS
    rows = []
    for dil in DILATIONS:
        inside = jnp.transpose(rel_bias[_t5_bucket(rel * dil)]) * LOG2E
        rows.append(jnp.pad(inside, ((0, 0), (0, SWA_TK - inside.shape[1])), constant_values=NEG_BIG))
    bias = jnp.stack(rows, axis=0)
    bias = bias.reshape(len(DILATIONS), SWA_HEADS // 2, 2, 1, SWA_TK)
    return jnp.transpose(bias, (1, 0, 2, 3, 4)).astype(F32)


def _swa(qkvb, bias, *, qb):
    b, s, _ = qkvb.shape
    pairs = SWA_HEADS // 2
    assert s % qb == 0 and qb % (max(DILATIONS) * SWA_TQ) == 0
    col = lambda base: pl.BlockSpec((1, s, LANES), lambda i, p, j: (i, 0, base + p))
    padded = s + 2 * SWA_HALO
    return pl.pallas_call(
        functools.partial(_swa_kernel, seq=s, qb=qb),
        grid=(b, pairs, s // qb),
        in_specs=[col(0), col(pairs), col(2 * pairs),
                  pl.BlockSpec((1,) + bias.shape[1:], lambda i, p, j: (p, 0, 0, 0, 0))],
        out_specs=pl.BlockSpec((1, qb, LANES), lambda i, p, j: (i, j, p)),
        out_shape=jax.ShapeDtypeStruct((b, s, SWA_WIDTH), F32),
        scratch_shapes=[pltpu.VMEM((padded, LANES), F32), pltpu.VMEM((padded, LANES), F32)]
                       + [pltpu.VMEM((qb, LANES), F32)] * 4
                       + [pltpu.VMEM((len(DILATIONS), 2, SWA_TQ, SWA_TK), F32)],
        compiler_params=_cparams(("arbitrary", "arbitrary", "arbitrary")),
        name="swa",
    )(qkvb, qkvb, qkvb, bias)


FFN_TM = 512
FFN_FCHUNK = 256
SCAN_TS = 512
SWA_QB = 2048


def _lane_row(vals, copies_at):
    row = jnp.zeros((LANES,), F32)
    for off in copies_at:
        row = row.at[off:off + vals.shape[0]].set(vals)
    return row[None, :]


def kernel(x, ffn1_norm, ffn1_w_gate, ffn1_w_up, ffn1_w_down, mix_norm, w_in, conv_w, a_log, dt_bias, gdn_norm_w, q_norm_w, k_norm_w, rel_bias, w_out, ffn2_norm, ffn2_w_gate, ffn2_w_up, ffn2_w_down, final_norm):
    b, s, d = x.shape
    n = b * s
    x2d = x.reshape(n, d)
    ng = 2 * GDN_HEADS
    c_z = 3 * GDN_WIDTH
    c_a = c_z + GDN_WIDTH
    c_b = c_a + ng
    c_qkvb = c_b + ng
    bias = _band_bias(rel_bias)
    for l in range(ffn1_norm.shape[0]):
        wl = w_in[l]
        w_a = wl[:, c_a:c_b]
        w_gates = jnp.concatenate([w_a, wl[:, c_b:c_qkvb], w_a, w_a, jnp.zeros((d, LANES - 4 * ng), F32)], axis=1)
        win = jnp.concatenate([wl[:, :c_a], w_gates, wl[:, c_qkvb:]], axis=1).astype(BF16)
        copies = (0, 2 * ng, 3 * ng)
        alog = _lane_row(a_log[l].reshape(ng), copies)
        dtb = _lane_row(dt_bias[l].reshape(ng), copies)
        x1, z, qkvb, q, k, v, gcol, grow = _ffn1_proj(
            x2d, ffn1_norm[l][None, :], ffn1_w_gate[l].astype(BF16), ffn1_w_up[l].astype(BF16),
            ffn1_w_down[l].astype(BF16), mix_norm[l][None, :], win,
            jnp.tile(q_norm_w[l], 2)[None, :], jnp.tile(k_norm_w[l], 2)[None, :],
            jnp.transpose(conv_w[l]), alog, dtb, batch=b, tm=FFN_TM, fchunk=FFN_FCHUNK)
        o_fwd, o_bwd = _gdn_scan(q, k, v, gcol.reshape(b, s, LANES), grow.reshape(b, s // CHUNK, 8, 2 * CHUNK),
                                 ts=SCAN_TS)
        attn = _swa(qkvb.reshape(b, s, -1), bias, qb=SWA_QB)
        x2d = _out_ffn2(
            x1, o_fwd.reshape(n, -1), o_bwd.reshape(n, -1), z, attn.reshape(n, -1),
            gdn_norm_w[l][None, :], w_out[l].astype(BF16), ffn2_norm[l][None, :],
            ffn2_w_gate[l].astype(BF16), ffn2_w_up[l].astype(BF16), ffn2_w_down[l].astype(BF16),
            final_norm[l][None, :], tm=FFN_TM, fchunk=FFN_FCHUNK)
    return x2d.reshape(b, s, d)
```

```python
import functools
import math

import numpy as np
import jax
import jax.numpy as jnp
from jax import lax
from jax.experimental import pallas as pl
from jax.experimental.pallas import tpu as pltpu

F32 = jnp.float32
BF16 = jnp.bfloat16
EPS = 1e-6
NEG_BIG = -1e30
LOG2E = math.log2(math.e)

LANES = 128
GDN_HEADS = 4
GDN_HEAD_DIM = 128
GDN_WIDTH = GDN_HEADS * GDN_HEAD_DIM
CONV_WIDTH = 5
CHUNK = 64
SWA_HEADS = 8
SWA_HEAD_DIM = 64
SWA_WIDTH = SWA_HEADS * SWA_HEAD_DIM
DILATIONS = (1, 4, 16)
BAND_RADIUS = 64
REL_BUCKETS = 32
REL_MAX_DISTANCE = 1024
VMEM_LIMIT = 56 * 1024 * 1024


def _cparams(sem):
    return pltpu.CompilerParams(dimension_semantics=sem, vmem_limit_bytes=VMEM_LIMIT)


def _resident(shape):
    zeros = (0,) * len(shape)
    return pl.BlockSpec(shape, lambda *_: zeros, pipeline_mode=pl.Buffered(1))


def _rms(x, w):
    return x * lax.rsqrt(jnp.mean(x * x, axis=-1, keepdims=True) + EPS) * w


def _silu(x):
    return x * (1.0 / (1.0 + jnp.exp(-x)))


def _dot(a, b):
    return jnp.dot(a, b, preferred_element_type=F32)


def _zero_after(values):
    s = values[0:8]
    for r in range(8, values.shape[0], 8):
        s = s + values[r:r + 8]
    bits = lax.bitcast_convert_type(s, jnp.int32)
    return lax.shift_right_logical(lax.shift_right_logical(bits, 16), 16).astype(F32)


def _swiglu(h, wg_ref, wu_ref, wd_ref, fchunk, order_after=()):
    acc = None
    for n, c0 in enumerate(range(0, wg_ref.shape[1], fchunk)):
        g = _dot(h, wg_ref[:, c0:c0 + fchunk])
        if n < len(order_after):
            first = jnp.concatenate([g[0:8, 0:LANES] + order_after[n], g[0:8, LANES:]], axis=1)
            g = jnp.concatenate([first, g[8:]], axis=0)
        u = _dot(h, wu_ref[:, c0:c0 + fchunk])
        a = (_silu(g) * u).astype(BF16)
        d = _dot(a, wd_ref[c0:c0 + fchunk, :])
        acc = d if acc is None else acc + d
    return acc


def _ffn1_proj_kernel(x_ref, n1_ref, wg_ref, wu_ref, wd_ref, nm_ref, wa_ref, wgt_ref, wb_ref, qw_ref, kw_ref,
                      cw_ref, alog_ref, dtb_ref,
                      x1_ref, z_ref, qkvb_ref, q_ref, k_ref, v_ref, gcol_ref, grow_ref, pad_ref, tail_ref,
                      *, fchunk, tiles_per_seq, num_tiles):
    i = pl.program_id(0)
    tm = x_ref.shape[0]
    heads_out = (q_ref, k_ref, v_ref)

    @pl.when(i == 0)
    def _():
        pad_ref[...] = jnp.zeros(pad_ref.shape, F32)

    def conv_tail(cb, next_rows):
        tail_ref[cb, 0:8 + CONV_TAIL, :] = pad_ref[cb, tm - CONV_TAIL:8 + tm, :]
        tail_ref[cb, 8 + CONV_TAIL:16 + CONV_TAIL, :] = next_rows
        _gdn_conv_head(tail_ref.at[cb], cw_ref, cb, heads_out[cb // GDN_HEADS], tm - CONV_TAIL, CONV_TAIL)

    @pl.when(i == num_tiles)
    def _():
        for cb in range(3 * GDN_HEADS):
            _gdn_conv_head(pad_ref.at[cb], cw_ref, cb, heads_out[cb // GDN_HEADS], 0, tm - CONV_TAIL)
            conv_tail(cb, jnp.zeros((8, LANES), F32))

    @pl.when(i < num_tiles)
    def _():
        _ffn1_proj_tile(i, x_ref, n1_ref, wg_ref, wu_ref, wd_ref, nm_ref, (wa_ref, wgt_ref, wb_ref), qw_ref, kw_ref,
                        cw_ref, alog_ref, dtb_ref, x1_ref, z_ref, qkvb_ref, heads_out, gcol_ref, grow_ref,
                        pad_ref, conv_tail, fchunk, tiles_per_seq)


def _ffn1_proj_tile(i, x_ref, n1_ref, wg_ref, wu_ref, wd_ref, nm_ref, win_refs, qw_ref, kw_ref,
                    cw_ref, alog_ref, dtb_ref, x1_ref, z_ref, qkvb_ref, heads_out, gcol_ref, grow_ref,
                    pad_ref, conv_tail, fchunk, tiles_per_seq):
    tm = x_ref.shape[0]
    conv_done = []
    for cb in range(3 * GDN_HEADS):
        even, odd = _gdn_conv_head(pad_ref.at[cb], cw_ref, cb, heads_out[cb // GDN_HEADS], 0, tm - CONV_TAIL)
        conv_done.append(_zero_after(even + odd))
    nchunks = wg_ref.shape[1] // fchunk
    order_after = conv_done[:nchunks - 1] + [functools.reduce(lambda a, b: a + b, conv_done[nchunks - 1:])]

    seq_start = lax.rem(i, tiles_per_seq) == 0
    x = x_ref[...]
    h = _rms(x, n1_ref[...]).astype(BF16)
    x1 = x + 0.5 * _swiglu(h, wg_ref, wu_ref, wd_ref, fchunk, order_after)
    x1_ref[...] = x1
    h2 = _rms(x1, nm_ref[...]).astype(BF16)
    wa_ref, wgt_ref, wb_ref = win_refs
    qkva = _dot(h2, wa_ref[:, :3 * GDN_WIDTH])
    z_ref[...] = _dot(h2, wa_ref[:, 3 * GDN_WIDTH:])
    _gdn_gates(_dot(h2, wgt_ref[...]), alog_ref[...], dtb_ref[...], gcol_ref, grow_ref)

    for cb in range(3 * GDN_HEADS):
        cs = slice(cb * LANES, (cb + 1) * LANES)
        conv_tail(cb, jnp.where(seq_start, 0.0, qkva[0:8, cs]))
        pad_ref[cb, 0:8, :] = jnp.where(seq_start, 0.0, pad_ref[cb, tm:tm + 8, :])
        pad_ref[cb, 8:8 + tm, :] = qkva[:, cs]

    low = lax.broadcasted_iota(jnp.int32, (1, LANES), 1) < SWA_HEAD_DIM

    def head_rms(y, w):
        y2 = y * y
        s0 = jnp.sum(jnp.where(low, y2, 0.0), axis=-1, keepdims=True)
        s1 = jnp.sum(jnp.where(low, 0.0, y2), axis=-1, keepdims=True)
        ms = jnp.where(low, s0, s1) * (1.0 / SWA_HEAD_DIM)
        return y * lax.rsqrt(ms + EPS) * w

    qscale = qw_ref[...] * (SWA_HEAD_DIM ** -0.5 * LOG2E)
    wide = 2 * LANES
    for j in range(3 * SWA_WIDTH // wide):
        y = _dot(h2, wb_ref[:, j * wide:(j + 1) * wide])
        for half in range(2):
            p = 2 * j + half
            yp = y[:, half * LANES:(half + 1) * LANES]
            if p < SWA_WIDTH // LANES:
                yp = head_rms(yp, qscale)
            elif p < 2 * SWA_WIDTH // LANES:
                yp = head_rms(yp, kw_ref[...])
            qkvb_ref[:, p * LANES:(p + 1) * LANES] = yp


def _ffn1_proj(x2d, n1, wg, wu, wd, nm, win, qw, kw, cw, alog, dtb, *, batch, tm, fchunk):
    n, d = x2d.shape
    f = wg.shape[1]
    seq = n // batch
    assert [w.shape[1] for w in win] == [4 * GDN_WIDTH, LANES, 3 * SWA_WIDTH] and seq % tm == 0
    nt = n // tm
    tps = seq // tm
    cur = lambda i: jnp.minimum(i, nt - 1)
    late = lambda i: jnp.maximum(i - 1, 0)
    row = lambda w: pl.BlockSpec((tm, w), lambda i: (cur(i), 0))
    heads = pl.BlockSpec((1, GDN_HEADS, tm, GDN_HEAD_DIM), lambda i: (late(i) // tps, 0, late(i) % tps, 0))
    f32 = lambda *shape: jax.ShapeDtypeStruct(shape, F32)
    return pl.pallas_call(
        functools.partial(_ffn1_proj_kernel, fchunk=fchunk, tiles_per_seq=tps, num_tiles=nt),
        grid=(nt + 1,),
        in_specs=[row(d), _resident((1, d)), _resident((d, f)), _resident((d, f)), _resident((f, d)),
                  _resident((1, d))] + [_resident(w.shape) for w in win] + [_resident((1, LANES)), _resident((1, LANES)),
                  _resident(cw.shape), _resident((1, LANES)), _resident((1, LANES))],
        out_specs=[row(d), row(GDN_WIDTH), row(3 * SWA_WIDTH), heads, heads, heads, row(LANES),
                   pl.BlockSpec((tm // CHUNK, 8, 2 * CHUNK), lambda i: (cur(i), 0, 0))],
        out_shape=[f32(n, d), f32(n, GDN_WIDTH), f32(n, 3 * SWA_WIDTH)]
                  + [f32(batch, GDN_HEADS, seq, GDN_HEAD_DIM)] * 3 + [f32(n, LANES), f32(n // CHUNK, 8, 2 * CHUNK)],
        scratch_shapes=[pltpu.VMEM((3 * GDN_HEADS, tm + 8, LANES), F32),
                        pltpu.VMEM((3 * GDN_HEADS, CONV_TAIL + 16, LANES), F32)],
        compiler_params=_cparams(("arbitrary",)),
        name="ffn1_proj",
    )(x2d, n1, wg, wu, wd, nm, *win, qw, kw, cw, alog, dtb)


def _out_ffn2_kernel(x1_ref, of_ref, ob_ref, z_ref, attn_ref, gnw_ref, wout_ref, n2_ref,
                     wg_ref, wu_ref, wd_ref, nf_ref, out_ref, *, fchunk):
    o = of_ref[...] + ob_ref[...]
    z = z_ref[...]
    gnw = gnw_ref[...]
    heads = []
    for h in range(GDN_HEADS):
        sl = slice(h * GDN_HEAD_DIM, (h + 1) * GDN_HEAD_DIM)
        heads.append(_rms(o[:, sl], gnw) * _silu(z[:, sl]))
    oa = jnp.concatenate(heads, axis=1).astype(BF16)
    mix = _dot(oa, wout_ref[:GDN_WIDTH, :]) + _dot(attn_ref[...].astype(BF16), wout_ref[GDN_WIDTH:, :])
    x2 = x1_ref[...] + mix
    h2 = _rms(x2, n2_ref[...]).astype(BF16)
    x3 = x2 + 0.5 * _swiglu(h2, wg_ref, wu_ref, wd_ref, fchunk)
    out_ref[...] = _rms(x3, nf_ref[...])


def _out_ffn2(x1, o_fwd, o_bwd, z, attn, gnw, wout, n2, wg, wu, wd, nf, *, tm, fchunk):
    n, d = x1.shape
    f = wg.shape[1]
    row = lambda w: pl.BlockSpec((tm, w), lambda i: (i, 0))
    return pl.pallas_call(
        functools.partial(_out_ffn2_kernel, fchunk=fchunk),
        grid=(n // tm,),
        in_specs=[row(d), row(GDN_WIDTH), row(GDN_WIDTH), row(GDN_WIDTH), row(SWA_WIDTH),
                  _resident((1, GDN_HEAD_DIM)), _resident(wout.shape), _resident((1, d)),
                  _resident((d, f)), _resident((d, f)), _resident((f, d)), _resident((1, d))],
        out_specs=row(d),
        out_shape=jax.ShapeDtypeStruct((n, d), F32),
        compiler_params=_cparams(("arbitrary",)),
        name="out_ffn2",
    )(x1, o_fwd, o_bwd, z, attn, gnw, wout, n2, wg, wu, wd, nf)


CONV_TAIL = 16


def _gdn_conv_head(src, cw_ref, cb, dst, out0, count):
    cs = slice(cb * LANES, (cb + 1) * LANES)
    half = count // 2
    done = []
    for par in range(2):
        acc = None
        for t in range(CONV_WIDTH):
            term = src[pl.ds(6 + t + par, half, stride=2), :] * cw_ref[t:t + 1, cs]
            acc = term if acc is None else acc + term
        y = _silu(acc)
        if cb < 2 * GDN_HEADS:
            y = y * lax.rsqrt(jnp.sum(y * y, axis=-1, keepdims=True) + EPS)
        if cb < GDN_HEADS:
            y = y * (GDN_HEAD_DIM ** -0.5)
        dst[0, cb % GDN_HEADS, pl.ds(out0 + par, half, stride=2), :] = y
        done.append(y)
    return done


def _gdn_gates(ab, alog, dtb, gcol_ref, grow_ref):
    ts = ab.shape[0]
    xs = ab + dtb
    softplus = jnp.maximum(xs, 0.0) + jnp.log(1.0 + jnp.exp(-jnp.abs(xs)))
    g = -jnp.exp(alog) * softplus
    beta = 1.0 / (1.0 + jnp.exp(-ab))
    g1 = g.astype(BF16)
    r1 = g - g1.astype(F32)
    g2 = r1.astype(BF16)
    g3 = (r1 - g2.astype(F32)).astype(BF16)
    ri = lax.broadcasted_iota(jnp.int32, (LANES, LANES), 0)
    ci = lax.broadcasted_iota(jnp.int32, (LANES, LANES), 1)
    same = (ri // CHUNK) == (ci // CHUNK)
    lower = jnp.where(same & (ri >= ci), 1.0, 0.0).astype(BF16)
    upper = jnp.where(same & (ri <= ci), 1.0, 0.0).astype(BF16)
    lane = lax.broadcasted_iota(jnp.int32, (1, LANES), 1)
    for s in range(ts // LANES):
        rs = slice(s * LANES, (s + 1) * LANES)
        pre = _dot(lower, g1[rs]) + _dot(lower, g2[rs]) + _dot(lower, g3[rs])
        suf = _dot(upper, g1[rs]) + _dot(upper, g2[rs]) + _dot(upper, g3[rs])
        gs = g[rs]
        col = jnp.where(lane < 4, pre,
              jnp.where(lane < 8, suf,
              jnp.where(lane < 16, beta[rs],
              jnp.where(lane < 20, suf - gs,
              jnp.where(lane < 24, pre - gs, pre + suf - gs)))))
        gcol_ref[rs, :] = col
        rows = col.T[0:8]
        swapped = pltpu.roll(rows, CHUNK, 1)
        grow_ref[2 * s] = jnp.where(lane < CHUNK, rows, swapped)
        grow_ref[2 * s + 1] = jnp.where(lane < CHUNK, swapped, rows)


CHUNKS_PER_STEP = 2

def _gdn_scan_kernel(*refs, nc, nblk):
    ins, (of_ref, ob_ref, s_ref), scr = refs[:10], refs[10:13], refs[13:]
    j = pl.program_id(1)
    slot_build = lax.rem(j, 2)
    slot_scan = 1 - slot_build

    ri = lax.broadcasted_iota(jnp.int32, (CHUNK, 2 * CHUNK), 0)
    li = lax.broadcasted_iota(jnp.int32, (CHUNK, 2 * CHUNK), 1)
    left = li < CHUNK
    eye_lo = jnp.where(li == ri, 1.0, 0.0)
    eye_hi = jnp.where(li == ri + CHUNK, 1.0, 0.0)
    nt = (((1,), (1,)), ((), ()))
    zc = jnp.zeros((CHUNK, GDN_HEAD_DIM), BF16)
    zs = jnp.zeros((GDN_HEAD_DIM, GDN_HEAD_DIM), BF16)
    npairs = GDN_HEADS // 2

    def block_diag(a, b, zero):
        return jnp.concatenate([jnp.concatenate([a, zero], axis=1), jnp.concatenate([zero, b], axis=1)], axis=0)

    nsq = int(math.log2(CHUNK))

    def chunk_matrices(ci, d):
        q_ref, k_ref, v_ref, gcol_ref, grow_ref = ins[5 * d:5 * d + 5]
        wqg_ref, u_ref, ik_ref, eg_ref = scr[4 * d:4 * d + 4]
        dd = (ri - jnp.where(left, li, li - CHUNK)) * (1 - 2 * d)
        incl = dd >= 0
        strict = dd > 0
        probs = [(ci * CHUNKS_PER_STEP + cc, p) for cc in range(CHUNKS_PER_STEP) for p in range(npairs)]
        qgs, kds, egs, rs, xa, xb, intras = [], [], [], [], [], [], []
        for c, p in probs:
            r0 = pl.multiple_of(c * CHUNK, CHUNK)
            gates = gcol_ref[0, pl.ds(r0, CHUNK), :]
            rows = grow_ref[0, c]
            per_head = []
            for h in (2 * p, 2 * p + 1):
                col = GDN_HEADS * d + h
                gc, beta, gdec, gl = (gates[:, base + col:base + col + 1] for base in (0, 8, 16, 24))
                grow = rows[col:col + 1, :]
                qc = q_ref[0, h, pl.ds(r0, CHUNK), :]
                kc = k_ref[0, h, pl.ds(r0, CHUNK), :]
                vc = v_ref[0, h, pl.ds(r0, CHUNK), :]
                kb = kc * beta
                eg = jnp.exp(gc)
                per_head.append(dict(gc=gc, grow=grow, q=qc, k=kc, kb=kb, r=jnp.concatenate([vc * beta, kb * eg], axis=1),
                                     qg=qc * eg, kd=kc * jnp.exp(gdec), eg=jnp.exp(gl[0:1, :])))
            a, b = per_head
            dec = jnp.where(incl, jnp.exp(jnp.where(incl, jnp.where(left, a["gc"], b["gc"])
                                                    - jnp.where(left[0:1], a["grow"], b["grow"]), 0.0)), 0.0)
            lhs = jnp.concatenate([jnp.concatenate([a["kb"], b["kb"]], axis=1),
                                   jnp.concatenate([a["q"], b["q"]], axis=1)], axis=0).astype(BF16)
            qk = lax.dot_general(lhs, block_diag(a["k"].astype(BF16), b["k"].astype(BF16), zc), nt,
                                 preferred_element_type=F32)
            neg_a = -jnp.where(strict, qk[:CHUNK] * dec, 0.0)
            xa.append(jnp.where(left, neg_a, eye_hi))
            xb.append(jnp.where(left, eye_lo, neg_a))
            intras.append(qk[CHUNK:] * dec)
            rs.append((a["r"], b["r"]))
            qgs.append((a["qg"], b["qg"]))
            kds.append(jnp.concatenate([a["kd"], b["kd"]], axis=0))
            egs.append(jnp.concatenate([jnp.broadcast_to(a["eg"], (8, GDN_HEAD_DIM)),
                                        jnp.broadcast_to(b["eg"], (8, GDN_HEAD_DIM))], axis=1))
        yield
        for t in range(nsq):
            for n in range(len(probs)):
                ya = xa[n].astype(BF16)
                yb = xb[n].astype(BF16)
                zz = jnp.zeros((CHUNK, 2 * CHUNK), BF16)
                y = _dot(jnp.where(left, ya, yb), block_diag(ya, yb, zz))
                xa[n] = jnp.where(left, y[:, :2 * CHUNK], xa[n] + y[:, :2 * CHUNK])
                xb[n] = jnp.where(left, xb[n] + y[:, 2 * CHUNK:], y[:, 2 * CHUNK:])
            yield
        for n, (c, p) in enumerate(probs):
            t_off = jnp.where(left, xb[n] - eye_lo, xa[n] - eye_hi).astype(BF16)
            ra, rb = rs[n]
            zr = jnp.zeros((CHUNK, 2 * GDN_HEAD_DIM), BF16)
            out = _dot(t_off, block_diag(rb.astype(BF16), ra.astype(BF16), zr))
            rb = rb + out[:, :2 * GDN_HEAD_DIM]
            ra = ra + out[:, 2 * GDN_HEAD_DIM:]
            ua, wa = ra[:, :GDN_HEAD_DIM], ra[:, GDN_HEAD_DIM:]
            ub, wb = rb[:, :GDN_HEAD_DIM], rb[:, GDN_HEAD_DIM:]
            u_ref[slot_build, c, p] = jnp.concatenate([ua, ub], axis=1)
            wqg_ref[slot_build, c, p] = jnp.concatenate(
                [jnp.concatenate([wa, wb], axis=1), jnp.concatenate(qgs[n], axis=1)], axis=0).astype(BF16)
            ik_ref[slot_build, c, p] = jnp.concatenate([intras[n], kds[n].T], axis=0).astype(BF16)
            eg_ref[slot_build, c, p] = egs[n]
        yield

    def scan_chunks(ci, d):
        wqg_ref, u_ref, ik_ref, eg_ref = scr[4 * d:4 * d + 4]
        o_ref = (of_ref, ob_ref)[d]
        for cc in range(CHUNKS_PER_STEP):
            i = ci * CHUNKS_PER_STEP + cc
            c = nc - 1 - i if d else i
            r0 = pl.multiple_of(c * CHUNK, CHUNK)
            states = [s_ref[d, p] for p in range(npairs)]
            m1 = [_dot(wqg_ref[slot_scan, c, p],
                       block_diag(states[p][:, :GDN_HEAD_DIM].astype(BF16), states[p][:, GDN_HEAD_DIM:].astype(BF16), zs))
                  for p in range(npairs)]
            yield
            v_new = [(u_ref[slot_scan, c, p] - m1[p][:CHUNK]).astype(BF16) for p in range(npairs)]
            m2 = [_dot(ik_ref[slot_scan, c, p], block_diag(v_new[p][:, :GDN_HEAD_DIM], v_new[p][:, GDN_HEAD_DIM:], zc))
                  for p in range(npairs)]
            for p in range(npairs):
                ps = slice(2 * p * GDN_HEAD_DIM, 2 * (p + 1) * GDN_HEAD_DIM)
                o_ref[0, pl.ds(r0, CHUNK), ps] = m1[p][CHUNK:] + m2[p][:CHUNK]
                s_ref[d, p] = states[p] * eg_ref[slot_scan, c, p][0:1, :] + m2[p][CHUNK:]
            yield

    def run(*makers):
        def body(ci, carry):
            live = [m(ci, d) for m in makers for d in range(2)]
            while live:
                live = [g for g in live if next(g, StopIteration) is not StopIteration]
            return carry
        lax.fori_loop(0, nc // CHUNKS_PER_STEP, body, 0)

    @pl.when(j == 0)
    def _():
        s_ref[...] = jnp.zeros(s_ref.shape, F32)
        run(chunk_matrices)

    @pl.when((j > 0) & (j < nblk))
    def _():
        run(chunk_matrices, scan_chunks)

    @pl.when(j == nblk)
    def _():
        run(scan_chunks)


def _gdn_scan(q, k, v, gcol, grow, *, ts):
    b, nh, s, hd = q.shape
    w = nh * hd
    nblk = s // ts
    nc = ts // CHUNK
    order = (lambda t: t, lambda t: nblk - 1 - t)
    in_specs, out_specs, scratch = [], [], [pltpu.VMEM((2, nh // 2, hd, 2 * hd), F32)]
    for d in range(2):
        src = lambda j, d=d: order[d](jnp.minimum(j, nblk - 1))
        dst = lambda j, d=d: order[d](jnp.maximum(j - 1, 0))
        in_specs += [pl.BlockSpec((1, nh, ts, hd), lambda i, j, src=src: (i, 0, src(j), 0))] * 3
        in_specs += [pl.BlockSpec((1, ts, LANES), lambda i, j, src=src: (i, src(j), 0)),
                     pl.BlockSpec((1, nc, 8, 2 * CHUNK), lambda i, j, src=src: (i, src(j), 0, 0))]
        out_specs.append(pl.BlockSpec((1, ts, w), lambda i, j, dst=dst: (i, dst(j), 0)))
        scratch += [pltpu.VMEM((2, nc, nh // 2, 2 * CHUNK, 2 * hd), BF16),
                    pltpu.VMEM((2, nc, nh // 2, CHUNK, 2 * hd), F32),
                    pltpu.VMEM((2, nc, nh // 2, CHUNK + hd, 2 * CHUNK), BF16),
                    pltpu.VMEM((2, nc, nh // 2, 8, 2 * hd), F32)]
    return pl.pallas_call(
        functools.partial(_gdn_scan_kernel, nc=nc, nblk=nblk),
        grid=(b, nblk + 1),
        in_specs=in_specs,
        out_specs=out_specs,
        out_shape=[jax.ShapeDtypeStruct((b, s, w), F32)] * 2,
        scratch_shapes=scratch,
        compiler_params=_cparams(("arbitrary", "arbitrary")),
        name="gdn_scan",
    )(*([q, k, v, gcol, grow] * 2))


SWA_TQ = 128
SWA_TK = SWA_TQ + 2 * BAND_RADIUS
SWA_HALO = BAND_RADIUS * max(DILATIONS)
COPY_ROWS = 512
SWA_TILES_PER_STEP = 2


def _swa_kernel(q_ref, k_ref, v_ref, bias_ref, o_ref,
                kn_ref, vp_ref, m0_ref, m1_ref, l_ref, acc_ref, bias_scr, *, seq, qb):
    qi = pl.program_id(2)
    qn_ref = q_ref.at[0]
    lane = lax.broadcasted_iota(jnp.int32, (1, LANES), 1)
    low = lane < SWA_HEAD_DIM
    nt = (((1,), (1,)), ((), ()))

    @pl.when(qi == 0)
    def _():
        for pi in range(len(DILATIONS)):
            for hh in range(2):
                full = jnp.broadcast_to(bias_ref[0, pi, hh], (SWA_TQ, SWA_TK))
                bias_scr[pi, hh] = pltpu.roll(full, 0, 1, stride=1, stride_axis=0)

        zeros = jnp.zeros((SWA_HALO, LANES), F32)
        for ref in (kn_ref, vp_ref):
            ref[0:SWA_HALO, :] = zeros
            ref[SWA_HALO + seq:2 * SWA_HALO + seq, :] = zeros

        def copy_rows(i, carry):
            r0 = pl.multiple_of(i * COPY_ROWS, COPY_ROWS)
            kn_ref[pl.ds(SWA_HALO + r0, COPY_ROWS), :] = k_ref[0, pl.ds(r0, COPY_ROWS), :]
            vp_ref[pl.ds(SWA_HALO + r0, COPY_ROWS), :] = v_ref[0, pl.ds(r0, COPY_ROWS), :]
            return carry

        lax.fori_loop(0, seq // COPY_ROWS, copy_rows, 0)

    kcol = lax.broadcasted_iota(jnp.int32, (1, SWA_TK), 1)

    def rows(ref, start, size, stride):
        if stride == 1:
            return ref[pl.ds(start, size), :]
        return ref[pl.ds(start, size, stride=stride), :]

    def put(ref, start, size, stride, val):
        if stride == 1:
            ref[pl.ds(start, size), :] = val
        else:
            ref[pl.ds(start, size, stride=stride), :] = val

    def logits(pi, dil, g):
        sub_len = seq // dil
        tiles_per_res = qb // dil // SWA_TQ
        ss = []
        for u in range(SWA_TILES_PER_STEP):
            res, t = divmod(g * SWA_TILES_PER_STEP + u, tiles_per_res)
            tau0 = qi * (qb // dil) + t * SWA_TQ
            loc0 = res + dil * (t * SWA_TQ)
            krow = SWA_HALO + qi * qb + loc0 - dil * BAND_RADIUS
            qt = rows(qn_ref, qi * qb + loc0, SWA_TQ, dil)
            kt = rows(kn_ref, krow, SWA_TK, dil).astype(BF16)
            at_end = t in (0, tiles_per_res - 1)
            kidx = kcol + (tau0 - BAND_RADIUS)
            kvalid = (kidx >= 0) & (kidx < sub_len)
            for hh in range(2):
                mine = low if hh == 0 else jnp.logical_not(low)
                qh = jnp.where(mine, qt, 0.0).astype(BF16)
                s = lax.dot_general(qh, kt, nt, preferred_element_type=F32) + bias_scr[pi, hh]
                ss.append(jnp.where(kvalid, s, NEG_BIG) if at_end else s)
        return ss

    def accumulate(dil, g, ss, first):
        tiles_per_res = qb // dil // SWA_TQ
        loc, vts, old = [], [], []
        for u in range(SWA_TILES_PER_STEP):
            res, t = divmod(g * SWA_TILES_PER_STEP + u, tiles_per_res)
            loc0 = res + dil * (t * SWA_TQ)
            vt = rows(vp_ref, SWA_HALO + qi * qb + loc0 - dil * BAND_RADIUS, SWA_TK, dil)
            vts.append((jnp.where(low, vt, 1.0).astype(BF16), jnp.where(low, 1.0, vt).astype(BF16)))
            loc.append(loc0)
            if not first:
                old.append(tuple(rows(ref, loc0, SWA_TQ, dil) for ref in (m0_ref, m1_ref, l_ref, acc_ref)))
        m_new, alpha, ps = [], [], []
        for n, s in enumerate(ss):
            u, hh = divmod(n, 2)
            mt = jnp.max(s, axis=-1, keepdims=True)
            if first:
                mn = jnp.broadcast_to(mt, (SWA_TQ, LANES))
            else:
                mo = old[u][hh]
                mn = jnp.maximum(mo, mt)
                alpha.append(jnp.exp2(mo - mn))
            m_new.append(mn)
            ps.append(jnp.exp2(s - jnp.concatenate([mn, mn], axis=1)).astype(BF16))
        pv = [_dot(p, vts[n // 2][n % 2]) for n, p in enumerate(ps)]
        for u in range(SWA_TILES_PER_STEP):
            pv0, pv1 = pv[2 * u], pv[2 * u + 1]
            put(m0_ref, loc[u], SWA_TQ, dil, m_new[2 * u])
            put(m1_ref, loc[u], SWA_TQ, dil, m_new[2 * u + 1])
            lsum = jnp.where(low, pv1, pv0)
            pvv = jnp.where(low, pv0, pv1)
            if first:
                put(l_ref, loc[u], SWA_TQ, dil, lsum)
                put(acc_ref, loc[u], SWA_TQ, dil, pvv)
            else:
                a0, a1 = alpha[2 * u], alpha[2 * u + 1]
                put(l_ref, loc[u], SWA_TQ, dil, old[u][2] * jnp.where(low, a1, a0) + lsum)
                put(acc_ref, loc[u], SWA_TQ, dil, old[u][3] * jnp.where(low, a0, a1) + pvv)

    order = sorted(range(len(DILATIONS)), key=lambda pi: -DILATIONS[pi])
    units = [(pi, DILATIONS[pi], g) for pi in order for g in range(qb // SWA_TQ // SWA_TILES_PER_STEP)]
    pending = logits(*units[0])
    for k, (pi, dil, g) in enumerate(units):
        nxt = logits(*units[k + 1]) if k + 1 < len(units) else None
        accumulate(dil, g, pending, first=pi == order[0])
        pending = nxt

    o_ref[0] = acc_ref[...] / pltpu.roll(l_ref[...], SWA_HEAD_DIM, 1)


def _t5_bucket(rel):
    nb = REL_BUCKETS // 2
    bucket = (rel > 0).astype(np.int32) * nb
    n = np.abs(rel)
    max_exact = nb // 2
    large = max_exact + (np.log(np.maximum(n, 1) / max_exact)
                         / math.log(REL_MAX_DISTANCE / max_exact) * (nb - max_exact)).astype(np.int32)
    large = np.minimum(large, nb - 1)
    return (bucket + np.where(n < max_exact, n, large)).astype(np.int32)


def _band_bias(rel_bias):
    rel = np.arange(2 * BAND_RADIUS + 1) - BAND_RADIUS
    rows = []
    for dil in DILATIONS:
        inside = jnp.transpose(rel_bias[_t5_bucket(rel * dil)]) * LOG2E
        rows.append(jnp.pad(inside, ((0, 0), (0, SWA_TK - inside.shape[1])), constant_values=NEG_BIG))
    bias = jnp.stack(rows, axis=0)
    bias = bias.reshape(len(DILATIONS), SWA_HEADS // 2, 2, 1, SWA_TK)
    return jnp.transpose(bias, (1, 0, 2, 3, 4)).astype(F32)


def _swa(qkvb, bias, *, qb):
    b, s, _ = qkvb.shape
    pairs = SWA_HEADS // 2
    assert s % qb == 0 and qb % (max(DILATIONS) * SWA_TQ) == 0
    col = lambda base: pl.BlockSpec((1, s, LANES), lambda i, p, j: (i, 0, base + p))
    padded = s + 2 * SWA_HALO
    return pl.pallas_call(
        functools.partial(_swa_kernel, seq=s, qb=qb),
        grid=(b, pairs, s // qb),
        in_specs=[col(0), col(pairs), col(2 * pairs),
                  pl.BlockSpec((1,) + bias.shape[1:], lambda i, p, j: (p, 0, 0, 0, 0))],
        out_specs=pl.BlockSpec((1, qb, LANES), lambda i, p, j: (i, j, p)),
        out_shape=jax.ShapeDtypeStruct((b, s, SWA_WIDTH), F32),
        scratch_shapes=[pltpu.VMEM((padded, LANES), F32), pltpu.VMEM((padded, LANES), F32)]
                       + [pltpu.VMEM((qb, LANES), F32)] * 4
                       + [pltpu.VMEM((len(DILATIONS), 2, SWA_TQ, SWA_TK), F32)],
        compiler_params=_cparams(("arbitrary", "arbitrary", "arbitrary")),
        name="swa",
    )(qkvb, qkvb, qkvb, bias)


FFN_TM = 512
FFN_FCHUNK = 256
SCAN_TS = 512
SWA_QB = 2048


def _lane_row(vals, copies_at):
    row = jnp.zeros((LANES,), F32)
    for off in copies_at:
        row = row.at[off:off + vals.shape[0]].set(vals)
    return row[None, :]


def kernel(x, ffn1_norm, ffn1_w_gate, ffn1_w_up, ffn1_w_down, mix_norm, w_in, conv_w, a_log, dt_bias, gdn_norm_w, q_norm_w, k_norm_w, rel_bias, w_out, ffn2_norm, ffn2_w_gate, ffn2_w_up, ffn2_w_down, final_norm):
    b, s, d = x.shape
    n = b * s
    x2d = x.reshape(n, d)
    ng = 2 * GDN_HEADS
    c_z = 3 * GDN_WIDTH
    c_a = c_z + GDN_WIDTH
    c_b = c_a + ng
    c_qkvb = c_b + ng
    bias = _band_bias(rel_bias)
    for l in range(ffn1_norm.shape[0]):
        wl = w_in[l]
        w_a = wl[:, c_a:c_b]
        w_gates = jnp.concatenate([w_a, wl[:, c_b:c_qkvb], w_a, w_a, jnp.zeros((d, LANES - 4 * ng), F32)], axis=1)
        win = (wl[:, :c_a].astype(BF16), w_gates.astype(BF16), wl[:, c_qkvb:].astype(BF16))
        copies = (0, 2 * ng, 3 * ng)
        alog = _lane_row(a_log[l].reshape(ng), copies)
        dtb = _lane_row(dt_bias[l].reshape(ng), copies)
        x1, z, qkvb, q, k, v, gcol, grow = _ffn1_proj(
            x2d, ffn1_norm[l][None, :], ffn1_w_gate[l].astype(BF16), ffn1_w_up[l].astype(BF16),
            ffn1_w_down[l].astype(BF16), mix_norm[l][None, :], win,
            jnp.tile(q_norm_w[l], 2)[None, :], jnp.tile(k_norm_w[l], 2)[None, :],
            jnp.transpose(conv_w[l]), alog, dtb, batch=b, tm=FFN_TM, fchunk=FFN_FCHUNK)
        o_fwd, o_bwd = _gdn_scan(q, k, v, gcol.reshape(b, s, LANES), grow.reshape(b, s // CHUNK, 8, 2 * CHUNK),
                                 ts=SCAN_TS)
        attn = _swa(qkvb.reshape(b, s, -1), bias, qb=SWA_QB)
        x2d = _out_ffn2(
            x1, o_fwd.reshape(n, -1), o_bwd.reshape(n, -1), z, attn.reshape(n, -1),
            gdn_norm_w[l][None, :], w_out[l].astype(BF16), ffn2_norm[l][None, :],
            ffn2_w_gate[l].astype(BF16), ffn2_w_up[l].astype(BF16), ffn2_w_down[l].astype(BF16),
            final_norm[l][None, :], tm=FFN_TM, fchunk=FFN_FCHUNK)
    return x2d.reshape(b, s, d)
```

```python
import functools
import math

import numpy as np
import jax
import jax.numpy as jnp
from jax import lax
from jax.experimental import pallas as pl
from jax.experimental.pallas import tpu as pltpu

F32 = jnp.float32
BF16 = jnp.bfloat16
EPS = 1e-6
NEG_BIG = -1e30
LOG2E = math.log2(math.e)

LANES = 128
GDN_HEADS = 4
GDN_HEAD_DIM = 128
GDN_WIDTH = GDN_HEADS * GDN_HEAD_DIM
CONV_WIDTH = 5
CHUNK = 64
SWA_HEADS = 8
SWA_HEAD_DIM = 64
SWA_WIDTH = SWA_HEADS * SWA_HEAD_DIM
DILATIONS = (1, 4, 16)
BAND_RADIUS = 64
REL_BUCKETS = 32
REL_MAX_DISTANCE = 1024
VMEM_LIMIT = 56 * 1024 * 1024


def _cparams(sem):
    return pltpu.CompilerParams(dimension_semantics=sem, vmem_limit_bytes=VMEM_LIMIT)


def _resident(shape):
    zeros = (0,) * len(shape)
    return pl.BlockSpec(shape, lambda *_: zeros, pipeline_mode=pl.Buffered(1))


def _rms(x, w):
    return x * lax.rsqrt(jnp.mean(x * x, axis=-1, keepdims=True) + EPS) * w


def _silu(x):
    return x * (1.0 / (1.0 + jnp.exp(-x)))


def _dot(a, b):
    return jnp.dot(a, b, preferred_element_type=F32)


def _zero_after(values):
    s = values[0:8]
    for r in range(8, values.shape[0], 8):
        s = s + values[r:r + 8]
    bits = lax.bitcast_convert_type(s, jnp.int32)
    return lax.shift_right_logical(lax.shift_right_logical(bits, 16), 16).astype(F32)


def _swiglu(h, wg_ref, wu_ref, wd_ref, fchunk, order_after=()):
    acc = None
    for n, c0 in enumerate(range(0, wg_ref.shape[1], fchunk)):
        g = _dot(h, wg_ref[:, c0:c0 + fchunk])
        if n < len(order_after):
            first = jnp.concatenate([g[0:8, 0:LANES] + order_after[n], g[0:8, LANES:]], axis=1)
            g = jnp.concatenate([first, g[8:]], axis=0)
        u = _dot(h, wu_ref[:, c0:c0 + fchunk])
        a = (_silu(g) * u).astype(BF16)
        d = _dot(a, wd_ref[c0:c0 + fchunk, :])
        acc = d if acc is None else acc + d
    return acc


def _ffn1_proj_kernel(x_ref, n1_ref, wg_ref, wu_ref, wd_ref, nm_ref, wa_ref, wgt_ref, wb_ref, qw_ref, kw_ref,
                      cw_ref, alog_ref, dtb_ref,
                      x1_ref, z_ref, qkvb_ref, q_ref, k_ref, v_ref, gcol_ref, grow_ref, pad_ref, tail_ref,
                      *, fchunk, tiles_per_seq, num_tiles):
    i = pl.program_id(0)
    tm = x_ref.shape[0]
    heads_out = (q_ref, k_ref, v_ref)

    @pl.when(i == 0)
    def _():
        pad_ref[...] = jnp.zeros(pad_ref.shape, F32)

    def conv_tail(cb, next_rows):
        tail_ref[cb, 0:8 + CONV_TAIL, :] = pad_ref[cb, tm - CONV_TAIL:8 + tm, :]
        tail_ref[cb, 8 + CONV_TAIL:16 + CONV_TAIL, :] = next_rows
        _gdn_conv_head(tail_ref.at[cb], cw_ref, cb, heads_out[cb // GDN_HEADS], tm - CONV_TAIL, CONV_TAIL)

    @pl.when(i == num_tiles)
    def _():
        for cb in range(3 * GDN_HEADS):
            _gdn_conv_head(pad_ref.at[cb], cw_ref, cb, heads_out[cb // GDN_HEADS], 0, tm - CONV_TAIL)
            conv_tail(cb, jnp.zeros((8, LANES), F32))

    @pl.when(i < num_tiles)
    def _():
        _ffn1_proj_tile(i, x_ref, n1_ref, wg_ref, wu_ref, wd_ref, nm_ref, (wa_ref, wgt_ref, wb_ref), qw_ref, kw_ref,
                        cw_ref, alog_ref, dtb_ref, x1_ref, z_ref, qkvb_ref, heads_out, gcol_ref, grow_ref,
                        pad_ref, conv_tail, fchunk, tiles_per_seq)


def _ffn1_proj_tile(i, x_ref, n1_ref, wg_ref, wu_ref, wd_ref, nm_ref, win_refs, qw_ref, kw_ref,
                    cw_ref, alog_ref, dtb_ref, x1_ref, z_ref, qkvb_ref, heads_out, gcol_ref, grow_ref,
                    pad_ref, conv_tail, fchunk, tiles_per_seq):
    tm = x_ref.shape[0]
    conv_done = []
    for cb in range(3 * GDN_HEADS):
        even, odd = _gdn_conv_head(pad_ref.at[cb], cw_ref, cb, heads_out[cb // GDN_HEADS], 0, tm - CONV_TAIL)
        conv_done.append(_zero_after(even + odd))
    nchunks = wg_ref.shape[1] // fchunk
    order_after = conv_done[:nchunks - 1] + [functools.reduce(lambda a, b: a + b, conv_done[nchunks - 1:])]

    seq_start = lax.rem(i, tiles_per_seq) == 0
    x = x_ref[...]
    h = _rms(x, n1_ref[...]).astype(BF16)
    x1 = x + 0.5 * _swiglu(h, wg_ref, wu_ref, wd_ref, fchunk, order_after)
    x1_ref[...] = x1
    h2 = _rms(x1, nm_ref[...]).astype(BF16)
    wa_ref, wgt_ref, wb_ref = win_refs
    qkva = _dot(h2, wa_ref[:, :3 * GDN_WIDTH])
    z_ref[...] = _dot(h2, wa_ref[:, 3 * GDN_WIDTH:])
    _gdn_gates(_dot(h2, wgt_ref[...]), alog_ref[...], dtb_ref[...], gcol_ref, grow_ref)

    for cb in range(3 * GDN_HEADS):
        cs = slice(cb * LANES, (cb + 1) * LANES)
        conv_tail(cb, jnp.where(seq_start, 0.0, qkva[0:8, cs]))
        pad_ref[cb, 0:8, :] = jnp.where(seq_start, 0.0, pad_ref[cb, tm:tm + 8, :])
        pad_ref[cb, 8:8 + tm, :] = qkva[:, cs]

    low = lax.broadcasted_iota(jnp.int32, (1, LANES), 1) < SWA_HEAD_DIM

    def head_rms(y, w):
        y2 = y * y
        s0 = jnp.sum(jnp.where(low, y2, 0.0), axis=-1, keepdims=True)
        s1 = jnp.sum(jnp.where(low, 0.0, y2), axis=-1, keepdims=True)
        ms = jnp.where(low, s0, s1) * (1.0 / SWA_HEAD_DIM)
        return y * lax.rsqrt(ms + EPS) * w

    qscale = qw_ref[...] * (SWA_HEAD_DIM ** -0.5 * LOG2E)
    wide = 2 * LANES
    for j in range(3 * SWA_WIDTH // wide):
        y = _dot(h2, wb_ref[:, j * wide:(j + 1) * wide])
        for half in range(2):
            p = 2 * j + half
            yp = y[:, half * LANES:(half + 1) * LANES]
            if p < SWA_WIDTH // LANES:
                yp = head_rms(yp, qscale)
            elif p < 2 * SWA_WIDTH // LANES:
                yp = head_rms(yp, kw_ref[...])
            qkvb_ref[:, p * LANES:(p + 1) * LANES] = yp


def _ffn1_proj(x2d, n1, wg, wu, wd, nm, win, qw, kw, cw, alog, dtb, *, batch, tm, fchunk):
    n, d = x2d.shape
    f = wg.shape[1]
    seq = n // batch
    assert [w.shape[1] for w in win] == [4 * GDN_WIDTH, LANES, 3 * SWA_WIDTH] and seq % tm == 0
    nt = n // tm
    tps = seq // tm
    cur = lambda i: jnp.minimum(i, nt - 1)
    late = lambda i: jnp.maximum(i - 1, 0)
    row = lambda w: pl.BlockSpec((tm, w), lambda i: (cur(i), 0))
    heads = pl.BlockSpec((1, GDN_HEADS, tm, GDN_HEAD_DIM), lambda i: (late(i) // tps, 0, late(i) % tps, 0))
    f32 = lambda *shape: jax.ShapeDtypeStruct(shape, F32)
    return pl.pallas_call(
        functools.partial(_ffn1_proj_kernel, fchunk=fchunk, tiles_per_seq=tps, num_tiles=nt),
        grid=(nt + 1,),
        in_specs=[row(d), _resident((1, d)), _resident((d, f)), _resident((d, f)), _resident((f, d)),
                  _resident((1, d))] + [_resident(w.shape) for w in win] + [_resident((1, LANES)), _resident((1, LANES)),
                  _resident(cw.shape), _resident((1, LANES)), _resident((1, LANES))],
        out_specs=[row(d), row(GDN_WIDTH), row(3 * SWA_WIDTH), heads, heads, heads, row(LANES),
                   pl.BlockSpec((tm // CHUNK, 8, 2 * CHUNK), lambda i: (cur(i), 0, 0))],
        out_shape=[f32(n, d), f32(n, GDN_WIDTH), f32(n, 3 * SWA_WIDTH)]
                  + [f32(batch, GDN_HEADS, seq, GDN_HEAD_DIM)] * 3 + [f32(n, LANES), f32(n // CHUNK, 8, 2 * CHUNK)],
        scratch_shapes=[pltpu.VMEM((3 * GDN_HEADS, tm + 8, LANES), F32),
                        pltpu.VMEM((3 * GDN_HEADS, CONV_TAIL + 16, LANES), F32)],
        compiler_params=_cparams(("arbitrary",)),
        name="ffn1_proj",
    )(x2d, n1, wg, wu, wd, nm, *win, qw, kw, cw, alog, dtb)


def _out_ffn2_kernel(x1_ref, of_ref, ob_ref, z_ref, attn_ref, gnw_ref, wout_ref, n2_ref,
                     wg_ref, wu_ref, wd_ref, nf_ref, out_ref, *, fchunk):
    o = of_ref[...] + ob_ref[...]
    z = z_ref[...]
    gnw = gnw_ref[...]
    heads = []
    for h in range(GDN_HEADS):
        sl = slice(h * GDN_HEAD_DIM, (h + 1) * GDN_HEAD_DIM)
        heads.append(_rms(o[:, sl], gnw) * _silu(z[:, sl]))
    oa = jnp.concatenate(heads, axis=1).astype(BF16)
    mix = _dot(oa, wout_ref[:GDN_WIDTH, :]) + _dot(attn_ref[...].astype(BF16), wout_ref[GDN_WIDTH:, :])
    x2 = x1_ref[...] + mix
    h2 = _rms(x2, n2_ref[...]).astype(BF16)
    x3 = x2 + 0.5 * _swiglu(h2, wg_ref, wu_ref, wd_ref, fchunk)
    out_ref[...] = _rms(x3, nf_ref[...])


def _out_ffn2(x1, o_fwd, o_bwd, z, attn, gnw, wout, n2, wg, wu, wd, nf, *, tm, fchunk):
    n, d = x1.shape
    f = wg.shape[1]
    row = lambda w: pl.BlockSpec((tm, w), lambda i: (i, 0))
    return pl.pallas_call(
        functools.partial(_out_ffn2_kernel, fchunk=fchunk),
        grid=(n // tm,),
        in_specs=[row(d), row(GDN_WIDTH), row(GDN_WIDTH), row(GDN_WIDTH), row(SWA_WIDTH),
                  _resident((1, GDN_HEAD_DIM)), _resident(wout.shape), _resident((1, d)),
                  _resident((d, f)), _resident((d, f)), _resident((f, d)), _resident((1, d))],
        out_specs=row(d),
        out_shape=jax.ShapeDtypeStruct((n, d), F32),
        compiler_params=_cparams(("arbitrary",)),
        name="out_ffn2",
    )(x1, o_fwd, o_bwd, z, attn, gnw, wout, n2, wg, wu, wd, nf)


CONV_TAIL = 16


def _gdn_conv_head(src, cw_ref, cb, dst, out0, count):
    cs = slice(cb * LANES, (cb + 1) * LANES)
    half = count // 2
    done = []
    for par in range(2):
        acc = None
        for t in range(CONV_WIDTH):
            term = src[pl.ds(6 + t + par, half, stride=2), :] * cw_ref[t:t + 1, cs]
            acc = term if acc is None else acc + term
        y = _silu(acc)
        if cb < 2 * GDN_HEADS:
            y = y * lax.rsqrt(jnp.sum(y * y, axis=-1, keepdims=True) + EPS)
        if cb < GDN_HEADS:
            y = y * (GDN_HEAD_DIM ** -0.5)
        dst[0, cb % GDN_HEADS, pl.ds(out0 + par, half, stride=2), :] = y
        done.append(y)
    return done


def _gdn_gates(ab, alog, dtb, gcol_ref, grow_ref):
    ts = ab.shape[0]
    xs = ab + dtb
    softplus = jnp.maximum(xs, 0.0) + jnp.log(1.0 + jnp.exp(-jnp.abs(xs)))
    g = -jnp.exp(alog) * softplus
    beta = 1.0 / (1.0 + jnp.exp(-ab))
    g1 = g.astype(BF16)
    r1 = g - g1.astype(F32)
    g2 = r1.astype(BF16)
    g3 = (r1 - g2.astype(F32)).astype(BF16)
    ri = lax.broadcasted_iota(jnp.int32, (LANES, LANES), 0)
    ci = lax.broadcasted_iota(jnp.int32, (LANES, LANES), 1)
    same = (ri // CHUNK) == (ci // CHUNK)
    lower = jnp.where(same & (ri >= ci), 1.0, 0.0).astype(BF16)
    upper = jnp.where(same & (ri <= ci), 1.0, 0.0).astype(BF16)
    lane = lax.broadcasted_iota(jnp.int32, (1, LANES), 1)
    for s in range(ts // LANES):
        rs = slice(s * LANES, (s + 1) * LANES)
        pre = _dot(lower, g1[rs]) + _dot(lower, g2[rs]) + _dot(lower, g3[rs])
        suf = _dot(upper, g1[rs]) + _dot(upper, g2[rs]) + _dot(upper, g3[rs])
        gs = g[rs]
        col = jnp.where(lane < 4, pre,
              jnp.where(lane < 8, suf,
              jnp.where(lane < 16, beta[rs],
              jnp.where(lane < 20, suf - gs,
              jnp.where(lane < 24, pre - gs, pre + suf - gs)))))
        gcol_ref[rs, :] = col
        rows = col.T[0:8]
        swapped = pltpu.roll(rows, CHUNK, 1)
        grow_ref[2 * s] = jnp.where(lane < CHUNK, rows, swapped)
        grow_ref[2 * s + 1] = jnp.where(lane < CHUNK, swapped, rows)


CHUNKS_PER_STEP = 2

def _gdn_scan_kernel(*refs, nc, nblk):
    ins, (of_ref, ob_ref, s_ref), scr = refs[:10], refs[10:13], refs[13:]
    j = pl.program_id(1)
    slot_build = lax.rem(j, 2)
    slot_scan = 1 - slot_build

    ri = lax.broadcasted_iota(jnp.int32, (CHUNK, 2 * CHUNK), 0)
    li = lax.broadcasted_iota(jnp.int32, (CHUNK, 2 * CHUNK), 1)
    left = li < CHUNK
    eye_lo = jnp.where(li == ri, 1.0, 0.0)
    eye_hi = jnp.where(li == ri + CHUNK, 1.0, 0.0)
    nt = (((1,), (1,)), ((), ()))
    zc = jnp.zeros((CHUNK, GDN_HEAD_DIM), BF16)
    zs = jnp.zeros((GDN_HEAD_DIM, GDN_HEAD_DIM), BF16)
    npairs = GDN_HEADS // 2

    def block_diag(a, b, zero):
        return jnp.concatenate([jnp.concatenate([a, zero], axis=1), jnp.concatenate([zero, b], axis=1)], axis=0)

    nsq = int(math.log2(CHUNK))

    groups = nc // CHUNKS_PER_STEP
    nprob = CHUNKS_PER_STEP * npairs

    def scan_chunk(i, d):
        return nc - 1 - i if d else i

    def neumann_step(xa, xb):
        ya = xa.astype(BF16)
        yb = xb.astype(BF16)
        zz = jnp.zeros((CHUNK, 2 * CHUNK), BF16)
        y = _dot(jnp.where(left, ya, yb), block_diag(ya, yb, zz))
        return (jnp.where(left, y[:, :2 * CHUNK], xa + y[:, :2 * CHUNK]),
                jnp.where(left, xb + y[:, 2 * CHUNK:], y[:, 2 * CHUNK:]))

    def build_first(g, d):
        parity = g % 2 if isinstance(g, int) else lax.rem(g, 2)
        q_ref, k_ref, v_ref, gcol_ref, grow_ref = ins[5 * d:5 * d + 5]
        wqg_ref, u_ref, ik_ref, eg_ref, hx_ref, hr_ref = scr[6 * d:6 * d + 6]
        dd = (ri - jnp.where(left, li, li - CHUNK)) * (1 - 2 * d)
        incl = dd >= 0
        strict = dd > 0
        probs = [(scan_chunk(g * CHUNKS_PER_STEP + cc, d), p) for cc in range(CHUNKS_PER_STEP) for p in range(npairs)]
        xa, xb, done = [], [], []
        for n, (c, p) in enumerate(probs):
            r0 = pl.multiple_of(c * CHUNK, CHUNK)
            gates = gcol_ref[0, pl.ds(r0, CHUNK), :]
            rows = grow_ref[0, c]
            per_head = []
            for h in (2 * p, 2 * p + 1):
                col = GDN_HEADS * d + h
                gc, beta, gdec, gl = (gates[:, base + col:base + col + 1] for base in (0, 8, 16, 24))
                qc = q_ref[0, h, pl.ds(r0, CHUNK), :]
                kc = k_ref[0, h, pl.ds(r0, CHUNK), :]
                vc = v_ref[0, h, pl.ds(r0, CHUNK), :]
                kb = kc * beta
                eg = jnp.exp(gc)
                per_head.append(dict(gc=gc, grow=rows[col:col + 1, :], q=qc, k=kc, kb=kb,
                                     r=jnp.concatenate([vc * beta, kb * eg], axis=1),
                                     qg=qc * eg, kd=kc * jnp.exp(gdec), eg=jnp.exp(gl[0:1, :])))
            a, b = per_head
            dec = jnp.where(incl, jnp.exp(jnp.where(incl, jnp.where(left, a["gc"], b["gc"])
                                                    - jnp.where(left[0:1], a["grow"], b["grow"]), 0.0)), 0.0)
            lhs = jnp.concatenate([jnp.concatenate([a["kb"], b["kb"]], axis=1),
                                   jnp.concatenate([a["q"], b["q"]], axis=1)], axis=0).astype(BF16)
            qk = lax.dot_general(lhs, block_diag(a["k"].astype(BF16), b["k"].astype(BF16), zc), nt,
                                 preferred_element_type=F32)
            neg_a = -jnp.where(strict, qk[:CHUNK] * dec, 0.0)
            xa.append(jnp.where(left, neg_a, eye_hi))
            xb.append(jnp.where(left, eye_lo, neg_a))
            done.append((jnp.concatenate([a["qg"], b["qg"]], axis=1).astype(BF16),
                         jnp.concatenate([qk[CHUNK:] * dec, jnp.concatenate([a["kd"], b["kd"]], axis=0).T],
                                         axis=0).astype(BF16),
                         jnp.concatenate([jnp.broadcast_to(a["eg"], (8, GDN_HEAD_DIM)),
                                          jnp.broadcast_to(b["eg"], (8, GDN_HEAD_DIM))], axis=1)))
            hr_ref[parity, n, 0] = a["r"]
            hr_ref[parity, n, 1] = b["r"]
        yield
        for t in range(nsq // 2):
            for n in range(nprob):
                xa[n], xb[n] = neumann_step(xa[n], xb[n])
            yield
        for n, (c, p) in enumerate(probs):
            hx_ref[n, 0] = xa[n]
            hx_ref[n, 1] = xb[n]
            wqg_ref[slot_build, c, p, CHUNK:, :], ik_ref[slot_build, c, p], eg_ref[slot_build, c, p] = done[n]

    def build_second(g, slot, d):
        parity = g % 2 if isinstance(g, int) else lax.rem(g, 2)
        wqg_ref, u_ref, ik_ref, eg_ref, hx_ref, hr_ref = scr[6 * d:6 * d + 6]
        probs = [(scan_chunk(g * CHUNKS_PER_STEP + cc, d), p) for cc in range(CHUNKS_PER_STEP) for p in range(npairs)]
        xa = [hx_ref[n, 0] for n in range(nprob)]
        xb = [hx_ref[n, 1] for n in range(nprob)]
        for t in range(nsq // 2, nsq):
            for n in range(nprob):
                xa[n], xb[n] = neumann_step(xa[n], xb[n])
            yield
        done = []
        for n in range(nprob):
            t_off = jnp.where(left, xb[n] - eye_lo, xa[n] - eye_hi).astype(BF16)
            ra, rb = hr_ref[parity, n, 0], hr_ref[parity, n, 1]
            zr = jnp.zeros((CHUNK, 2 * GDN_HEAD_DIM), BF16)
            out = _dot(t_off, block_diag(rb.astype(BF16), ra.astype(BF16), zr))
            rb = rb + out[:, :2 * GDN_HEAD_DIM]
            ra = ra + out[:, 2 * GDN_HEAD_DIM:]
            done.append((jnp.concatenate([ra[:, :GDN_HEAD_DIM], rb[:, :GDN_HEAD_DIM]], axis=1),
                         jnp.concatenate([ra[:, GDN_HEAD_DIM:], rb[:, GDN_HEAD_DIM:]], axis=1).astype(BF16)))
        yield
        for n, (c, p) in enumerate(probs):
            u_ref[slot, c, p], wqg_ref[slot, c, p, :CHUNK, :] = done[n]

    def scan_chunks(g, d):
        wqg_ref, u_ref, ik_ref, eg_ref = scr[6 * d:6 * d + 4]
        o_ref = (of_ref, ob_ref)[d]
        for cc in range(CHUNKS_PER_STEP):
            c = scan_chunk(g * CHUNKS_PER_STEP + cc, d)
            r0 = pl.multiple_of(c * CHUNK, CHUNK)
            states = [s_ref[d, p] for p in range(npairs)]
            m1 = [_dot(wqg_ref[slot_scan, c, p],
                       block_diag(states[p][:, :GDN_HEAD_DIM].astype(BF16), states[p][:, GDN_HEAD_DIM:].astype(BF16), zs))
                  for p in range(npairs)]
            yield
            v_new = [(u_ref[slot_scan, c, p] - m1[p][:CHUNK]).astype(BF16) for p in range(npairs)]
            m2 = [_dot(ik_ref[slot_scan, c, p], block_diag(v_new[p][:, :GDN_HEAD_DIM], v_new[p][:, GDN_HEAD_DIM:], zc))
                  for p in range(npairs)]
            for p in range(npairs):
                ps = slice(2 * p * GDN_HEAD_DIM, 2 * (p + 1) * GDN_HEAD_DIM)
                o_ref[0, pl.ds(r0, CHUNK), ps] = m1[p][CHUNK:] + m2[p][:CHUNK]
                s_ref[d, p] = states[p] * eg_ref[slot_scan, c, p][0:1, :] + m2[p][CHUNK:]
            yield

    def iteration(it, first, second, scan):
        live = []
        for d in range(2):
            if first:
                live.append(build_first(it, d))
            if second:
                if isinstance(it, int):
                    g2, slot2 = (groups - 1, slot_scan) if it == 0 else (it - 1, slot_build)
                else:
                    g2 = jnp.where(it == 0, groups - 1, it - 1)
                    slot2 = jnp.where(it == 0, slot_scan, slot_build)
                live.append(build_second(g2, slot2, d))
            if scan:
                live.append(scan_chunks(it, d))
        while live:
            live = [gen for gen in live if next(gen, StopIteration) is not StopIteration]

    def loop(lo, first, second, scan):
        def body(it, carry):
            iteration(it, first, second, scan)
            return carry
        lax.fori_loop(lo, groups, body, 0)

    @pl.when(j == 0)
    def _():
        s_ref[...] = jnp.zeros(s_ref.shape, F32)
        iteration(0, True, False, False)
        loop(1, True, True, False)

    @pl.when((j > 0) & (j < nblk))
    def _():
        loop(0, True, True, True)

    @pl.when(j == nblk)
    def _():
        iteration(0, False, True, True)
        loop(1, False, False, True)


def _gdn_scan(q, k, v, gcol, grow, *, ts):
    b, nh, s, hd = q.shape
    w = nh * hd
    nblk = s // ts
    nc = ts // CHUNK
    assert nblk >= 2 and (nc // CHUNKS_PER_STEP) % 2 == 0
    order = (lambda t: t, lambda t: nblk - 1 - t)
    in_specs, out_specs, scratch = [], [], [pltpu.VMEM((2, nh // 2, hd, 2 * hd), F32)]
    for d in range(2):
        src = lambda j, d=d: order[d](jnp.minimum(j, nblk - 1))
        dst = lambda j, d=d: order[d](jnp.maximum(j - 1, 0))
        in_specs += [pl.BlockSpec((1, nh, ts, hd), lambda i, j, src=src: (i, 0, src(j), 0))] * 3
        in_specs += [pl.BlockSpec((1, ts, LANES), lambda i, j, src=src: (i, src(j), 0)),
                     pl.BlockSpec((1, nc, 8, 2 * CHUNK), lambda i, j, src=src: (i, src(j), 0, 0))]
        out_specs.append(pl.BlockSpec((1, ts, w), lambda i, j, dst=dst: (i, dst(j), 0)))
        nprob = CHUNKS_PER_STEP * (nh // 2)
        scratch += [pltpu.VMEM((2, nc, nh // 2, 2 * CHUNK, 2 * hd), BF16),
                    pltpu.VMEM((2, nc, nh // 2, CHUNK, 2 * hd), F32),
                    pltpu.VMEM((2, nc, nh // 2, CHUNK + hd, 2 * CHUNK), BF16),
                    pltpu.VMEM((2, nc, nh // 2, 8, 2 * hd), F32),
                    pltpu.VMEM((nprob, 2, CHUNK, 2 * CHUNK), F32),
                    pltpu.VMEM((2, nprob, 2, CHUNK, 2 * hd), F32)]
    return pl.pallas_call(
        functools.partial(_gdn_scan_kernel, nc=nc, nblk=nblk),
        grid=(b, nblk + 1),
        in_specs=in_specs,
        out_specs=out_specs,
        out_shape=[jax.ShapeDtypeStruct((b, s, w), F32)] * 2,
        scratch_shapes=scratch,
        compiler_params=_cparams(("arbitrary", "arbitrary")),
        name="gdn_scan",
    )(*([q, k, v, gcol, grow] * 2))


SWA_TQ = 128
SWA_TK = SWA_TQ + 2 * BAND_RADIUS
SWA_HALO = BAND_RADIUS * max(DILATIONS)
COPY_ROWS = 512
SWA_TILES_PER_STEP = 2


def _swa_kernel(q_ref, k_ref, v_ref, bias_ref, o_ref,
                kn_ref, vp_ref, m0_ref, m1_ref, l_ref, acc_ref, bias_scr, *, seq, qb):
    qi = pl.program_id(2)
    qn_ref = q_ref.at[0]
    lane = lax.broadcasted_iota(jnp.int32, (1, LANES), 1)
    low = lane < SWA_HEAD_DIM
    nt = (((1,), (1,)), ((), ()))

    @pl.when(qi == 0)
    def _():
        for pi in range(len(DILATIONS)):
            for hh in range(2):
                full = jnp.broadcast_to(bias_ref[0, pi, hh], (SWA_TQ, SWA_TK))
                bias_scr[pi, hh] = pltpu.roll(full, 0, 1, stride=1, stride_axis=0)

        zeros = jnp.zeros((SWA_HALO, LANES), F32)
        for ref in (kn_ref, vp_ref):
            ref[0:SWA_HALO, :] = zeros
            ref[SWA_HALO + seq:2 * SWA_HALO + seq, :] = zeros

        def copy_rows(i, carry):
            r0 = pl.multiple_of(i * COPY_ROWS, COPY_ROWS)
            kn_ref[pl.ds(SWA_HALO + r0, COPY_ROWS), :] = k_ref[0, pl.ds(r0, COPY_ROWS), :]
            vp_ref[pl.ds(SWA_HALO + r0, COPY_ROWS), :] = v_ref[0, pl.ds(r0, COPY_ROWS), :]
            return carry

        lax.fori_loop(0, seq // COPY_ROWS, copy_rows, 0)

    kcol = lax.broadcasted_iota(jnp.int32, (1, SWA_TK), 1)

    def rows(ref, start, size, stride):
        if stride == 1:
            return ref[pl.ds(start, size), :]
        return ref[pl.ds(start, size, stride=stride), :]

    def put(ref, start, size, stride, val):
        if stride == 1:
            ref[pl.ds(start, size), :] = val
        else:
            ref[pl.ds(start, size, stride=stride), :] = val

    def logits(pi, dil, g):
        sub_len = seq // dil
        tiles_per_res = qb // dil // SWA_TQ
        ss = []
        for u in range(SWA_TILES_PER_STEP):
            res, t = divmod(g * SWA_TILES_PER_STEP + u, tiles_per_res)
            tau0 = qi * (qb // dil) + t * SWA_TQ
            loc0 = res + dil * (t * SWA_TQ)
            krow = SWA_HALO + qi * qb + loc0 - dil * BAND_RADIUS
            qt = rows(qn_ref, qi * qb + loc0, SWA_TQ, dil)
            kt = rows(kn_ref, krow, SWA_TK, dil).astype(BF16)
            at_end = t in (0, tiles_per_res - 1)
            kidx = kcol + (tau0 - BAND_RADIUS)
            kvalid = (kidx >= 0) & (kidx < sub_len)
            for hh in range(2):
                mine = low if hh == 0 else jnp.logical_not(low)
                qh = jnp.where(mine, qt, 0.0).astype(BF16)
                s = lax.dot_general(qh, kt, nt, preferred_element_type=F32) + bias_scr[pi, hh]
                ss.append(jnp.where(kvalid, s, NEG_BIG) if at_end else s)
        return ss

    def accumulate(dil, g, ss, first):
        tiles_per_res = qb // dil // SWA_TQ
        loc, vts, old = [], [], []
        for u in range(SWA_TILES_PER_STEP):
            res, t = divmod(g * SWA_TILES_PER_STEP + u, tiles_per_res)
            loc0 = res + dil * (t * SWA_TQ)
            vt = rows(vp_ref, SWA_HALO + qi * qb + loc0 - dil * BAND_RADIUS, SWA_TK, dil)
            vts.append((jnp.where(low, vt, 1.0).astype(BF16), jnp.where(low, 1.0, vt).astype(BF16)))
            loc.append(loc0)
            if not first:
                old.append(tuple(rows(ref, loc0, SWA_TQ, dil) for ref in (m0_ref, m1_ref, l_ref, acc_ref)))
        m_new, alpha, ps = [], [], []
        for n, s in enumerate(ss):
            u, hh = divmod(n, 2)
            mt = jnp.max(s, axis=-1, keepdims=True)
            if first:
                mn = jnp.broadcast_to(mt, (SWA_TQ, LANES))
            else:
                mo = old[u][hh]
                mn = jnp.maximum(mo, mt)
                alpha.append(jnp.exp2(mo - mn))
            m_new.append(mn)
            ps.append(jnp.exp2(s - jnp.concatenate([mn, mn], axis=1)).astype(BF16))
        pv = [_dot(p, vts[n // 2][n % 2]) for n, p in enumerate(ps)]
        for u in range(SWA_TILES_PER_STEP):
            pv0, pv1 = pv[2 * u], pv[2 * u + 1]
            put(m0_ref, loc[u], SWA_TQ, dil, m_new[2 * u])
            put(m1_ref, loc[u], SWA_TQ, dil, m_new[2 * u + 1])
            lsum = jnp.where(low, pv1, pv0)
            pvv = jnp.where(low, pv0, pv1)
            if first:
                put(l_ref, loc[u], SWA_TQ, dil, lsum)
                put(acc_ref, loc[u], SWA_TQ, dil, pvv)
            else:
                a0, a1 = alpha[2 * u], alpha[2 * u + 1]
                put(l_ref, loc[u], SWA_TQ, dil, old[u][2] * jnp.where(low, a1, a0) + lsum)
                put(acc_ref, loc[u], SWA_TQ, dil, old[u][3] * jnp.where(low, a0, a1) + pvv)

    order = sorted(range(len(DILATIONS)), key=lambda pi: -DILATIONS[pi])
    units = [(pi, DILATIONS[pi], g) for pi in order for g in range(qb // SWA_TQ // SWA_TILES_PER_STEP)]
    pending = logits(*units[0])
    for k, (pi, dil, g) in enumerate(units):
        nxt = logits(*units[k + 1]) if k + 1 < len(units) else None
        accumulate(dil, g, pending, first=pi == order[0])
        pending = nxt

    o_ref[0] = acc_ref[...] / pltpu.roll(l_ref[...], SWA_HEAD_DIM, 1)


def _t5_bucket(rel):
    nb = REL_BUCKETS // 2
    bucket = (rel > 0).astype(np.int32) * nb
    n = np.abs(rel)
    max_exact = nb // 2
    large = max_exact + (np.log(np.maximum(n, 1) / max_exact)
                         / math.log(REL_MAX_DISTANCE / max_exact) * (nb - max_exact)).astype(np.int32)
    large = np.minimum(large, nb - 1)
    return (bucket + np.where(n < max_exact, n, large)).astype(np.int32)


def _band_bias(rel_bias):
    rel = np.arange(2 * BAND_RADIUS + 1) - BAND_RADIUS
    rows = []
    for dil in DILATIONS:
        inside = jnp.transpose(rel_bias[_t5_bucket(rel * dil)]) * LOG2E
        rows.append(jnp.pad(inside, ((0, 0), (0, SWA_TK - inside.shape[1])), constant_values=NEG_BIG))
    bias = jnp.stack(rows, axis=0)
    bias = bias.reshape(len(DILATIONS), SWA_HEADS // 2, 2, 1, SWA_TK)
    return jnp.transpose(bias, (1, 0, 2, 3, 4)).astype(F32)


def _swa(qkvb, bias, *, qb):
    b, s, _ = qkvb.shape
    pairs = SWA_HEADS // 2
    assert s % qb == 0 and qb % (max(DILATIONS) * SWA_TQ) == 0
    col = lambda base: pl.BlockSpec((1, s, LANES), lambda i, p, j: (i, 0, base + p))
    padded = s + 2 * SWA_HALO
    return pl.pallas_call(
        functools.partial(_swa_kernel, seq=s, qb=qb),
        grid=(b, pairs, s // qb),
        in_specs=[col(0), col(pairs), col(2 * pairs),
                  pl.BlockSpec((1,) + bias.shape[1:], lambda i, p, j: (p, 0, 0, 0, 0))],
        out_specs=pl.BlockSpec((1, qb, LANES), lambda i, p, j: (i, j, p)),
        out_shape=jax.ShapeDtypeStruct((b, s, SWA_WIDTH), F32),
        scratch_shapes=[pltpu.VMEM((padded, LANES), F32), pltpu.VMEM((padded, LANES), F32)]
                       + [pltpu.VMEM((qb, LANES), F32)] * 4
                       + [pltpu.VMEM((len(DILATIONS), 2, SWA_TQ, SWA_TK), F32)],
        compiler_params=_cparams(("arbitrary", "arbitrary", "arbitrary")),
        name="swa",
    )(qkvb, qkvb, qkvb, bias)


FFN_TM = 512
FFN_FCHUNK = 256
SCAN_TS = 512
SWA_QB = 2048


def _lane_row(vals, copies_at):
    row = jnp.zeros((LANES,), F32)
    for off in copies_at:
        row = row.at[off:off + vals.shape[0]].set(vals)
    return row[None, :]


def kernel(x, ffn1_norm, ffn1_w_gate, ffn1_w_up, ffn1_w_down, mix_norm, w_in, conv_w, a_log, dt_bias, gdn_norm_w, q_norm_w, k_norm_w, rel_bias, w_out, ffn2_norm, ffn2_w_gate, ffn2_w_up, ffn2_w_down, final_norm):
    b, s, d = x.shape
    n = b * s
    x2d = x.reshape(n, d)
    ng = 2 * GDN_HEADS
    c_z = 3 * GDN_WIDTH
    c_a = c_z + GDN_WIDTH
    c_b = c_a + ng
    c_qkvb = c_b + ng
    bias = _band_bias(rel_bias)
    for l in range(ffn1_norm.shape[0]):
        wl = w_in[l]
        w_a = wl[:, c_a:c_b]
        w_gates = jnp.concatenate([w_a, wl[:, c_b:c_qkvb], w_a, w_a, jnp.zeros((d, LANES - 4 * ng), F32)], axis=1)
        win = (wl[:, :c_a].astype(BF16), w_gates.astype(BF16), wl[:, c_qkvb:].astype(BF16))
        copies = (0, 2 * ng, 3 * ng)
        alog = _lane_row(a_log[l].reshape(ng), copies)
        dtb = _lane_row(dt_bias[l].reshape(ng), copies)
        x1, z, qkvb, q, k, v, gcol, grow = _ffn1_proj(
            x2d, ffn1_norm[l][None, :], ffn1_w_gate[l].astype(BF16), ffn1_w_up[l].astype(BF16),
            ffn1_w_down[l].astype(BF16), mix_norm[l][None, :], win,
            jnp.tile(q_norm_w[l], 2)[None, :], jnp.tile(k_norm_w[l], 2)[None, :],
            jnp.transpose(conv_w[l]), alog, dtb, batch=b, tm=FFN_TM, fchunk=FFN_FCHUNK)
        o_fwd, o_bwd = _gdn_scan(q, k, v, gcol.reshape(b, s, LANES), grow.reshape(b, s // CHUNK, 8, 2 * CHUNK),
                                 ts=SCAN_TS)
        attn = _swa(qkvb.reshape(b, s, -1), bias, qb=SWA_QB)
        x2d = _out_ffn2(
            x1, o_fwd.reshape(n, -1), o_bwd.reshape(n, -1), z, attn.reshape(n, -1),
            gdn_norm_w[l][None, :], w_out[l].astype(BF16), ffn2_norm[l][None, :],
            ffn2_w_gate[l].astype(BF16), ffn2_w_up[l].astype(BF16), ffn2_w_down[l].astype(BF16),
            final_norm[l][None, :], tm=FFN_TM, fchunk=FFN_FCHUNK)
    return x2d.reshape(b, s, d)
```

```python
import functools
import math

import numpy as np
import jax
import jax.numpy as jnp
from jax import lax
from jax.experimental import pallas as pl
from jax.experimental.pallas import tpu as pltpu

F32 = jnp.float32
BF16 = jnp.bfloat16
EPS = 1e-6
NEG_BIG = -1e30
LOG2E = math.log2(math.e)

LANES = 128
GDN_HEADS = 4
GDN_HEAD_DIM = 128
GDN_WIDTH = GDN_HEADS * GDN_HEAD_DIM
CONV_WIDTH = 5
CHUNK = 64
SWA_HEADS = 8
SWA_HEAD_DIM = 64
SWA_WIDTH = SWA_HEADS * SWA_HEAD_DIM
DILATIONS = (1, 4, 16)
BAND_RADIUS = 64
REL_BUCKETS = 32
REL_MAX_DISTANCE = 1024
VMEM_LIMIT = 56 * 1024 * 1024


def _cparams(sem):
    return pltpu.CompilerParams(dimension_semantics=sem, vmem_limit_bytes=VMEM_LIMIT)


def _resident(shape):
    zeros = (0,) * len(shape)
    return pl.BlockSpec(shape, lambda *_: zeros, pipeline_mode=pl.Buffered(1))


def _rms(x, w):
    return x * lax.rsqrt(jnp.mean(x * x, axis=-1, keepdims=True) + EPS) * w


def _silu(x):
    return x * (1.0 / (1.0 + jnp.exp(-x)))


def _dot(a, b):
    return jnp.dot(a, b, preferred_element_type=F32)


def _zero_after(values):
    s = values[0:8]
    for r in range(8, values.shape[0], 8):
        s = s + values[r:r + 8]
    bits = lax.bitcast_convert_type(s, jnp.int32)
    return lax.shift_right_logical(lax.shift_right_logical(bits, 16), 16).astype(F32)


def _swiglu(h, wg_ref, wu_ref, wd_ref, fchunk, order_after=()):
    acc = None
    for n, c0 in enumerate(range(0, wg_ref.shape[1], fchunk)):
        g = _dot(h, wg_ref[:, c0:c0 + fchunk])
        if n < len(order_after):
            first = jnp.concatenate([g[0:8, 0:LANES] + order_after[n], g[0:8, LANES:]], axis=1)
            g = jnp.concatenate([first, g[8:]], axis=0)
        u = _dot(h, wu_ref[:, c0:c0 + fchunk])
        a = (_silu(g) * u).astype(BF16)
        d = _dot(a, wd_ref[c0:c0 + fchunk, :])
        acc = d if acc is None else acc + d
    return acc


def _ffn1_proj_kernel(x_ref, n1_ref, wg_ref, wu_ref, wd_ref, nm_ref, wa_ref, wgt_ref, wb_ref, qw_ref, kw_ref,
                      cw_ref, alog_ref, dtb_ref,
                      x1_ref, z_ref, qkvb_ref, q_ref, k_ref, v_ref, gcol_ref, grow_ref, pad_ref, tail_ref,
                      *, fchunk, tiles_per_seq, num_tiles):
    i = pl.program_id(0)
    tm = x_ref.shape[0]
    heads_out = (q_ref, k_ref, v_ref)

    @pl.when(i == 0)
    def _():
        pad_ref[...] = jnp.zeros(pad_ref.shape, F32)

    def conv_tail(cb, next_rows):
        tail_ref[cb, 0:8 + CONV_TAIL, :] = pad_ref[cb, tm - CONV_TAIL:8 + tm, :]
        tail_ref[cb, 8 + CONV_TAIL:16 + CONV_TAIL, :] = next_rows
        _gdn_conv_head(tail_ref.at[cb], cw_ref, cb, heads_out[cb // GDN_HEADS], tm - CONV_TAIL, CONV_TAIL)

    @pl.when(i == num_tiles)
    def _():
        for cb in range(3 * GDN_HEADS):
            _gdn_conv_head(pad_ref.at[cb], cw_ref, cb, heads_out[cb // GDN_HEADS], 0, tm - CONV_TAIL)
            conv_tail(cb, jnp.zeros((8, LANES), F32))

    @pl.when(i < num_tiles)
    def _():
        _ffn1_proj_tile(i, x_ref, n1_ref, wg_ref, wu_ref, wd_ref, nm_ref, (wa_ref, wgt_ref, wb_ref), qw_ref, kw_ref,
                        cw_ref, alog_ref, dtb_ref, x1_ref, z_ref, qkvb_ref, heads_out, gcol_ref, grow_ref,
                        pad_ref, conv_tail, fchunk, tiles_per_seq)


def _ffn1_proj_tile(i, x_ref, n1_ref, wg_ref, wu_ref, wd_ref, nm_ref, win_refs, qw_ref, kw_ref,
                    cw_ref, alog_ref, dtb_ref, x1_ref, z_ref, qkvb_ref, heads_out, gcol_ref, grow_ref,
                    pad_ref, conv_tail, fchunk, tiles_per_seq):
    tm = x_ref.shape[0]
    conv_done = []
    for cb in range(3 * GDN_HEADS):
        even, odd = _gdn_conv_head(pad_ref.at[cb], cw_ref, cb, heads_out[cb // GDN_HEADS], 0, tm - CONV_TAIL)
        conv_done.append(_zero_after(even + odd))
    nchunks = wg_ref.shape[1] // fchunk
    order_after = conv_done[:nchunks - 1] + [functools.reduce(lambda a, b: a + b, conv_done[nchunks - 1:])]

    seq_start = lax.rem(i, tiles_per_seq) == 0
    x = x_ref[...]
    h = _rms(x, n1_ref[...]).astype(BF16)
    x1 = x + 0.5 * _swiglu(h, wg_ref, wu_ref, wd_ref, fchunk, order_after)
    x1_ref[...] = x1
    h2 = _rms(x1, nm_ref[...]).astype(BF16)
    wa_ref, wgt_ref, wb_ref = win_refs
    qkva = _dot(h2, wa_ref[:, :3 * GDN_WIDTH])
    z_ref[...] = _dot(h2, wa_ref[:, 3 * GDN_WIDTH:])
    _gdn_gates(_dot(h2, wgt_ref[...]), alog_ref[...], dtb_ref[...], gcol_ref, grow_ref)

    for cb in range(3 * GDN_HEADS):
        cs = slice(cb * LANES, (cb + 1) * LANES)
        conv_tail(cb, jnp.where(seq_start, 0.0, qkva[0:8, cs]))
        pad_ref[cb, 0:8, :] = jnp.where(seq_start, 0.0, pad_ref[cb, tm:tm + 8, :])
        pad_ref[cb, 8:8 + tm, :] = qkva[:, cs]

    low = lax.broadcasted_iota(jnp.int32, (1, LANES), 1) < SWA_HEAD_DIM

    def head_rms(y, w):
        y2 = y * y
        s0 = jnp.sum(jnp.where(low, y2, 0.0), axis=-1, keepdims=True)
        s1 = jnp.sum(jnp.where(low, 0.0, y2), axis=-1, keepdims=True)
        ms = jnp.where(low, s0, s1) * (1.0 / SWA_HEAD_DIM)
        return y * lax.rsqrt(ms + EPS) * w

    qscale = qw_ref[...] * (SWA_HEAD_DIM ** -0.5 * LOG2E)
    wide = 2 * LANES
    for j in range(3 * SWA_WIDTH // wide):
        y = _dot(h2, wb_ref[:, j * wide:(j + 1) * wide])
        for half in range(2):
            p = 2 * j + half
            yp = y[:, half * LANES:(half + 1) * LANES]
            if p < SWA_WIDTH // LANES:
                yp = head_rms(yp, qscale)
            elif p < 2 * SWA_WIDTH // LANES:
                yp = head_rms(yp, kw_ref[...])
            qkvb_ref[:, p * LANES:(p + 1) * LANES] = yp


def _ffn1_proj(x2d, n1, wg, wu, wd, nm, win, qw, kw, cw, alog, dtb, *, batch, tm, fchunk):
    n, d = x2d.shape
    f = wg.shape[1]
    seq = n // batch
    assert [w.shape[1] for w in win] == [4 * GDN_WIDTH, LANES, 3 * SWA_WIDTH] and seq % tm == 0
    nt = n // tm
    tps = seq // tm
    cur = lambda i: jnp.minimum(i, nt - 1)
    late = lambda i: jnp.maximum(i - 1, 0)
    row = lambda w: pl.BlockSpec((tm, w), lambda i: (cur(i), 0))
    heads = pl.BlockSpec((1, GDN_HEADS, tm, GDN_HEAD_DIM), lambda i: (late(i) // tps, 0, late(i) % tps, 0))
    f32 = lambda *shape: jax.ShapeDtypeStruct(shape, F32)
    return pl.pallas_call(
        functools.partial(_ffn1_proj_kernel, fchunk=fchunk, tiles_per_seq=tps, num_tiles=nt),
        grid=(nt + 1,),
        in_specs=[row(d), _resident((1, d)), _resident((d, f)), _resident((d, f)), _resident((f, d)),
                  _resident((1, d))] + [_resident(w.shape) for w in win] + [_resident((1, LANES)), _resident((1, LANES)),
                  _resident(cw.shape), _resident((1, LANES)), _resident((1, LANES))],
        out_specs=[row(d), row(GDN_WIDTH), row(3 * SWA_WIDTH), heads, heads, heads, row(LANES),
                   pl.BlockSpec((tm // CHUNK, 8, 2 * CHUNK), lambda i: (cur(i), 0, 0))],
        out_shape=[f32(n, d), f32(n, GDN_WIDTH), f32(n, 3 * SWA_WIDTH)]
                  + [f32(batch, GDN_HEADS, seq, GDN_HEAD_DIM)] * 3 + [f32(n, LANES), f32(n // CHUNK, 8, 2 * CHUNK)],
        scratch_shapes=[pltpu.VMEM((3 * GDN_HEADS, tm + 8, LANES), F32),
                        pltpu.VMEM((3 * GDN_HEADS, CONV_TAIL + 16, LANES), F32)],
        compiler_params=_cparams(("arbitrary",)),
        name="ffn1_proj",
    )(x2d, n1, wg, wu, wd, nm, *win, qw, kw, cw, alog, dtb)


def _out_ffn2_kernel(x1_ref, of_ref, ob_ref, z_ref, attn_ref, gnw_ref, wout_ref, n2_ref,
                     wg_ref, wu_ref, wd_ref, nf_ref, out_ref, *, fchunk):
    o = of_ref[...] + ob_ref[...]
    z = z_ref[...]
    gnw = gnw_ref[...]
    heads = []
    for h in range(GDN_HEADS):
        sl = slice(h * GDN_HEAD_DIM, (h + 1) * GDN_HEAD_DIM)
        heads.append(_rms(o[:, sl], gnw) * _silu(z[:, sl]))
    oa = jnp.concatenate(heads, axis=1).astype(BF16)
    mix = _dot(oa, wout_ref[:GDN_WIDTH, :]) + _dot(attn_ref[...].astype(BF16), wout_ref[GDN_WIDTH:, :])
    x2 = x1_ref[...] + mix
    h2 = _rms(x2, n2_ref[...]).astype(BF16)
    x3 = x2 + 0.5 * _swiglu(h2, wg_ref, wu_ref, wd_ref, fchunk)
    out_ref[...] = _rms(x3, nf_ref[...])


def _out_ffn2(x1, o_fwd, o_bwd, z, attn, gnw, wout, n2, wg, wu, wd, nf, *, tm, fchunk):
    n, d = x1.shape
    f = wg.shape[1]
    row = lambda w: pl.BlockSpec((tm, w), lambda i: (i, 0))
    return pl.pallas_call(
        functools.partial(_out_ffn2_kernel, fchunk=fchunk),
        grid=(n // tm,),
        in_specs=[row(d), row(GDN_WIDTH), row(GDN_WIDTH), row(GDN_WIDTH), row(SWA_WIDTH),
                  _resident((1, GDN_HEAD_DIM)), _resident(wout.shape), _resident((1, d)),
                  _resident((d, f)), _resident((d, f)), _resident((f, d)), _resident((1, d))],
        out_specs=row(d),
        out_shape=jax.ShapeDtypeStruct((n, d), F32),
        compiler_params=_cparams(("arbitrary",)),
        name="out_ffn2",
    )(x1, o_fwd, o_bwd, z, attn, gnw, wout, n2, wg, wu, wd, nf)


CONV_TAIL = 16


def _gdn_conv_head(src, cw_ref, cb, dst, out0, count):
    cs = slice(cb * LANES, (cb + 1) * LANES)
    half = count // 2
    done = []
    for par in range(2):
        acc = None
        for t in range(CONV_WIDTH):
            term = src[pl.ds(6 + t + par, half, stride=2), :] * cw_ref[t:t + 1, cs]
            acc = term if acc is None else acc + term
        y = _silu(acc)
        if cb < 2 * GDN_HEADS:
            y = y * lax.rsqrt(jnp.sum(y * y, axis=-1, keepdims=True) + EPS)
        if cb < GDN_HEADS:
            y = y * (GDN_HEAD_DIM ** -0.5)
        dst[0, cb % GDN_HEADS, pl.ds(out0 + par, half, stride=2), :] = y
        done.append(y)
    return done


def _gdn_gates(ab, alog, dtb, gcol_ref, grow_ref):
    ts = ab.shape[0]
    xs = ab + dtb
    softplus = jnp.maximum(xs, 0.0) + jnp.log(1.0 + jnp.exp(-jnp.abs(xs)))
    g = -jnp.exp(alog) * softplus
    beta = 1.0 / (1.0 + jnp.exp(-ab))
    g1 = g.astype(BF16)
    r1 = g - g1.astype(F32)
    g2 = r1.astype(BF16)
    g3 = (r1 - g2.astype(F32)).astype(BF16)
    ri = lax.broadcasted_iota(jnp.int32, (LANES, LANES), 0)
    ci = lax.broadcasted_iota(jnp.int32, (LANES, LANES), 1)
    same = (ri // CHUNK) == (ci // CHUNK)
    lower = jnp.where(same & (ri >= ci), 1.0, 0.0).astype(BF16)
    upper = jnp.where(same & (ri <= ci), 1.0, 0.0).astype(BF16)
    lane = lax.broadcasted_iota(jnp.int32, (1, LANES), 1)
    for s in range(ts // LANES):
        rs = slice(s * LANES, (s + 1) * LANES)
        pre = _dot(lower, g1[rs]) + _dot(lower, g2[rs]) + _dot(lower, g3[rs])
        suf = _dot(upper, g1[rs]) + _dot(upper, g2[rs]) + _dot(upper, g3[rs])
        gs = g[rs]
        col = jnp.where(lane < 4, pre,
              jnp.where(lane < 8, suf,
              jnp.where(lane < 16, beta[rs],
              jnp.where(lane < 20, suf - gs,
              jnp.where(lane < 24, pre - gs, pre + suf - gs)))))
        gcol_ref[rs, :] = col
        rows = col.T[0:8]
        swapped = pltpu.roll(rows, CHUNK, 1)
        grow_ref[2 * s] = jnp.where(lane < CHUNK, rows, swapped)
        grow_ref[2 * s + 1] = jnp.where(lane < CHUNK, swapped, rows)


CHUNKS_PER_STEP = 2

def _gdn_scan_kernel(*refs, nc, nblk, ncast):
    ins, cast_in = refs[:10], refs[10:10 + ncast]
    of_ref, ob_ref = refs[10 + ncast:12 + ncast]
    cast_out = refs[12 + ncast:12 + 2 * ncast]
    s_ref, scr = refs[12 + 2 * ncast], refs[13 + 2 * ncast:]
    for src, dst in zip(cast_in, cast_out):
        dst[...] = src[...].astype(BF16)
    j = pl.program_id(1)
    slot_build = lax.rem(j, 2)
    slot_scan = 1 - slot_build

    ri = lax.broadcasted_iota(jnp.int32, (CHUNK, 2 * CHUNK), 0)
    li = lax.broadcasted_iota(jnp.int32, (CHUNK, 2 * CHUNK), 1)
    left = li < CHUNK
    eye_lo = jnp.where(li == ri, 1.0, 0.0)
    eye_hi = jnp.where(li == ri + CHUNK, 1.0, 0.0)
    nt = (((1,), (1,)), ((), ()))
    zc = jnp.zeros((CHUNK, GDN_HEAD_DIM), BF16)
    zs = jnp.zeros((GDN_HEAD_DIM, GDN_HEAD_DIM), BF16)
    npairs = GDN_HEADS // 2

    def block_diag(a, b, zero):
        return jnp.concatenate([jnp.concatenate([a, zero], axis=1), jnp.concatenate([zero, b], axis=1)], axis=0)

    nsq = int(math.log2(CHUNK))

    groups = nc // CHUNKS_PER_STEP
    nprob = CHUNKS_PER_STEP * npairs

    def scan_chunk(i, d):
        return nc - 1 - i if d else i

    def neumann_step(xa, xb):
        ya = xa.astype(BF16)
        yb = xb.astype(BF16)
        zz = jnp.zeros((CHUNK, 2 * CHUNK), BF16)
        y = _dot(jnp.where(left, ya, yb), block_diag(ya, yb, zz))
        return (jnp.where(left, y[:, :2 * CHUNK], xa + y[:, :2 * CHUNK]),
                jnp.where(left, xb + y[:, 2 * CHUNK:], y[:, 2 * CHUNK:]))

    def build_first(g, d):
        parity = g % 2 if isinstance(g, int) else lax.rem(g, 2)
        q_ref, k_ref, v_ref, gcol_ref, grow_ref = ins[5 * d:5 * d + 5]
        wqg_ref, u_ref, ik_ref, eg_ref, hx_ref, hr_ref = scr[6 * d:6 * d + 6]
        dd = (ri - jnp.where(left, li, li - CHUNK)) * (1 - 2 * d)
        incl = dd >= 0
        strict = dd > 0
        probs = [(scan_chunk(g * CHUNKS_PER_STEP + cc, d), p) for cc in range(CHUNKS_PER_STEP) for p in range(npairs)]
        xa, xb, done = [], [], []
        for n, (c, p) in enumerate(probs):
            r0 = pl.multiple_of(c * CHUNK, CHUNK)
            gates = gcol_ref[0, pl.ds(r0, CHUNK), :]
            rows = grow_ref[0, c]
            per_head = []
            for h in (2 * p, 2 * p + 1):
                col = GDN_HEADS * d + h
                gc, beta, gdec, gl = (gates[:, base + col:base + col + 1] for base in (0, 8, 16, 24))
                qc = q_ref[0, h, pl.ds(r0, CHUNK), :]
                kc = k_ref[0, h, pl.ds(r0, CHUNK), :]
                vc = v_ref[0, h, pl.ds(r0, CHUNK), :]
                kb = kc * beta
                eg = jnp.exp(gc)
                per_head.append(dict(gc=gc, grow=rows[col:col + 1, :], q=qc, k=kc, kb=kb,
                                     r=jnp.concatenate([vc * beta, kb * eg], axis=1),
                                     qg=qc * eg, kd=kc * jnp.exp(gdec), eg=jnp.exp(gl[0:1, :])))
            a, b = per_head
            dec = jnp.where(incl, jnp.exp(jnp.where(incl, jnp.where(left, a["gc"], b["gc"])
                                                    - jnp.where(left[0:1], a["grow"], b["grow"]), 0.0)), 0.0)
            lhs = jnp.concatenate([jnp.concatenate([a["kb"], b["kb"]], axis=1),
                                   jnp.concatenate([a["q"], b["q"]], axis=1)], axis=0).astype(BF16)
            qk = lax.dot_general(lhs, block_diag(a["k"].astype(BF16), b["k"].astype(BF16), zc), nt,
                                 preferred_element_type=F32)
            neg_a = -jnp.where(strict, qk[:CHUNK] * dec, 0.0)
            xa.append(jnp.where(left, neg_a, eye_hi))
            xb.append(jnp.where(left, eye_lo, neg_a))
            done.append((jnp.concatenate([a["qg"], b["qg"]], axis=1).astype(BF16),
                         jnp.concatenate([qk[CHUNK:] * dec, jnp.concatenate([a["kd"], b["kd"]], axis=0).T],
                                         axis=0).astype(BF16),
                         jnp.concatenate([jnp.broadcast_to(a["eg"], (8, GDN_HEAD_DIM)),
                                          jnp.broadcast_to(b["eg"], (8, GDN_HEAD_DIM))], axis=1)))
            hr_ref[parity, n, 0] = a["r"]
            hr_ref[parity, n, 1] = b["r"]
        yield
        for t in range(nsq // 2):
            for n in range(nprob):
                xa[n], xb[n] = neumann_step(xa[n], xb[n])
            yield
        for n, (c, p) in enumerate(probs):
            hx_ref[n, 0] = xa[n]
            hx_ref[n, 1] = xb[n]
            wqg_ref[slot_build, c, p, CHUNK:, :], ik_ref[slot_build, c, p], eg_ref[slot_build, c, p] = done[n]

    def build_second(g, slot, d):
        parity = g % 2 if isinstance(g, int) else lax.rem(g, 2)
        wqg_ref, u_ref, ik_ref, eg_ref, hx_ref, hr_ref = scr[6 * d:6 * d + 6]
        probs = [(scan_chunk(g * CHUNKS_PER_STEP + cc, d), p) for cc in range(CHUNKS_PER_STEP) for p in range(npairs)]
        xa = [hx_ref[n, 0] for n in range(nprob)]
        xb = [hx_ref[n, 1] for n in range(nprob)]
        for t in range(nsq // 2, nsq):
            for n in range(nprob):
                xa[n], xb[n] = neumann_step(xa[n], xb[n])
            yield
        done = []
        for n in range(nprob):
            t_off = jnp.where(left, xb[n] - eye_lo, xa[n] - eye_hi).astype(BF16)
            ra, rb = hr_ref[parity, n, 0], hr_ref[parity, n, 1]
            zr = jnp.zeros((CHUNK, 2 * GDN_HEAD_DIM), BF16)
            out = _dot(t_off, block_diag(rb.astype(BF16), ra.astype(BF16), zr))
            rb = rb + out[:, :2 * GDN_HEAD_DIM]
            ra = ra + out[:, 2 * GDN_HEAD_DIM:]
            done.append((jnp.concatenate([ra[:, :GDN_HEAD_DIM], rb[:, :GDN_HEAD_DIM]], axis=1),
                         jnp.concatenate([ra[:, GDN_HEAD_DIM:], rb[:, GDN_HEAD_DIM:]], axis=1).astype(BF16)))
        yield
        for n, (c, p) in enumerate(probs):
            u_ref[slot, c, p], wqg_ref[slot, c, p, :CHUNK, :] = done[n]

    def scan_chunks(g, d):
        wqg_ref, u_ref, ik_ref, eg_ref = scr[6 * d:6 * d + 4]
        o_ref = (of_ref, ob_ref)[d]
        for cc in range(CHUNKS_PER_STEP):
            c = scan_chunk(g * CHUNKS_PER_STEP + cc, d)
            r0 = pl.multiple_of(c * CHUNK, CHUNK)
            states = [s_ref[d, p] for p in range(npairs)]
            m1 = [_dot(wqg_ref[slot_scan, c, p],
                       block_diag(states[p][:, :GDN_HEAD_DIM].astype(BF16), states[p][:, GDN_HEAD_DIM:].astype(BF16), zs))
                  for p in range(npairs)]
            yield
            v_new = [(u_ref[slot_scan, c, p] - m1[p][:CHUNK]).astype(BF16) for p in range(npairs)]
            m2 = [_dot(ik_ref[slot_scan, c, p], block_diag(v_new[p][:, :GDN_HEAD_DIM], v_new[p][:, GDN_HEAD_DIM:], zc))
                  for p in range(npairs)]
            for p in range(npairs):
                ps = slice(2 * p * GDN_HEAD_DIM, 2 * (p + 1) * GDN_HEAD_DIM)
                o_ref[0, pl.ds(r0, CHUNK), ps] = m1[p][CHUNK:] + m2[p][:CHUNK]
                s_ref[d, p] = states[p] * eg_ref[slot_scan, c, p][0:1, :] + m2[p][CHUNK:]
            yield

    def iteration(it, first, second, scan):
        live = []
        for d in range(2):
            if first:
                live.append(build_first(it, d))
            if second:
                if isinstance(it, int):
                    g2, slot2 = (groups - 1, slot_scan) if it == 0 else (it - 1, slot_build)
                else:
                    g2 = jnp.where(it == 0, groups - 1, it - 1)
                    slot2 = jnp.where(it == 0, slot_scan, slot_build)
                live.append(build_second(g2, slot2, d))
            if scan:
                live.append(scan_chunks(it, d))
        while live:
            live = [gen for gen in live if next(gen, StopIteration) is not StopIteration]

    def loop(lo, first, second, scan):
        def body(it, carry):
            iteration(it, first, second, scan)
            return carry
        lax.fori_loop(lo, groups, body, 0)

    @pl.when(j == 0)
    def _():
        s_ref[...] = jnp.zeros(s_ref.shape, F32)
        iteration(0, True, False, False)
        loop(1, True, True, False)

    @pl.when((j > 0) & (j < nblk))
    def _():
        loop(0, True, True, True)

    @pl.when(j == nblk)
    def _():
        iteration(0, False, True, True)
        loop(1, False, False, True)


def _gdn_scan(q, k, v, gcol, grow, to_bf16=(), *, ts):
    b, nh, s, hd = q.shape
    w = nh * hd
    nblk = s // ts
    nc = ts // CHUNK
    assert nblk >= 2 and (nc // CHUNKS_PER_STEP) % 2 == 0
    order = (lambda t: t, lambda t: nblk - 1 - t)
    in_specs, out_specs, scratch = [], [], [pltpu.VMEM((2, nh // 2, hd, 2 * hd), F32)]
    for d in range(2):
        src = lambda j, d=d: order[d](jnp.minimum(j, nblk - 1))
        dst = lambda j, d=d: order[d](jnp.maximum(j - 1, 0))
        in_specs += [pl.BlockSpec((1, nh, ts, hd), lambda i, j, src=src: (i, 0, src(j), 0))] * 3
        in_specs += [pl.BlockSpec((1, ts, LANES), lambda i, j, src=src: (i, src(j), 0)),
                     pl.BlockSpec((1, nc, 8, 2 * CHUNK), lambda i, j, src=src: (i, src(j), 0, 0))]
        out_specs.append(pl.BlockSpec((1, ts, w), lambda i, j, dst=dst: (i, dst(j), 0)))
        nprob = CHUNKS_PER_STEP * (nh // 2)
        scratch += [pltpu.VMEM((2, nc, nh // 2, 2 * CHUNK, 2 * hd), BF16),
                    pltpu.VMEM((2, nc, nh // 2, CHUNK, 2 * hd), F32),
                    pltpu.VMEM((2, nc, nh // 2, CHUNK + hd, 2 * CHUNK), BF16),
                    pltpu.VMEM((2, nc, nh // 2, 8, 2 * hd), F32),
                    pltpu.VMEM((nprob, 2, CHUNK, 2 * CHUNK), F32),
                    pltpu.VMEM((2, nprob, 2, CHUNK, 2 * hd), F32)]
    steps = b * (nblk + 1)
    cast_specs = []
    for arr in to_bf16:
        rows, cols = arr.shape
        nb = max(n for n in range(1, steps + 1) if rows % n == 0 and (rows // n) % 16 == 0)
        cast_specs.append(pl.BlockSpec((rows // nb, cols),
                                       lambda i, j, nb=nb: (jnp.minimum(i * (nblk + 1) + j, nb - 1), 0)))
    return pl.pallas_call(
        functools.partial(_gdn_scan_kernel, nc=nc, nblk=nblk, ncast=len(to_bf16)),
        grid=(b, nblk + 1),
        in_specs=in_specs + cast_specs,
        out_specs=out_specs + cast_specs,
        out_shape=[jax.ShapeDtypeStruct((b, s, w), F32)] * 2
                  + [jax.ShapeDtypeStruct(arr.shape, BF16) for arr in to_bf16],
        scratch_shapes=scratch,
        compiler_params=_cparams(("arbitrary", "arbitrary")),
        name="gdn_scan",
    )(*([q, k, v, gcol, grow] * 2), *to_bf16)


SWA_TQ = 128
SWA_TK = SWA_TQ + 2 * BAND_RADIUS
SWA_HALO = BAND_RADIUS * max(DILATIONS)
COPY_ROWS = 512
SWA_TILES_PER_STEP = 2


def _swa_kernel(q_ref, k_ref, v_ref, bias_ref, o_ref,
                kn_ref, vp_ref, m0_ref, m1_ref, l_ref, acc_ref, bias_scr, *, seq, qb):
    qi = pl.program_id(2)
    qn_ref = q_ref.at[0]
    lane = lax.broadcasted_iota(jnp.int32, (1, LANES), 1)
    low = lane < SWA_HEAD_DIM
    nt = (((1,), (1,)), ((), ()))

    @pl.when(qi == 0)
    def _():
        for pi in range(len(DILATIONS)):
            for hh in range(2):
                full = jnp.broadcast_to(bias_ref[0, pi, hh], (SWA_TQ, SWA_TK))
                bias_scr[pi, hh] = pltpu.roll(full, 0, 1, stride=1, stride_axis=0)

        zeros = jnp.zeros((SWA_HALO, LANES), F32)
        for ref in (kn_ref, vp_ref):
            ref[0:SWA_HALO, :] = zeros
            ref[SWA_HALO + seq:2 * SWA_HALO + seq, :] = zeros

        def copy_rows(i, carry):
            r0 = pl.multiple_of(i * COPY_ROWS, COPY_ROWS)
            kn_ref[pl.ds(SWA_HALO + r0, COPY_ROWS), :] = k_ref[0, pl.ds(r0, COPY_ROWS), :]
            vp_ref[pl.ds(SWA_HALO + r0, COPY_ROWS), :] = v_ref[0, pl.ds(r0, COPY_ROWS), :]
            return carry

        lax.fori_loop(0, seq // COPY_ROWS, copy_rows, 0)

    kcol = lax.broadcasted_iota(jnp.int32, (1, SWA_TK), 1)

    def rows(ref, start, size, stride):
        if stride == 1:
            return ref[pl.ds(start, size), :]
        return ref[pl.ds(start, size, stride=stride), :]

    def put(ref, start, size, stride, val):
        if stride == 1:
            ref[pl.ds(start, size), :] = val
        else:
            ref[pl.ds(start, size, stride=stride), :] = val

    def logits(pi, dil, g):
        sub_len = seq // dil
        tiles_per_res = qb // dil // SWA_TQ
        ss = []
        for u in range(SWA_TILES_PER_STEP):
            res, t = divmod(g * SWA_TILES_PER_STEP + u, tiles_per_res)
            tau0 = qi * (qb // dil) + t * SWA_TQ
            loc0 = res + dil * (t * SWA_TQ)
            krow = SWA_HALO + qi * qb + loc0 - dil * BAND_RADIUS
            qt = rows(qn_ref, qi * qb + loc0, SWA_TQ, dil)
            kt = rows(kn_ref, krow, SWA_TK, dil).astype(BF16)
            at_end = t in (0, tiles_per_res - 1)
            kidx = kcol + (tau0 - BAND_RADIUS)
            kvalid = (kidx >= 0) & (kidx < sub_len)
            for hh in range(2):
                mine = low if hh == 0 else jnp.logical_not(low)
                qh = jnp.where(mine, qt, 0.0).astype(BF16)
                s = lax.dot_general(qh, kt, nt, preferred_element_type=F32) + bias_scr[pi, hh]
                ss.append(jnp.where(kvalid, s, NEG_BIG) if at_end else s)
        return ss

    def accumulate(dil, g, ss, first):
        tiles_per_res = qb // dil // SWA_TQ
        loc, vts, old = [], [], []
        for u in range(SWA_TILES_PER_STEP):
            res, t = divmod(g * SWA_TILES_PER_STEP + u, tiles_per_res)
            loc0 = res + dil * (t * SWA_TQ)
            vt = rows(vp_ref, SWA_HALO + qi * qb + loc0 - dil * BAND_RADIUS, SWA_TK, dil)
            vts.append((jnp.where(low, vt, 1.0).astype(BF16), jnp.where(low, 1.0, vt).astype(BF16)))
            loc.append(loc0)
            if not first:
                old.append(tuple(rows(ref, loc0, SWA_TQ, dil) for ref in (m0_ref, m1_ref, l_ref, acc_ref)))
        m_new, alpha, ps = [], [], []
        for n, s in enumerate(ss):
            u, hh = divmod(n, 2)
            mt = jnp.max(s, axis=-1, keepdims=True)
            if first:
                mn = jnp.broadcast_to(mt, (SWA_TQ, LANES))
            else:
                mo = old[u][hh]
                mn = jnp.maximum(mo, mt)
                alpha.append(jnp.exp2(mo - mn))
            m_new.append(mn)
            ps.append(jnp.exp2(s - jnp.concatenate([mn, mn], axis=1)).astype(BF16))
        pv = [_dot(p, vts[n // 2][n % 2]) for n, p in enumerate(ps)]
        for u in range(SWA_TILES_PER_STEP):
            pv0, pv1 = pv[2 * u], pv[2 * u + 1]
            put(m0_ref, loc[u], SWA_TQ, dil, m_new[2 * u])
            put(m1_ref, loc[u], SWA_TQ, dil, m_new[2 * u + 1])
            lsum = jnp.where(low, pv1, pv0)
            pvv = jnp.where(low, pv0, pv1)
            if first:
                put(l_ref, loc[u], SWA_TQ, dil, lsum)
                put(acc_ref, loc[u], SWA_TQ, dil, pvv)
            else:
                a0, a1 = alpha[2 * u], alpha[2 * u + 1]
                put(l_ref, loc[u], SWA_TQ, dil, old[u][2] * jnp.where(low, a1, a0) + lsum)
                put(acc_ref, loc[u], SWA_TQ, dil, old[u][3] * jnp.where(low, a0, a1) + pvv)

    order = sorted(range(len(DILATIONS)), key=lambda pi: -DILATIONS[pi])
    units = [(pi, DILATIONS[pi], g) for pi in order for g in range(qb // SWA_TQ // SWA_TILES_PER_STEP)]
    pending = logits(*units[0])
    for k, (pi, dil, g) in enumerate(units):
        nxt = logits(*units[k + 1]) if k + 1 < len(units) else None
        accumulate(dil, g, pending, first=pi == order[0])
        pending = nxt

    o_ref[0] = acc_ref[...] / pltpu.roll(l_ref[...], SWA_HEAD_DIM, 1)


def _t5_bucket(rel):
    nb = REL_BUCKETS // 2
    bucket = (rel > 0).astype(np.int32) * nb
    n = np.abs(rel)
    max_exact = nb // 2
    large = max_exact + (np.log(np.maximum(n, 1) / max_exact)
                         / math.log(REL_MAX_DISTANCE / max_exact) * (nb - max_exact)).astype(np.int32)
    large = np.minimum(large, nb - 1)
    return (bucket + np.where(n < max_exact, n, large)).astype(np.int32)


def _band_bias(rel_bias):
    rel = np.arange(2 * BAND_RADIUS + 1) - BAND_RADIUS
    rows = []
    for dil in DILATIONS:
        inside = jnp.transpose(rel_bias[_t5_bucket(rel * dil)]) * LOG2E
        rows.append(jnp.pad(inside, ((0, 0), (0, SWA_TK - inside.shape[1])), constant_values=NEG_BIG))
    bias = jnp.stack(rows, axis=0)
    bias = bias.reshape(len(DILATIONS), SWA_HEADS // 2, 2, 1, SWA_TK)
    return jnp.transpose(bias, (1, 0, 2, 3, 4)).astype(F32)


def _swa(qkvb, bias, *, qb):
    b, s, _ = qkvb.shape
    pairs = SWA_HEADS // 2
    assert s % qb == 0 and qb % (max(DILATIONS) * SWA_TQ) == 0
    col = lambda base: pl.BlockSpec((1, s, LANES), lambda i, p, j: (i, 0, base + p))
    padded = s + 2 * SWA_HALO
    return pl.pallas_call(
        functools.partial(_swa_kernel, seq=s, qb=qb),
        grid=(b, pairs, s // qb),
        in_specs=[col(0), col(pairs), col(2 * pairs),
                  pl.BlockSpec((1,) + bias.shape[1:], lambda i, p, j: (p, 0, 0, 0, 0))],
        out_specs=pl.BlockSpec((1, qb, LANES), lambda i, p, j: (i, j, p)),
        out_shape=jax.ShapeDtypeStruct((b, s, SWA_WIDTH), F32),
        scratch_shapes=[pltpu.VMEM((padded, LANES), F32), pltpu.VMEM((padded, LANES), F32)]
                       + [pltpu.VMEM((qb, LANES), F32)] * 4
                       + [pltpu.VMEM((len(DILATIONS), 2, SWA_TQ, SWA_TK), F32)],
        compiler_params=_cparams(("arbitrary", "arbitrary", "arbitrary")),
        name="swa",
    )(qkvb, qkvb, qkvb, bias)


FFN_TM = 512
FFN_FCHUNK = 256
SCAN_TS = 512
SWA_QB = 2048


def _lane_row(vals, copies_at):
    row = jnp.zeros((LANES,), F32)
    for off in copies_at:
        row = row.at[off:off + vals.shape[0]].set(vals)
    return row[None, :]


def kernel(x, ffn1_norm, ffn1_w_gate, ffn1_w_up, ffn1_w_down, mix_norm, w_in, conv_w, a_log, dt_bias, gdn_norm_w, q_norm_w, k_norm_w, rel_bias, w_out, ffn2_norm, ffn2_w_gate, ffn2_w_up, ffn2_w_down, final_norm):
    b, s, d = x.shape
    n = b * s
    x2d = x.reshape(n, d)
    ng = 2 * GDN_HEADS
    c_z = 3 * GDN_WIDTH
    c_a = c_z + GDN_WIDTH
    c_b = c_a + ng
    c_qkvb = c_b + ng
    bias = _band_bias(rel_bias)
    for l in range(ffn1_norm.shape[0]):
        wl = w_in[l]
        w_a = wl[:, c_a:c_b]
        w_gates = jnp.concatenate([w_a, wl[:, c_b:c_qkvb], w_a, w_a, jnp.zeros((d, LANES - 4 * ng), F32)], axis=1)
        win = (wl[:, :c_a].astype(BF16), w_gates.astype(BF16), wl[:, c_qkvb:].astype(BF16))
        copies = (0, 2 * ng, 3 * ng)
        alog = _lane_row(a_log[l].reshape(ng), copies)
        dtb = _lane_row(dt_bias[l].reshape(ng), copies)
        x1, z, qkvb, q, k, v, gcol, grow = _ffn1_proj(
            x2d, ffn1_norm[l][None, :], ffn1_w_gate[l].astype(BF16), ffn1_w_up[l].astype(BF16),
            ffn1_w_down[l].astype(BF16), mix_norm[l][None, :], win,
            jnp.tile(q_norm_w[l], 2)[None, :], jnp.tile(k_norm_w[l], 2)[None, :],
            jnp.transpose(conv_w[l]), alog, dtb, batch=b, tm=FFN_TM, fchunk=FFN_FCHUNK)
        o_fwd, o_bwd, wout_bf, wg2, wu2, wd2 = _gdn_scan(
            q, k, v, gcol.reshape(b, s, LANES), grow.reshape(b, s // CHUNK, 8, 2 * CHUNK),
            (w_out[l], ffn2_w_gate[l], ffn2_w_up[l], ffn2_w_down[l]), ts=SCAN_TS)
        attn = _swa(qkvb.reshape(b, s, -1), bias, qb=SWA_QB)
        x2d = _out_ffn2(
            x1, o_fwd.reshape(n, -1), o_bwd.reshape(n, -1), z, attn.reshape(n, -1),
            gdn_norm_w[l][None, :], wout_bf, ffn2_norm[l][None, :], wg2, wu2, wd2,
            final_norm[l][None, :], tm=FFN_TM, fchunk=FFN_FCHUNK)
    return x2d.reshape(b, s, d)
```
